```python
import math
import jax
import jax.numpy as jnp
from jax import lax
import numpy as np

D_MODEL = 1024
BATCH = 8
SEQ = 2048
DEPTH = 4
DEC_BATCH = 128
DEC_SEQ = 8
PAST_LEN = 2048
PAGE_SIZE = 128

HEAD_DIM = 64
NSA_HEADS = D_MODEL // (2 * HEAD_DIM)
NSA_KV = 2
NSA_QPG = NSA_HEADS // NSA_KV
NSA_WIDTH = NSA_HEADS * HEAD_DIM
NSA_KV_WIDTH = NSA_KV * HEAD_DIM
CMP_LEN = 32
CMP_STRIDE = 16
CMP_HIDDEN = 2 * HEAD_DIM
SEL_LEN = 64
N_SELECT = 16
WINDOW = 512
NSA_CHUNK = 64
FORCE_SCORE = 1e9
RWKV_HEADS = D_MODEL // (2 * HEAD_DIM)
RWKV_WIDTH = RWKV_HEADS * HEAD_DIM
DECAY_LORA = 64
AAA_LORA = 64
GATE_LORA = 128
RWKV_PROJ = 3 * RWKV_WIDTH + DECAY_LORA + AAA_LORA + GATE_LORA
NSA_PROJ = NSA_WIDTH + 6 * NSA_KV_WIDTH + 3 * NSA_HEADS
EVEN_PROJ = NSA_PROJ + RWKV_PROJ
MIX_WIDTH = NSA_WIDTH + RWKV_WIDTH
CONV_WIDTH = D_MODEL
CONV_K = 3
D_FF = ((8 * D_MODEL // 3 + 255) // 256) * 256
FFN_CONV_K = 3
NUM_BUCKETS = 32
MAX_DISTANCE = 128
N_EVEN = (DEPTH + 1) // 2
N_ODD = DEPTH // 2
RMS_EPS = 1e-6
GN_EPS = 64e-5

kernel_name = 'nsa_rwkv7_shortconv_convffn_decode_step'


def rms_norm(x, g):
    xf = x.astype(jnp.float32)
    y = xf * lax.rsqrt(jnp.mean(xf * xf, axis=-1, keepdims=True) + RMS_EPS)
    return (y * g.astype(jnp.float32)).astype(x.dtype)


def masked_softmax(s, mask):
    s = jnp.where(mask, s, -jnp.inf)
    m = jnp.max(s, axis=-1, keepdims=True)
    e = jnp.exp(s - jnp.where(jnp.isfinite(m), m, 0.0))
    den = jnp.sum(e, axis=-1, keepdims=True)
    return e / jnp.where(den > 0, den, 1.0)


def rel_bucket(dist):
    n = jnp.maximum(dist, 0)
    max_exact = NUM_BUCKETS // 2
    nf = jnp.maximum(n, 1).astype(jnp.float32)
    large = max_exact + (jnp.log(nf / max_exact) / math.log(MAX_DISTANCE / max_exact)
                         * (NUM_BUCKETS - max_exact)).astype(jnp.int32)
    large = jnp.minimum(large, NUM_BUCKETS - 1)
    return jnp.where(n < max_exact, n, large)


def causal_dwconv(u, prev, w):
    kw = w.shape[0]
    t = u.shape[1]
    ext = jnp.concatenate([prev.astype(u.dtype), u], axis=1)
    y = ext[:, 0:t] * w[0]
    for j in range(1, kw):
        y = y + ext[:, j:j + t] * w[j]
    return y, ext[:, t:]


def compress(k, pe, w1, w2):
    b, t = k.shape[:2]
    r_n = CMP_LEN // CMP_STRIDE
    m = t // CMP_STRIDE
    nc = m - r_n + 1
    seg = k[:, :m * CMP_STRIDE].reshape(b, m, CMP_STRIDE, NSA_KV, HEAD_DIM)
    w1r = w1.reshape(r_n, CMP_STRIDE, HEAD_DIM, CMP_HIDDEN)
    part = jnp.einsum('bmjgd,rjde->rbmge', seg, w1r)
    hid = jnp.einsum('jd,jde->e', pe, w1) + part[0, :, 0:nc]
    for r in range(1, r_n):
        hid = hid + part[r, :, r:r + nc]
    return jnp.einsum('bnge,ed->bngd', jax.nn.gelu(hid), w2)


def nsa_chunk(q, gate, qpos, kc, vc, cend, ks_blk, vs_blk, kw, vw, wpos, overlap, rel_bias):
    b, c = q.shape[:2]
    scale = HEAD_DIM ** -0.5
    tb = rel_bias.astype(jnp.float32).reshape(NUM_BUCKETS, NSA_KV, NSA_QPG)
    dist_c = qpos[:, None] - cend[None, :]
    s_c = jnp.einsum('bcgqd,bngd->bcgqn', q, kc).astype(jnp.float32) * scale
    s_c = s_c + jnp.moveaxis(tb[rel_bucket(dist_c)], 1, -1)[None]
    p_c = masked_softmax(s_c, (dist_c >= 0)[None, :, None, None, :])
    o_c = jnp.einsum('bcgqn,bngd->bcgqd', p_c.astype(vc.dtype), vc)
    ns = ks_blk.shape[2]
    imp = jnp.einsum('bcgqn,ns->bcgs', p_c, overlap)
    blk = jnp.arange(ns, dtype=jnp.int32)[None, :]
    cur = (qpos // SEL_LEN)[:, None]
    forced = (blk == 0) | (blk == cur) | (blk == cur - 1)
    future = blk * SEL_LEN > qpos[:, None]
    imp = jnp.where(forced[None, :, None], FORCE_SCORE, imp)
    imp = jnp.where(future[None, :, None], -FORCE_SCORE, imp)
    n_top = min(N_SELECT, ns)
    _, idx = lax.top_k(imp, n_top)
    bi = jnp.arange(b)[:, None, None, None]
    gi = jnp.arange(NSA_KV)[None, None, :, None]
    n_keys = n_top * SEL_LEN
    k_sel = ks_blk[bi, gi, idx].reshape(b, c, NSA_KV, n_keys, HEAD_DIM)
    v_sel = vs_blk[bi, gi, idx].reshape(b, c, NSA_KV, n_keys, HEAD_DIM)
    kpos = (idx[..., None] * SEL_LEN + jnp.arange(SEL_LEN, dtype=jnp.int32)).reshape(b, c, NSA_KV, n_keys)
    dist_s = qpos[None, :, None, None] - kpos
    s_s = jnp.einsum('bcgqd,bcgkd->bcgqk', q, k_sel).astype(jnp.float32) * scale
    s_s = s_s + jnp.swapaxes(tb[rel_bucket(dist_s), gi], -1, -2)
    p_s = masked_softmax(s_s, (dist_s >= 0)[:, :, :, None, :])
    o_s = jnp.einsum('bcgqk,bcgkd->bcgqd', p_s.astype(v_sel.dtype), v_sel)
    dist_w = qpos[:, None] - wpos[None, :]
    s_w = jnp.einsum('bcgqd,bwgd->bcgqw', q, kw).astype(jnp.float32) * scale
    s_w = s_w + jnp.moveaxis(tb[rel_bucket(dist_w)], 1, -1)[None]
    mask_w = (dist_w >= 0) & (dist_w < WINDOW) & (wpos >= 0)[None, :]
    p_w = masked_softmax(s_w, mask_w[None, :, None, None, :])
    o_w = jnp.einsum('bcgqw,bwgd->bcgqd', p_w.astype(vw.dtype), vw)
    g = jax.nn.sigmoid(gate.astype(jnp.float32))
    o = g[..., 0:1] * o_c + g[..., 1:2] * o_s + g[..., 2:3] * o_w
    return o.astype(q.dtype)


def nsa_attend(q, gates, k_cmp, v_cmp, k_slc, v_slc, k_win, v_win, q0, w0, chunk, banded, phi, rel_bias):
    b, tq = q.shape[:2]
    pe, w1, w2, kn_cmp = phi
    kc = rms_norm(compress(k_cmp, pe[0], w1[0], w2[0]), kn_cmp)
    vc = compress(v_cmp, pe[1], w1[1], w2[1])
    nc = kc.shape[1]
    cend = jnp.arange(nc, dtype=jnp.int32) * CMP_STRIDE + (CMP_LEN - 1)
    t_all = k_slc.shape[1]
    ns = -(-t_all // SEL_LEN)

    def blockify(a):
        a = jnp.pad(a, ((0, 0), (0, ns * SEL_LEN - t_all), (0, 0), (0, 0)))
        return a.reshape(b, ns, SEL_LEN, NSA_KV, HEAD_DIM).transpose(0, 3, 1, 2, 4)

    ks_blk, vs_blk = blockify(k_slc), blockify(v_slc)
    c0 = jnp.arange(nc)[:, None] * CMP_STRIDE
    s0 = jnp.arange(ns)[None, :] * SEL_LEN
    overlap = ((c0 < s0 + SEL_LEN) & (c0 + CMP_LEN > s0)).astype(jnp.float32)
    n_chunks = tq // chunk

    def to_chunks(a):
        return jnp.swapaxes(a.reshape((b, n_chunks, chunk) + a.shape[2:]), 0, 1)

    qc = to_chunks(q.reshape(b, tq, NSA_KV, NSA_QPG, HEAD_DIM))
    gc = to_chunks(gates.reshape(b, tq, NSA_KV, NSA_QPG, 3))
    if banded:
        pad = ((0, 0), (WINDOW, 0), (0, 0), (0, 0))
        kw_src, vw_src = jnp.pad(k_win, pad), jnp.pad(v_win, pad)
        span = WINDOW + chunk
    else:
        kw_src, vw_src = k_win, v_win
        span = k_win.shape[1]

    def body(args):
        q_i, g_i, c_i = args
        start = c_i * chunk
        qpos = q0 + start + jnp.arange(chunk, dtype=jnp.int32)
        if banded:
            kw_i = lax.dynamic_slice_in_dim(kw_src, start, span, axis=1)
            vw_i = lax.dynamic_slice_in_dim(vw_src, start, span, axis=1)
            wpos = w0 + start - WINDOW + jnp.arange(span, dtype=jnp.int32)
        else:
            kw_i, vw_i = kw_src, vw_src
            wpos = w0 + jnp.arange(span, dtype=jnp.int32)
        return nsa_chunk(q_i, g_i, qpos, kc, vc, cend, ks_blk, vs_blk, kw_i, vw_i, wpos, overlap, rel_bias)

    out = lax.map(body, (qc, gc, jnp.arange(n_chunks, dtype=jnp.int32)))
    return jnp.swapaxes(out, 0, 1).reshape(b, tq, NSA_WIDTH)


def rwkv_mix(z, z_prev, s0, li, P):
    b, t, _ = z.shape
    f32 = jnp.float32
    shifted = jnp.concatenate([z_prev[:, None].astype(z.dtype), z[:, :-1]], axis=1)
    zz = z + (shifted - z) * P['rwkv_mu'][li]
    offs = [RWKV_WIDTH, 2 * RWKV_WIDTH, 3 * RWKV_WIDTH, 3 * RWKV_WIDTH + DECAY_LORA,
            3 * RWKV_WIDTH + DECAY_LORA + AAA_LORA]
    r, k, v, zw, za, zg = jnp.split(zz, offs, axis=-1)
    w_log = -jax.nn.softplus(-(P['rwkv_w0'][li] + jnp.tanh(zw) @ P['rwkv_w2'][li]).astype(f32)) - 0.5
    decay = jnp.exp(-jnp.exp(w_log))
    a = jax.nn.sigmoid((P['rwkv_a0'][li] + za @ P['rwkv_a2'][li]).astype(f32))
    g = jax.nn.sigmoid(zg) @ P['rwkv_g2'][li]

    def hd(a_):
        return a_.reshape(b, t, RWKV_HEADS, HEAD_DIM)

    kk = hd((k * P['rwkv_kk'][li]).astype(f32))
    kk = kk * lax.rsqrt(jnp.maximum(jnp.sum(kk * kk, axis=-1, keepdims=True), 1e-24))
    k_mod = hd(k.astype(f32) * (1.0 + (a - 1.0) * P['rwkv_ka'][li].astype(f32)))
    r_h, v_h, a_h, w_h = hd(r.astype(f32)), hd(v.astype(f32)), hd(a), hd(decay)

    def step(S, inp):
        r_t, w_t, k_t, v_t, kk_t, a_t = inp
        sa = jnp.einsum('bhij,bhj->bhi', S, -kk_t)
        S = (S * w_t[:, :, None, :] + sa[..., None] * (kk_t * a_t)[:, :, None, :]
             + v_t[..., None] * k_t[:, :, None, :])
        return S, jnp.einsum('bhij,bhj->bhi', S, r_t)

    xs = tuple(jnp.swapaxes(a_, 0, 1) for a_ in (r_h, w_h, k_mod, v_h, kk, a_h))
    s_t, y = lax.scan(step, s0.astype(f32), xs)
    y = jnp.swapaxes(y, 0, 1)
    mu = jnp.mean(y, axis=-1, keepdims=True)
    var = jnp.mean(jnp.square(y - mu), axis=-1, keepdims=True)
    y = (y - mu) * lax.rsqrt(var + GN_EPS)
    y = (y * P['rwkv_ln_w'][li].astype(f32).reshape(RWKV_HEADS, HEAD_DIM)
         + P['rwkv_ln_b'][li].astype(f32).reshape(RWKV_HEADS, HEAD_DIM))
    y = y + jnp.sum(r_h * k_mod * P['rwkv_rk'][li].astype(f32), axis=-1, keepdims=True) * v_h
    y = y.reshape(b, t, RWKV_WIDTH).astype(z.dtype) * g
    return y, s_t, z[:, -1]


def nsa_rwkv_mixer(h, li, P, st):
    b, t, _ = h.shape
    z = jnp.einsum('btd,de->bte', h, P['w_in_even'][li])
    sizes = [NSA_WIDTH] + [NSA_KV_WIDTH] * 6 + [3 * NSA_HEADS]
    offs, acc = [], 0
    for s in sizes:
        acc += s
        offs.append(acc)
    q, k_c, v_c, k_s, v_s, k_w, v_w, gates, z_rwkv = jnp.split(z, offs, axis=-1)

    def kvh(a_):
        return a_.reshape(b, t, NSA_KV, HEAD_DIM)

    q = rms_norm(q.reshape(b, t, NSA_HEADS, HEAD_DIM), P['q_norm'][li])
    k_c, v_c, v_s, v_w = kvh(k_c), kvh(v_c), kvh(v_s), kvh(v_w)
    k_s = rms_norm(kvh(k_s), P['k_norm'][li, 1])
    k_w = rms_norm(kvh(k_w), P['k_norm'][li, 2])
    phi = (P['phi_pe'][li], P['phi_w1'][li], P['phi_w2'][li], P['k_norm'][li, 0])
    if st is None:
        nsa = nsa_attend(q, gates, k_c, v_c, k_s, v_s, k_w, v_w, 0, 0, NSA_CHUNK, True, phi, P['rel_bias'])
        win_k, win_v = k_w, v_w
        z_prev = jnp.zeros((b, RWKV_PROJ), h.dtype)
        s0 = jnp.zeros((b, RWKV_HEADS, HEAD_DIM, HEAD_DIM), jnp.float32)
    else:
        pt = st['page_table']
        past = pt.shape[1] * PAGE_SIZE

        def paged(pool):
            rows = pool[pt, li]
            return rows.reshape(b, past, 2, NSA_KV, HEAD_DIM).astype(h.dtype)

        def cat(a_, b_):
            return jnp.concatenate([a_, b_], axis=1)

        pc, ps = paged(st['cmp']), paged(st['slc'])
        buf = st['win'][:, li].astype(h.dtype)
        wb = buf.shape[1]
        win_k, win_v = cat(buf[:, :, 0], k_w), cat(buf[:, :, 1], v_w)
        nsa = nsa_attend(q, gates, cat(pc[:, :, 0], k_c), cat(pc[:, :, 1], v_c),
                         cat(ps[:, :, 0], k_s), cat(ps[:, :, 1], v_s), win_k, win_v,
                         past, past - wb, 1, False, phi, P['rel_bias'])
        z_prev = st['shift'][:, li]
        s0 = st['wkv'][:, li]
    n_keep = min(WINDOW, win_k.shape[1])
    win_rows = jnp.stack([win_k, win_v], axis=2)[:, -n_keep:]
    rw, s_t, z_last = rwkv_mix(z_rwkv, z_prev, s0, li, P)
    out = jnp.einsum('bte,ed->btd', jnp.concatenate([nsa, rw], axis=-1), P['w_out_even'][li])
    return out, (jnp.stack([k_c, v_c], axis=2), jnp.stack([k_s, v_s], axis=2), win_rows, s_t, z_last)


def conv_mixer(h, li, P, st):
    b = h.shape[0]
    z = jnp.einsum('btd,de->bte', h, P['w_in_odd'][li])
    bg, cg, xi = jnp.split(z, [CONV_WIDTH, 2 * CONV_WIDTH], axis=-1)
    prev = jnp.zeros((b, CONV_K - 1, CONV_WIDTH), h.dtype) if st is None else st['conv'][:, li]
    c, new = causal_dwconv(cg * xi, prev, P['conv_w'][li])
    return jnp.einsum('bte,ed->btd', bg * c, P['w_out_odd'][li]), new


def conv_ffn(h, l, P, st):
    b = h.shape[0]
    a, u = jnp.split(jnp.einsum('btd,df->btf', h, P['ffn_up'][l]), [D_FF], axis=-1)
    prev = jnp.zeros((b, FFN_CONV_K - 1, D_FF), h.dtype) if st is None else st['ffn'][:, l]
    ac, new = causal_dwconv(a, prev, P['ffn_conv'][l])
    return jnp.einsum('btf,fd->btd', jax.nn.silu(ac) * u, P['ffn_down'][l]), new


def trunk(x, P, st):
    cmp_r, slc_r, win_r, wkv_r, sh_r, conv_r, ffn_r = [], [], [], [], [], [], []
    for l in range(DEPTH):
        li = l // 2
        h = rms_norm(x, P['norm_mix'][l])
        if l % 2 == 0:
            o, (c_, s_, w_, S_, sh_) = nsa_rwkv_mixer(h, li, P, st)
            cmp_r.append(c_)
            slc_r.append(s_)
            win_r.append(w_)
            wkv_r.append(S_)
            sh_r.append(sh_)
        else:
            o, cs = conv_mixer(h, li, P, st)
            conv_r.append(cs)
        x = x + o
        o, fs = conv_ffn(rms_norm(x, P['norm_ffn'][l]), l, P, st)
        ffn_r.append(fs)
        x = x + o
    stk = lambda a_: jnp.stack(a_, axis=1)
    return x, (stk(cmp_r), stk(slc_r), stk(win_r), stk(wkv_r), stk(sh_r), stk(conv_r), stk(ffn_r))


def setup_inputs(seed: int = 0) -> dict:
    key = jax.random.key(seed)
    keys = jax.random.split(key, 64)
    counter = iter(range(64))

    def nk():
        return keys[next(counter)]

    def nrm(shape, scale=1.0):
        return jax.random.normal(nk(), shape, jnp.float32) * scale

    def gain(shape, base=1.0):
        return base + 0.05 * jax.random.normal(nk(), shape, jnp.float32)

    def unif(shape, lo, hi):
        return jax.random.uniform(nk(), shape, jnp.float32, lo, hi)

    n_pages = PAST_LEN // PAGE_SIZE
    used = DEC_BATCH * n_pages
    n_pool = used + max(used // 4, 1)
    wb = min(WINDOW, PAST_LEN)
    d = D_MODEL
    return {
        'x_prompt': nrm((BATCH, SEQ, d)),
        'x_sample': nrm((DEC_BATCH, DEC_SEQ, d)),
        'cache_cmp_kv': nrm((n_pool, N_EVEN, PAGE_SIZE, 2, NSA_KV, HEAD_DIM)),
        'cache_slc_kv': nrm((n_pool, N_EVEN, PAGE_SIZE, 2, NSA_KV, HEAD_DIM)),
        'cache_win_kv': nrm((DEC_BATCH, N_EVEN, wb, 2, NSA_KV, HEAD_DIM)),
        'state_rwkv_wkv': nrm((DEC_BATCH, N_EVEN, RWKV_HEADS, HEAD_DIM, HEAD_DIM), 0.3),
        'state_rwkv_shift': nrm((DEC_BATCH, N_EVEN, RWKV_PROJ)),
        'state_conv': nrm((DEC_BATCH, N_ODD, CONV_K - 1, CONV_WIDTH)),
        'state_ffn_conv': nrm((DEC_BATCH, DEPTH, FFN_CONV_K - 1, D_FF)),
        'page_table': jax.random.permutation(nk(), n_pool)[:used].reshape(DEC_BATCH, n_pages).astype(jnp.int32),
        'norm_mix': gain((DEPTH, d)),
        'norm_ffn': gain((DEPTH, d)),
        'rel_bias': nrm((NUM_BUCKETS, NSA_HEADS), 0.5),
        'w_in_even': nrm((N_EVEN, d, EVEN_PROJ), d ** -0.5),
        'w_out_even': nrm((N_EVEN, MIX_WIDTH, d), MIX_WIDTH ** -0.5),
        'q_norm': gain((N_EVEN, HEAD_DIM)),
        'k_norm': gain((N_EVEN, 3, HEAD_DIM)),
        'phi_pe': nrm((N_EVEN, 2, CMP_LEN, HEAD_DIM), 0.5),
        'phi_w1': nrm((N_EVEN, 2, CMP_LEN, HEAD_DIM, CMP_HIDDEN), (CMP_LEN * HEAD_DIM) ** -0.5),
        'phi_w2': nrm((N_EVEN, 2, CMP_HIDDEN, HEAD_DIM), CMP_HIDDEN ** -0.5),
        'rwkv_mu': unif((N_EVEN, RWKV_PROJ), 0.0, 1.0),
        'rwkv_w0': unif((N_EVEN, RWKV_WIDTH), -5.0, -0.5),
        'rwkv_w2': nrm((N_EVEN, DECAY_LORA, RWKV_WIDTH), 0.1 * DECAY_LORA ** -0.5),
        'rwkv_a0': nrm((N_EVEN, RWKV_WIDTH), 0.1),
        'rwkv_a2': nrm((N_EVEN, AAA_LORA, RWKV_WIDTH), AAA_LORA ** -0.5),
        'rwkv_g2': nrm((N_EVEN, GATE_LORA, RWKV_WIDTH), GATE_LORA ** -0.5),
        'rwkv_kk': gain((N_EVEN, RWKV_WIDTH), 0.85),
        'rwkv_ka': gain((N_EVEN, RWKV_WIDTH)),
        'rwkv_rk': nrm((N_EVEN, RWKV_HEADS, HEAD_DIM), 0.1),
        'rwkv_ln_w': gain((N_EVEN, RWKV_WIDTH)),
        'rwkv_ln_b': nrm((N_EVEN, RWKV_WIDTH), 0.02),
        'w_in_odd': nrm((N_ODD, d, 3 * CONV_WIDTH), d ** -0.5),
        'conv_w': nrm((N_ODD, CONV_K, CONV_WIDTH), CONV_K ** -0.5),
        'w_out_odd': nrm((N_ODD, CONV_WIDTH, d), CONV_WIDTH ** -0.5),
        'ffn_up': nrm((DEPTH, d, 2 * D_FF), d ** -0.5),
        'ffn_conv': nrm((DEPTH, FFN_CONV_K, D_FF), FFN_CONV_K ** -0.5),
        'ffn_down': nrm((DEPTH, D_FF, d), D_FF ** -0.5),
    }


def reference(x_prompt, x_sample, cache_cmp_kv, cache_slc_kv, cache_win_kv, state_rwkv_wkv,
              state_rwkv_shift, state_conv, state_ffn_conv, page_table, norm_mix, norm_ffn,
              rel_bias, w_in_even, w_out_even, q_norm, k_norm, phi_pe, phi_w1, phi_w2,
              rwkv_mu, rwkv_w0, rwkv_w2, rwkv_a0, rwkv_a2, rwkv_g2, rwkv_kk, rwkv_ka, rwkv_rk,
              rwkv_ln_w, rwkv_ln_b, w_in_odd, conv_w, w_out_odd, ffn_up, ffn_conv, ffn_down):
    P = dict(norm_mix=norm_mix, norm_ffn=norm_ffn, rel_bias=rel_bias, w_in_even=w_in_even,
             w_out_even=w_out_even, q_norm=q_norm, k_norm=k_norm, phi_pe=phi_pe, phi_w1=phi_w1,
             phi_w2=phi_w2, rwkv_mu=rwkv_mu, rwkv_w0=rwkv_w0, rwkv_w2=rwkv_w2, rwkv_a0=rwkv_a0,
             rwkv_a2=rwkv_a2, rwkv_g2=rwkv_g2, rwkv_kk=rwkv_kk, rwkv_ka=rwkv_ka, rwkv_rk=rwkv_rk,
             rwkv_ln_w=rwkv_ln_w, rwkv_ln_b=rwkv_ln_b, w_in_odd=w_in_odd, conv_w=conv_w,
             w_out_odd=w_out_odd, ffn_up=ffn_up, ffn_conv=ffn_conv, ffn_down=ffn_down)
    st = dict(cmp=cache_cmp_kv, slc=cache_slc_kv, win=cache_win_kv, wkv=state_rwkv_wkv,
              shift=state_rwkv_shift, conv=state_conv, ffn=state_ffn_conv, page_table=page_table)
    y_prompt, (cmp_p, slc_p, win_p, wkv_p, sh_p, conv_p, ffn_p) = trunk(x_prompt, P, None)
    y_sample, (cmp_s, slc_s, win_s, wkv_s, sh_s, conv_s, ffn_s) = trunk(x_sample, P, st)
    return (y_prompt, y_sample, cmp_p, cmp_s, slc_p, slc_s, win_p, win_s, wkv_p, wkv_s,
            sh_p, sh_s, conv_p, conv_s, ffn_p, ffn_s)
```

```python
import functools
import math

import jax
import jax.numpy as jnp
from jax import lax
from jax.experimental import pallas as pl
from jax.experimental.pallas import tpu as pltpu

F32 = jnp.float32
BF16 = jnp.bfloat16
HI = lax.Precision.HIGHEST

D_MODEL = 1024
HEAD_DIM = 64
NSA_HEADS = 8
NSA_KV = 2
NSA_QPG = 4
NSA_WIDTH = 512
KV_LANES = 2 * NSA_KV * HEAD_DIM
CMP_LEN = 32
CMP_STRIDE = 16
CMP_HIDDEN = 128
SEL_LEN = 64
N_SELECT = 16
WINDOW = 512
FORCE_SCORE = 1e9
RWKV_HEADS = 8
RWKV_WIDTH = 512
DECAY_LORA = 64
AAA_LORA = 64
GATE_LORA = 128
RWKV_PROJ = 3 * RWKV_WIDTH + DECAY_LORA + AAA_LORA + GATE_LORA
NSA_PROJ = NSA_WIDTH + 6 * NSA_KV * HEAD_DIM + 3 * NSA_HEADS
NSA_PROJ_PAD = 1408
GATE_COL_BLOCK = 10
D_FF = 2816
NUM_BUCKETS = 32
MAX_DISTANCE = 128
RMS_EPS = 1e-6
GN_EPS = 64e-5
PAGE = 128
LANE = 128
KT = 128
NEG = -1e30
VMEM_LIMIT = 56 * 1024 * 1024


def _cp(*sem):
    return pltpu.CompilerParams(dimension_semantics=sem, vmem_limit_bytes=VMEM_LIMIT)


def _round_up(a, b):
    return (a + b - 1) // b * b


def _nt(a, b, precision=None):
    return lax.dot_general(a, b, (((1,), (1,)), ((), ())), precision=precision,
                           preferred_element_type=F32)


def _tn(a, b, precision=None):
    return lax.dot_general(a, b, (((0,), (0,)), ((), ())), precision=precision,
                           preferred_element_type=F32)


def _norm_mm_kernel(x_ref, g_ref, w_ref, o_ref, xn_ref):
    @pl.when(pl.program_id(1) == 0)
    def _():
        x = x_ref[...]
        ms = jnp.mean(x * x, axis=-1, keepdims=True)
        xn_ref[...] = (x * lax.rsqrt(ms + RMS_EPS) * g_ref[...]).astype(BF16)

    o_ref[...] = jnp.dot(xn_ref[...], w_ref[...], preferred_element_type=F32)


def _norm_matmul(x, g, w, tm, tn):
    m, k = x.shape
    n = w.shape[1]
    return pl.pallas_call(
        _norm_mm_kernel, name="norm_mm",
        grid=(m // tm, n // tn),
        in_specs=[pl.BlockSpec((tm, k), lambda i, j: (i, 0)),
                  pl.BlockSpec((1, k), lambda i, j: (0, 0)),
                  pl.BlockSpec((k, tn), lambda i, j: (0, j))],
        out_specs=pl.BlockSpec((tm, tn), lambda i, j: (i, j)),
        out_shape=jax.ShapeDtypeStruct((m, n), F32),
        scratch_shapes=[pltpu.VMEM((tm, k), BF16)],
        compiler_params=_cp("arbitrary", "arbitrary"),
    )(x, g.reshape(1, k), w)


def _mm_res_kernel(a_ref, w_ref, r_ref, o_ref):
    o_ref[...] = r_ref[...] + jnp.dot(a_ref[...], w_ref[...], preferred_element_type=F32)


def _matmul_res(a, w, res, tm, tn):
    m, k = a.shape
    n = w.shape[1]
    return pl.pallas_call(
        _mm_res_kernel, name="mm_res",
        grid=(m // tm, n // tn),
        in_specs=[pl.BlockSpec((tm, k), lambda i, j: (i, 0)),
                  pl.BlockSpec((k, tn), lambda i, j: (0, j)),
                  pl.BlockSpec((tm, tn), lambda i, j: (i, j))],
        out_specs=pl.BlockSpec((tm, tn), lambda i, j: (i, j)),
        out_shape=jax.ShapeDtypeStruct((m, n), F32),
        compiler_params=_cp("arbitrary", "arbitrary"),
    )(a, w, res)


def _gated_kernel(x_ref, g_ref, *rest, mode, nbr, seq_tiles, nseq, lt):
    w_refs = rest[:nbr]
    cw_ref, prev_ref, o_ref, st_ref, xn_ref, carry_ref = rest[nbr:]
    i = pl.program_id(0)
    j = pl.program_id(1)

    @pl.when(j == 0)
    def _():
        x = x_ref[...]
        ms = jnp.mean(x * x, axis=-1, keepdims=True)
        xn_ref[...] = (x * lax.rsqrt(ms + RMS_EPS) * g_ref[...]).astype(BF16)

    xn = xn_ref[...]
    br = [jnp.dot(xn, w[...], preferred_element_type=F32) for w in w_refs]
    if mode == "ffn":
        cin, other = br
    else:
        other, cg, xi = br
        cin = cg * xi
    tf = cin.shape[-1]
    c = cin.reshape(nseq, lt, tf)
    if seq_tiles == 1:
        p = prev_ref[...]
    else:
        p = jnp.where(i % seq_tiles == 0, prev_ref[...], carry_ref[j])
    p0 = p[:, 0:1, :]
    p1 = p[:, 1:2, :]
    t = lax.broadcasted_iota(jnp.int32, c.shape, 1)
    c1 = jnp.where(t == 0, p1, pltpu.roll(c, 1, 1))
    c2 = jnp.where(t == 0, p0, jnp.where(t == 1, p1, pltpu.roll(c, 2, 1)))
    cw = cw_ref[...]
    y = c2 * cw[0:1, :] + c1 * cw[1:2, :] + c * cw[2:3, :]
    last2 = c[:, lt - 2:lt, :]
    for jj in range(st_ref.shape[-1] // tf):
        @pl.when(j == jj)
        def _(jj=jj):
            st_ref[:, :, jj * tf:(jj + 1) * tf] = last2
    if seq_tiles > 1:
        carry_ref[j] = last2
    o3 = other.reshape(c.shape)
    if mode == "ffn":
        out = (y * jax.nn.sigmoid(y)) * o3
    else:
        out = o3 * y
    o_ref[...] = out.reshape(nseq * lt, tf).astype(BF16)


def _gated(x, g, w, cw, prev, mode, seq_len, tm, tf):
    m, k = x.shape
    nbr = 2 if mode == "ffn" else 3
    f = w.shape[1] // nbr
    nj = f // tf
    if seq_len >= tm:
        seq_tiles, nseq, lt = seq_len // tm, 1, tm
    else:
        seq_tiles, nseq, lt = 1, tm // seq_len, seq_len
    b = prev.shape[0]
    w_specs = [pl.BlockSpec((k, tf), functools.partial(lambda i, j, o: (0, j + o), o=r * nj))
               for r in range(nbr)]
    kern = functools.partial(_gated_kernel, mode=mode, nbr=nbr, seq_tiles=seq_tiles, nseq=nseq, lt=lt)
    return pl.pallas_call(
        kern, name="gated_" + mode,
        grid=(m // tm, nj),
        in_specs=[pl.BlockSpec((tm, k), lambda i, j: (i, 0)),
                  pl.BlockSpec((1, k), lambda i, j: (0, 0))] + w_specs + [
                  pl.BlockSpec((3, tf), lambda i, j: (0, j)),
                  pl.BlockSpec((nseq, 2, tf), lambda i, j: (i // seq_tiles, 0, j))],
        out_specs=[pl.BlockSpec((tm, tf), lambda i, j: (i, j)),
                   pl.BlockSpec((nseq, 2, f), lambda i, j: (i // seq_tiles, 0, 0))],
        out_shape=[jax.ShapeDtypeStruct((m, f), BF16),
                   jax.ShapeDtypeStruct((b, 2, f), F32)],
        scratch_shapes=[pltpu.VMEM((tm, k), BF16),
                        pltpu.VMEM((nj, nseq, 2, tf), F32)],
        compiler_params=_cp("arbitrary", "arbitrary"),
    )(x, g.reshape(1, k), *([w] * nbr), cw, prev)


def _nsa_prep_kernel(z_ref, qg_ref, kg_ref, g512_ref, g128_ref, q_ref, kv_ref):
    z = z_ref[...]
    q = z[:, 0:NSA_WIDTH]
    ms = jnp.dot(q * q, g512_ref[...], precision=HI, preferred_element_type=F32)
    q_ref[...] = (q * lax.rsqrt(ms + RMS_EPS) * qg_ref[...] * (HEAD_DIM ** -0.5)).astype(BF16)
    kv_ref[:, 0:256] = z[:, 512:768]
    for r, off in ((0, 768), (1, 1024)):
        k = z[:, off:off + 128]
        ms = jnp.dot(k * k, g128_ref[...], precision=HI, preferred_element_type=F32)
        kv_ref[:, off - 512:off - 384] = k * lax.rsqrt(ms + RMS_EPS) * kg_ref[r:r + 1, :]
        kv_ref[:, off - 384:off - 256] = z[:, off + 128:off + 256]


def _nsa_prep(z, qg, kg, g512, g128, tm):
    m = z.shape[0]
    const = lambda i: (0, 0)
    return pl.pallas_call(
        _nsa_prep_kernel, name="nsa_prep",
        grid=(m // tm,),
        in_specs=[pl.BlockSpec((tm, NSA_PROJ_PAD), lambda i: (i, 0)),
                  pl.BlockSpec((1, 512), const), pl.BlockSpec((2, 128), const),
                  pl.BlockSpec((512, 512), const), pl.BlockSpec((128, 128), const)],
        out_specs=[pl.BlockSpec((tm, 512), lambda i: (i, 0)),
                   pl.BlockSpec((tm, 768), lambda i: (i, 0))],
        out_shape=[jax.ShapeDtypeStruct((m, 512), BF16),
                   jax.ShapeDtypeStruct((m, 768), F32)],
        compiler_params=_cp("arbitrary"),
    )(z, qg, kg, g512, g128)


def _compress_rows(rows_refs, m, pe_ref, w1_ref, w2_ref, kn_ref, g128_ref):
    outs = []
    for kv, rows_ref in enumerate(rows_refs):
        acc0 = jnp.zeros((m, NSA_KV * CMP_HIDDEN), F32)
        acc1 = jnp.zeros((m, NSA_KV * CMP_HIDDEN), F32)
        for j in range(CMP_STRIDE):
            xj = rows_ref[pl.ds(j, m, stride=CMP_STRIDE), :]
            acc0 = acc0 + jnp.dot((xj + pe_ref[kv, j:j + 1, :]).astype(BF16), w1_ref[kv, j],
                                  preferred_element_type=F32)
            acc1 = acc1 + jnp.dot((xj + pe_ref[kv, CMP_STRIDE + j:CMP_STRIDE + j + 1, :]).astype(BF16),
                                  w1_ref[kv, CMP_STRIDE + j], preferred_element_type=F32)
        hid = acc0 + pltpu.roll(acc1, m - 1, 0)
        outs.append(jnp.dot(jax.nn.gelu(hid).astype(BF16), w2_ref[kv], preferred_element_type=F32))
    kc, vc = outs
    ms = jnp.dot(kc * kc, g128_ref[...], precision=HI, preferred_element_type=F32)
    kc = kc * lax.rsqrt(ms + RMS_EPS) * kn_ref[...]
    return jnp.concatenate([kc, vc], axis=1)


def _compress_kernel(k_ref, v_ref, pe_ref, w1_ref, w2_ref, kn_ref, g128_ref, o_ref, *, m):
    o_ref[0] = _compress_rows((k_ref.at[0], v_ref.at[0]), m, pe_ref, w1_ref, w2_ref, kn_ref, g128_ref)


def _compress(kv, pe, w1, w2, kn, g128):
    b, t, _ = kv.shape
    m = t // CMP_STRIDE
    c2 = lambda i: (0, 0)
    hw = NSA_KV * HEAD_DIM
    return pl.pallas_call(
        functools.partial(_compress_kernel, m=m), name="compress",
        grid=(b,),
        in_specs=[pl.BlockSpec((1, t, hw), lambda i: (i, 0, 0)),
                  pl.BlockSpec((1, t, hw), lambda i: (i, 0, 1)),
                  pl.BlockSpec((2, CMP_LEN, hw), lambda i: (0, 0, 0)),
                  pl.BlockSpec((2, CMP_LEN, hw, NSA_KV * CMP_HIDDEN), lambda i: (0, 0, 0, 0)),
                  pl.BlockSpec((2, NSA_KV * CMP_HIDDEN, hw), lambda i: (0, 0, 0)),
                  pl.BlockSpec((1, 128), c2), pl.BlockSpec((128, 128), c2)],
        out_specs=pl.BlockSpec((1, m, KV_LANES), lambda i: (i, 0, 0)),
        out_shape=jax.ShapeDtypeStruct((b, m, KV_LANES), F32),
        compiler_params=_cp("arbitrary"),
    )(kv, kv, pe, w1, w2, kn, g128)


def _msoftmax(s, mask):
    s = jnp.where(mask, s, NEG)
    mx = jnp.max(s, axis=-1, keepdims=True)
    e = jnp.where(mask, jnp.exp(s - mx), 0.0)
    den = jnp.sum(e, axis=-1, keepdims=True)
    return e / jnp.where(den > 0, den, 1.0)


def _select_blocks(imp, qpos, ns):
    s_io = lax.broadcasted_iota(jnp.int32, imp.shape, 1)
    cur = qpos // SEL_LEN
    forced = (s_io == 0) | (s_io == cur) | (s_io == cur - 1)
    future = s_io * SEL_LEN > qpos
    imp = jnp.where(forced, FORCE_SCORE, imp)
    imp = jnp.where(future, -FORCE_SCORE, imp)
    imp = jnp.where(s_io >= ns, -3e38, imp)
    rank = jnp.zeros(imp.shape, jnp.int32)
    for sp in range(ns):
        col = imp[:, sp:sp + 1]
        beats = (col > imp) | ((col == imp) & (s_io > sp))
        rank = rank + beats.astype(jnp.int32)
    sel = (rank < min(N_SELECT, ns)) & (s_io < ns)
    return sel.astype(F32)


def _heads_to_rows(q, g):
    return jnp.concatenate(
        [q[:, HEAD_DIM * (NSA_QPG * g + h):HEAD_DIM * (NSA_QPG * g + h + 1)] for h in range(NSA_QPG)],
        axis=0)


def _flash_step(qg, k, v, bias, mask, carry, rows):
    m_, l_, acc = carry
    s = _nt(qg, k).reshape(NSA_QPG, rows, k.shape[0])
    if bias is not None:
        s = s + bias
    s = jnp.where(mask, s, NEG)
    m_new = jnp.maximum(m_, jnp.max(s, axis=-1, keepdims=True))
    alpha = jnp.exp(m_ - m_new)
    e = jnp.where(mask, jnp.exp(s - m_new), 0.0)
    l_new = alpha * l_ + jnp.sum(e, axis=-1, keepdims=True)
    pv = jnp.dot(e.reshape(NSA_QPG * rows, k.shape[0]).astype(BF16), v, preferred_element_type=F32)
    acc_new = alpha * acc + pv.reshape(NSA_QPG, rows, HEAD_DIM)
    return m_new, l_new, acc_new


def _flash_init(rows):
    return (jnp.full((NSA_QPG, rows, 1), NEG, F32), jnp.zeros((NSA_QPG, rows, 1), F32),
            jnp.zeros((NSA_QPG, rows, HEAD_DIM), F32))


def _flash_out(carry):
    _, l_, acc = carry
    return acc / jnp.where(l_ > 0, l_, 1.0)


def _nsa_prompt_kernel(q_ref, gt_ref, cc_ref, kv_ref, bc_ref, tb_ref, ov_ref, ex_ref, o_ref, selk_s,
                       *, t_len, m, ns):
    i = pl.program_id(1)
    nkt = t_len // KT
    q = q_ref[0]
    gs = jax.nn.sigmoid(gt_ref[...])
    r_io = lax.broadcasted_iota(jnp.int32, (KT, KT), 0)
    c_io = lax.broadcasted_iota(jnp.int32, (KT, KT), 1)
    qpos = i * KT + lax.broadcasted_iota(jnp.int32, (KT, 1), 0)
    n_io = lax.broadcasted_iota(jnp.int32, (KT, m), 1)
    mask_c = ((n_io * CMP_STRIDE + (CMP_LEN - 1)) <= qpos) & (n_io < m - 1)
    cc = cc_ref[0]
    outs = []
    for g in range(NSA_KV):
        qg = _heads_to_rows(q, g)
        kc = cc[:, HEAD_DIM * g:HEAD_DIM * (g + 1)].astype(BF16)
        vc = cc[:, 128 + HEAD_DIM * g:128 + HEAD_DIM * (g + 1)].astype(BF16)
        s = _nt(qg, kc).reshape(NSA_QPG, KT, m) + bc_ref[NSA_QPG * g:NSA_QPG * (g + 1)]
        p_c = _msoftmax(s, mask_c[None])
        o_c = jnp.dot(p_c.reshape(NSA_QPG * KT, m).astype(BF16), vc,
                      preferred_element_type=F32).reshape(NSA_QPG, KT, HEAD_DIM)
        psum = p_c[0] + p_c[1] + p_c[2] + p_c[3]
        imp = jnp.dot(psum, ov_ref[...], precision=HI, preferred_element_type=F32)
        sel = _select_blocks(imp, qpos, ns)
        selk = jnp.dot(sel.astype(BF16), ex_ref[...], preferred_element_type=F32)
        for kt in range(nkt):
            selk_s[kt] = selk[:, kt * KT:(kt + 1) * KT]

        k_lo = 256 + HEAD_DIM * g
        v_lo = 384 + HEAD_DIM * g
        w_off = 256

        def load_kv(kt, off):
            r0 = pl.multiple_of(kt * KT, KT)
            k = kv_ref[0, pl.ds(r0, KT), k_lo + off:k_lo + off + HEAD_DIM].astype(BF16)
            v = kv_ref[0, pl.ds(r0, KT), v_lo + off:v_lo + off + HEAD_DIM].astype(BF16)
            return k, v

        tb_prev = tb_ref[NSA_QPG * g:NSA_QPG * (g + 1), 0]
        tb_diag = tb_ref[NSA_QPG * g:NSA_QPG * (g + 1), 1]

        def body(kt, carry):
            k, v = load_kv(kt, 0)
            return _flash_step(qg, k, v, None, (selk_s[kt] > 0.5)[None], carry, KT)

        carry = lax.fori_loop(0, jnp.maximum(i - 1, 0), body, _flash_init(KT))
        ktp = jnp.maximum(i - 1, 0)
        k, v = load_kv(ktp, 0)
        carry = _flash_step(qg, k, v, tb_prev, ((selk_s[ktp] > 0.5) & (i >= 1))[None], carry, KT)
        k, v = load_kv(i, 0)
        carry = _flash_step(qg, k, v, tb_diag, ((selk_s[i] > 0.5) & (c_io <= r_io))[None], carry, KT)
        o_s = _flash_out(carry)

        carry = _flash_init(KT)
        nwt = WINDOW // KT
        for d in range(nwt, -1, -1):
            kt = i - d
            k, v = load_kv(jnp.maximum(kt, 0), w_off)
            if d == nwt:
                mk = c_io > r_io
            elif d == 0:
                mk = c_io <= r_io
            else:
                mk = jnp.full((KT, KT), True)
            mk = mk & (kt >= 0)
            bias = tb_prev if d == 1 else (tb_diag if d == 0 else None)
            carry = _flash_step(qg, k, v, bias, mk[None], carry, KT)
        o_w = _flash_out(carry)

        for h in range(NSA_QPG):
            hh = NSA_QPG * g + h
            o = (gs[:, 3 * hh:3 * hh + 1] * o_c[h] + gs[:, 3 * hh + 1:3 * hh + 2] * o_s[h]
                 + gs[:, 3 * hh + 2:3 * hh + 3] * o_w[h])
            outs.append(o)
    o_ref[0] = jnp.concatenate(outs, axis=1).astype(BF16)


def _nsa_prompt(qn, z_nsa, cc, kv, bc, tb, ov, ex, b, t_len):
    m = cc.shape[1]
    ns = -(-t_len // SEL_LEN)
    nq = t_len // KT
    c2 = lambda bi, i: (0, 0)
    return pl.pallas_call(
        functools.partial(_nsa_prompt_kernel, t_len=t_len, m=m, ns=ns), name="nsa_prompt",
        grid=(b, nq),
        in_specs=[pl.BlockSpec((1, KT, 512), lambda bi, i: (bi, i, 0)),
                  pl.BlockSpec((KT, LANE), lambda bi, i: (bi * nq + i, GATE_COL_BLOCK)),
                  pl.BlockSpec((1, m, KV_LANES), lambda bi, i: (bi, 0, 0)),
                  pl.BlockSpec((1, t_len, 768), lambda bi, i: (bi, 0, 0)),
                  pl.BlockSpec((NSA_HEADS, KT, m), lambda bi, i: (0, i, 0)),
                  pl.BlockSpec((NSA_HEADS, 2, KT, KT), lambda bi, i: (0, 0, 0, 0)),
                  pl.BlockSpec((m, LANE), c2),
                  pl.BlockSpec((LANE, t_len), c2)],
        out_specs=pl.BlockSpec((1, KT, 512), lambda bi, i: (bi, i, 0)),
        out_shape=jax.ShapeDtypeStruct((b, t_len, 512), BF16),
        scratch_shapes=[pltpu.VMEM((nq, KT, KT), F32)],
        compiler_params=_cp("arbitrary", "arbitrary"),
    )(qn.reshape(b, t_len, 512), z_nsa, cc, kv.reshape(b, t_len, 768), bc, tb, ov, ex)


def _nsa_sample_kernel(pt_ref, cmp_pg, slc_pg, new_ref, win_ref, q_ref, gt_ref, pe_ref, w1_ref, w2_ref,
                       kn_ref, g128_ref, bc_ref, bs_ref, bw_ref, ov_ref, ex_ref, o_ref,
                       cmpk_s, cmpv_s, slc_s, win_s, *, npages, past, wb, tq, m, ns):
    p = pl.program_id(1)
    r0 = pl.multiple_of(p * PAGE, PAGE)
    cmpk_s[pl.ds(r0, PAGE), :] = cmp_pg[0, 0, :, 0:128]
    cmpv_s[pl.ds(r0, PAGE), :] = cmp_pg[0, 0, :, 128:256]
    slc_s[pl.ds(r0, PAGE), :] = slc_pg[0, 0]

    @pl.when(p == npages - 1)
    def _():
        kp = slc_s.shape[0]
        wp = win_s.shape[0]
        new = new_ref[0]
        pad = jnp.zeros((kp - past - tq, KV_LANES), F32)
        cmpk_s[past:past + tq, :] = new[:, 0:128]
        cmpk_s[past + tq:kp, :] = pad[:, 0:128]
        cmpv_s[past:past + tq, :] = new[:, 128:256]
        cmpv_s[past + tq:kp, :] = pad[:, 0:128]
        slc_s[past:past + tq, :] = new[:, 256:512]
        slc_s[past + tq:kp, :] = pad
        win_s[0:wb, :] = win_ref[0, 0]
        win_s[wb:wb + tq, :] = new[:, 512:768]
        win_s[wb + tq:wp, :] = jnp.zeros((wp - wb - tq, KV_LANES), F32)

        cc = _compress_rows((cmpk_s, cmpv_s), m, pe_ref, w1_ref, w2_ref, kn_ref, g128_ref)
        q = q_ref[0]
        gs = jax.nn.sigmoid(gt_ref[0])
        qpos = past + lax.broadcasted_iota(jnp.int32, (tq, 1), 0)
        n_io = lax.broadcasted_iota(jnp.int32, (tq, m), 1)
        mask_c = ((n_io * CMP_STRIDE + (CMP_LEN - 1)) <= qpos) & (n_io < m - 1)
        k_io = lax.broadcasted_iota(jnp.int32, (tq, kp), 1)
        mask_s0 = (k_io <= qpos) & (k_io < past + tq)
        j_io = lax.broadcasted_iota(jnp.int32, (tq, wp), 1)
        dist_w = qpos - (past - wb + j_io)
        mask_w = (dist_w >= 0) & (dist_w < WINDOW) & (j_io < wb + tq)
        outs = []
        for g in range(NSA_KV):
            qg = _heads_to_rows(q, g)
            kc = cc[:, HEAD_DIM * g:HEAD_DIM * (g + 1)].astype(BF16)
            vc = cc[:, 128 + HEAD_DIM * g:128 + HEAD_DIM * (g + 1)].astype(BF16)
            s = _nt(qg, kc).reshape(NSA_QPG, tq, m) + bc_ref[NSA_QPG * g:NSA_QPG * (g + 1)]
            p_c = _msoftmax(s, mask_c[None])
            o_c = jnp.dot(p_c.reshape(NSA_QPG * tq, m).astype(BF16), vc,
                          preferred_element_type=F32).reshape(NSA_QPG, tq, HEAD_DIM)
            psum = p_c[0] + p_c[1] + p_c[2] + p_c[3]
            imp = jnp.dot(psum, ov_ref[...], precision=HI, preferred_element_type=F32)
            sel = _select_blocks(imp, qpos, ns)
            selk = jnp.dot(sel.astype(BF16), ex_ref[...], preferred_element_type=F32)

            ks = slc_s[:, HEAD_DIM * g:HEAD_DIM * (g + 1)].astype(BF16)
            vs = slc_s[:, 128 + HEAD_DIM * g:128 + HEAD_DIM * (g + 1)].astype(BF16)
            s = _nt(qg, ks).reshape(NSA_QPG, tq, kp) + bs_ref[NSA_QPG * g:NSA_QPG * (g + 1)]
            p_s = _msoftmax(s, ((selk > 0.5) & mask_s0)[None])
            o_s = jnp.dot(p_s.reshape(NSA_QPG * tq, kp).astype(BF16), vs,
                          preferred_element_type=F32).reshape(NSA_QPG, tq, HEAD_DIM)

            kw = win_s[:, HEAD_DIM * g:HEAD_DIM * (g + 1)].astype(BF16)
            vw = win_s[:, 128 + HEAD_DIM * g:128 + HEAD_DIM * (g + 1)].astype(BF16)
            s = _nt(qg, kw).reshape(NSA_QPG, tq, wp) + bw_ref[NSA_QPG * g:NSA_QPG * (g + 1)]
            p_w = _msoftmax(s, mask_w[None])
            o_w = jnp.dot(p_w.reshape(NSA_QPG * tq, wp).astype(BF16), vw,
                          preferred_element_type=F32).reshape(NSA_QPG, tq, HEAD_DIM)
            for h in range(NSA_QPG):
                hh = NSA_QPG * g + h
                o = (gs[:, 3 * hh:3 * hh + 1] * o_c[h] + gs[:, 3 * hh + 1:3 * hh + 2] * o_s[h]
                     + gs[:, 3 * hh + 2:3 * hh + 3] * o_w[h])
                outs.append(o)
        o_ref[0] = jnp.concatenate(outs, axis=1).astype(BF16)


def _nsa_sample(pt, pool_cmp, pool_slc, li, kv_new, win, qn, z_nsa, cw, bc, bs, bw, ov, ex, b, tq):
    npages = pt.shape[1]
    past = npages * PAGE
    wb = win.shape[2]
    kp = _round_up(past + tq, LANE)
    wp = _round_up(wb + tq, LANE)
    m = (past + tq) // CMP_STRIDE
    ns = -(-(past + tq) // SEL_LEN)
    pe, w1, w2, kn, g128 = cw
    c2 = lambda bi, p, pt_: (0, 0)
    c3 = lambda bi, p, pt_: (0, 0, 0)
    per_b3 = lambda bi, p, pt_: (bi, 0, 0)
    grid_spec = pltpu.PrefetchScalarGridSpec(
        num_scalar_prefetch=1,
        grid=(b, npages),
        in_specs=[pl.BlockSpec((1, 1, PAGE, KV_LANES), lambda bi, p, pt_: (pt_[bi, p], li, 0, 0)),
                  pl.BlockSpec((1, 1, PAGE, KV_LANES), lambda bi, p, pt_: (pt_[bi, p], li, 0, 0)),
                  pl.BlockSpec((1, tq, 768), per_b3),
                  pl.BlockSpec((1, 1, wb, KV_LANES), lambda bi, p, pt_: (bi, li, 0, 0)),
                  pl.BlockSpec((1, tq, 512), per_b3),
                  pl.BlockSpec((1, tq, LANE), lambda bi, p, pt_: (bi, 0, GATE_COL_BLOCK)),
                  pl.BlockSpec((2, CMP_LEN, LANE), c3),
                  pl.BlockSpec((2, CMP_LEN, LANE, NSA_KV * CMP_HIDDEN), lambda bi, p, pt_: (0, 0, 0, 0)),
                  pl.BlockSpec((2, NSA_KV * CMP_HIDDEN, LANE), c3),
                  pl.BlockSpec((1, 128), c2), pl.BlockSpec((128, 128), c2),
                  pl.BlockSpec((NSA_HEADS, tq, m), c3),
                  pl.BlockSpec((NSA_HEADS, tq, kp), c3),
                  pl.BlockSpec((NSA_HEADS, tq, wp), c3),
                  pl.BlockSpec((m, LANE), c2),
                  pl.BlockSpec((LANE, kp), c2)],
        out_specs=pl.BlockSpec((1, tq, 512), per_b3),
        scratch_shapes=[pltpu.VMEM((kp, LANE), F32), pltpu.VMEM((kp, LANE), F32),
                        pltpu.VMEM((kp, KV_LANES), F32), pltpu.VMEM((wp, KV_LANES), F32)])
    kern = functools.partial(_nsa_sample_kernel, npages=npages, past=past, wb=wb, tq=tq, m=m, ns=ns)
    return pl.pallas_call(
        kern, grid_spec=grid_spec, name="nsa_sample",
        out_shape=jax.ShapeDtypeStruct((b, tq, 512), BF16),
        compiler_params=_cp("arbitrary", "arbitrary"),
    )(pt, pool_cmp, pool_slc, kv_new.reshape(b, tq, 768), win, qn.reshape(b, tq, 512),
      z_nsa.reshape(b, tq, NSA_PROJ_PAD), pe, w1, w2, kn, g128, bc, bs, bw, ov, ex)


def _rwkv_kernel(z_ref, zp_ref, s0_ref, mu_ref, w0_ref, w2_ref, a0_ref, a2_ref, g2_ref, kkp_ref, ka_ref,
                 rk_ref, lnw_ref, lnb_ref, y_ref, st_ref, s_scr, carry_scr, *, c_len):
    c = pl.program_id(1)
    cl = c_len

    @pl.when(c == 0)
    def _():
        s_scr[...] = s0_ref[0]
        carry_scr[...] = zp_ref[0]

    z = z_ref[0]
    row = lax.broadcasted_iota(jnp.int32, z.shape, 0)
    shifted = jnp.where(row == 0, carry_scr[...], pltpu.roll(z, 1, 0))
    carry_scr[...] = z[cl - 1:cl, :]
    zz = z + (shifted - z) * mu_ref[...]
    w = RWKV_WIDTH
    r = zz[:, 0:w]
    k = zz[:, w:2 * w]
    v = zz[:, 2 * w:3 * w]
    zw = zz[:, 3 * w:3 * w + DECAY_LORA]
    za = zz[:, 3 * w + DECAY_LORA:3 * w + DECAY_LORA + AAA_LORA]
    zg = zz[:, 3 * w + DECAY_LORA + AAA_LORA:]
    xw = -(w0_ref[...] + jnp.dot(jnp.tanh(zw), w2_ref[...], precision=HI, preferred_element_type=F32))
    softplus = jnp.maximum(xw, 0.0) + jnp.log(1.0 + jnp.exp(-jnp.abs(xw)))
    logdec = -jnp.exp(-softplus - 0.5)
    a = jax.nn.sigmoid(a0_ref[...] + jnp.dot(za, a2_ref[...], precision=HI, preferred_element_type=F32))
    gate = jnp.dot(jax.nn.sigmoid(zg), g2_ref[...], precision=HI, preferred_element_type=F32)
    k_mod = k * (1.0 + (a - 1.0) * ka_ref[...])
    kku = k * kkp_ref[...]

    ti = lax.broadcasted_iota(jnp.int32, (cl, cl), 0)
    si = lax.broadcasted_iota(jnp.int32, (cl, cl), 1)
    lower = si <= ti
    strict = si < ti
    gcum = jnp.dot(lower.astype(F32), logdec, precision=HI, preferred_element_type=F32)
    glast = gcum[cl - 1:cl, :]
    e_g = jnp.exp(gcum)
    e_gm = jnp.exp(gcum - logdec)
    e_ng = jnp.exp(-gcum)
    e_lg = jnp.exp(glast - gcum)
    e_l = jnp.exp(glast)
    nsteps = max(1, int(math.ceil(math.log2(cl))))

    outs = []
    for h in range(RWKV_HEADS):
        sl = slice(HEAD_DIM * h, HEAD_DIM * (h + 1))
        kk = kku[:, sl]
        kk = kk * lax.rsqrt(jnp.maximum(jnp.sum(kk * kk, axis=-1, keepdims=True), 1e-24))
        kka = kk * a[:, sl]
        km = k_mod[:, sl]
        vh = v[:, sl]
        rh = r[:, sl]
        at = -kk * e_gm[:, sl]
        rt = rh * e_g[:, sl]
        bt = kka * e_ng[:, sl]
        kt = km * e_ng[:, sl]
        s_h = s_scr[h]
        a_m = jnp.where(strict, _nt(at, bt, HI), 0.0)
        b_m = jnp.where(strict, _nt(at, kt, HI), 0.0)
        rb = jnp.where(lower, _nt(rt, bt, HI), 0.0)
        rkm = jnp.where(lower, _nt(rt, kt, HI), 0.0)
        zmat = _nt(at, s_h, HI) + jnp.dot(b_m, vh, precision=HI, preferred_element_type=F32)
        pw = a_m
        for it in range(nsteps):
            zmat = zmat + jnp.dot(pw, zmat, precision=HI, preferred_element_type=F32)
            if it + 1 < nsteps:
                pw = jnp.dot(pw, pw, precision=HI, preferred_element_type=F32)
        y = (_nt(rt, s_h, HI) + jnp.dot(rb, zmat, precision=HI, preferred_element_type=F32)
             + jnp.dot(rkm, vh, precision=HI, preferred_element_type=F32))
        bh = kka * e_lg[:, sl]
        kh = km * e_lg[:, sl]
        s_scr[h] = s_h * e_l[:, sl] + _tn(zmat, bh, HI) + _tn(vh, kh, HI)

        mean = jnp.mean(y, axis=-1, keepdims=True)
        yc = y - mean
        var = jnp.mean(yc * yc, axis=-1, keepdims=True)
        yn = yc * lax.rsqrt(var + GN_EPS) * lnw_ref[:, sl] + lnb_ref[:, sl]
        bonus = jnp.sum(rh * km * rk_ref[:, sl], axis=-1, keepdims=True) * vh
        outs.append(yn + bonus)
    y_ref[0] = (jnp.concatenate(outs, axis=1) * gate).astype(BF16)
    st_ref[0] = s_scr[...]


def _rwkv(z_rw, z_prev, s0, pr, b, t_len, c_len):
    nchunk = t_len // c_len
    c2 = lambda bi, c: (0, 0)
    vec = lambda n: pl.BlockSpec((1, n), c2)
    return pl.pallas_call(
        functools.partial(_rwkv_kernel, c_len=c_len), name="rwkv",
        grid=(b, nchunk),
        in_specs=[pl.BlockSpec((1, c_len, RWKV_PROJ), lambda bi, c: (bi, c, 0)),
                  pl.BlockSpec((1, 1, RWKV_PROJ), lambda bi, c: (bi, 0, 0)),
                  pl.BlockSpec((1, RWKV_HEADS, HEAD_DIM, HEAD_DIM), lambda bi, c: (bi, 0, 0, 0)),
                  vec(RWKV_PROJ), vec(512), pl.BlockSpec((DECAY_LORA, 512), c2),
                  vec(512), pl.BlockSpec((AAA_LORA, 512), c2), pl.BlockSpec((GATE_LORA, 512), c2),
                  vec(512), vec(512), vec(512), vec(512), vec(512)],
        out_specs=[pl.BlockSpec((1, c_len, 512), lambda bi, c: (bi, c, 0)),
                   pl.BlockSpec((1, RWKV_HEADS, HEAD_DIM, HEAD_DIM), lambda bi, c: (bi, 0, 0, 0))],
        out_shape=[jax.ShapeDtypeStruct((b, t_len, 512), BF16),
                   jax.ShapeDtypeStruct((b, RWKV_HEADS, HEAD_DIM, HEAD_DIM), F32)],
        scratch_shapes=[pltpu.VMEM((RWKV_HEADS, HEAD_DIM, HEAD_DIM), F32),
                        pltpu.VMEM((1, RWKV_PROJ), F32)],
        compiler_params=_cp("arbitrary", "arbitrary"),
    )(z_rw.reshape(b, t_len, RWKV_PROJ), z_prev.reshape(b, 1, RWKV_PROJ), s0, *pr)


def _rel_bucket(dist):
    n = jnp.maximum(dist, 0)
    max_exact = NUM_BUCKETS // 2
    nf = jnp.maximum(n, 1).astype(F32)
    large = max_exact + (jnp.log(nf / max_exact) / math.log(MAX_DISTANCE / max_exact)
                         * (NUM_BUCKETS - max_exact)).astype(jnp.int32)
    large = jnp.minimum(large, NUM_BUCKETS - 1)
    return jnp.where(n < max_exact, n, large)


def _bias_table(rel_bias, dist, delta):
    tb = rel_bias.astype(F32)
    out = jnp.moveaxis(tb[_rel_bucket(dist)], -1, 0)
    if delta:
        out = out - tb[NUM_BUCKETS - 1].reshape((NSA_HEADS,) + (1,) * dist.ndim)
    return out


def _block_diag_ones(n, grp):
    idx = jnp.arange(n) // grp
    return (idx[:, None] == idx[None, :]).astype(F32) / grp


def _compress_weights(phi_pe, phi_w1, phi_w2, kn_cmp):
    pe = jnp.concatenate([phi_pe, phi_pe], axis=-1)
    w1 = jnp.zeros((2, CMP_LEN, NSA_KV * HEAD_DIM, NSA_KV * CMP_HIDDEN), F32)
    w2 = jnp.zeros((2, NSA_KV * CMP_HIDDEN, NSA_KV * HEAD_DIM), F32)
    for g in range(NSA_KV):
        w1 = w1.at[:, :, HEAD_DIM * g:HEAD_DIM * (g + 1), CMP_HIDDEN * g:CMP_HIDDEN * (g + 1)].set(phi_w1)
        w2 = w2.at[:, CMP_HIDDEN * g:CMP_HIDDEN * (g + 1), HEAD_DIM * g:HEAD_DIM * (g + 1)].set(phi_w2)
    kn = jnp.tile(kn_cmp, NSA_KV).reshape(1, 128)
    return pe, w1.astype(BF16), w2.astype(BF16), kn


def _overlap_expand(t_all):
    nc = t_all // CMP_STRIDE - 1
    m = nc + 1
    ns = -(-t_all // SEL_LEN)
    c0 = jnp.arange(m)[:, None] * CMP_STRIDE
    s0 = jnp.arange(LANE)[None, :] * SEL_LEN
    ov = ((c0 < s0 + SEL_LEN) & (c0 + CMP_LEN > s0) & (jnp.arange(m)[:, None] < nc)
          & (jnp.arange(LANE)[None, :] < ns)).astype(F32)
    kp = _round_up(t_all, LANE)
    ex = (jnp.arange(kp)[None, :] // SEL_LEN == jnp.arange(LANE)[:, None]).astype(BF16)
    return ov, ex


def _pick_tm(t_len):
    return 512 if t_len % 512 == 0 else 128


def _trunk(x3, prm, st):
    b, t_len, d = x3.shape
    m_rows = b * t_len
    x = x3.reshape(m_rows, d)
    prompt = st is None
    if prompt:
        tm = _pick_tm(t_len)
        c_len = 64
    else:
        tm = 256 if m_rows % 256 == 0 else m_rows
        c_len = t_len
    g512 = _block_diag_ones(512, HEAD_DIM)
    g128 = _block_diag_ones(128, HEAD_DIM)
    rel_bias = prm["rel_bias"]

    if prompt:
        ov, ex = _overlap_expand(t_len)
        mcmp = t_len // CMP_STRIDE
        qp = jnp.arange(t_len)[:, None]
        cend = jnp.arange(mcmp)[None, :] * CMP_STRIDE + (CMP_LEN - 1)
        bc = _bias_table(rel_bias, qp - cend, False)
        rr = jnp.arange(KT)[:, None]
        cc_ = jnp.arange(KT)[None, :]
        tb = jnp.stack([_bias_table(rel_bias, KT + rr - cc_, True),
                        _bias_table(rel_bias, rr - cc_, True)], axis=1)
    else:
        pt = st["page_table"]
        past = pt.shape[1] * PAGE
        wb = st["win"].shape[2]
        t_all = past + t_len
        ov, ex = _overlap_expand(t_all)
        mcmp = t_all // CMP_STRIDE
        kp = _round_up(t_all, LANE)
        wp = _round_up(wb + t_len, LANE)
        qp = past + jnp.arange(t_len)[:, None]
        cend = jnp.arange(mcmp)[None, :] * CMP_STRIDE + (CMP_LEN - 1)
        bc = _bias_table(rel_bias, qp - cend, False)
        bs = _bias_table(rel_bias, qp - jnp.arange(kp)[None, :], True)
        bw = _bias_table(rel_bias, qp - (past - wb + jnp.arange(wp)[None, :]), True)
        pool_cmp = st["cmp"].reshape(st["cmp"].shape[0], st["cmp"].shape[1], PAGE, KV_LANES)
        pool_slc = st["slc"].reshape(st["slc"].shape[0], st["slc"].shape[1], PAGE, KV_LANES)
        win_all = st["win"].reshape(b, st["win"].shape[1], wb, KV_LANES)

    cmp_r, slc_r, win_r, wkv_r, sh_r, conv_r, ffn_r = [], [], [], [], [], [], []
    for l in range(4):
        li = l // 2
        if l % 2 == 0:
            w_in = prm["w_in_even"][li]
            w_nsa = jnp.pad(w_in[:, :NSA_PROJ], ((0, 0), (0, NSA_PROJ_PAD - NSA_PROJ))).astype(BF16)
            w_rw = w_in[:, NSA_PROJ:].astype(BF16)
            z_nsa = _norm_matmul(x, prm["norm_mix"][l], w_nsa, tm, NSA_PROJ_PAD)
            z_rw = _norm_matmul(x, prm["norm_mix"][l], w_rw, tm, RWKV_PROJ // 2)
            qg = jnp.tile(prm["q_norm"][li], NSA_HEADS).reshape(1, 512)
            kg = jnp.stack([jnp.tile(prm["k_norm"][li, 1], NSA_KV), jnp.tile(prm["k_norm"][li, 2], NSA_KV)])
            qn, kv3 = _nsa_prep(z_nsa, qg, kg, g512, g128, tm)
            cw = _compress_weights(prm["phi_pe"][li], prm["phi_w1"][li], prm["phi_w2"][li],
                                   prm["k_norm"][li, 0])
            kv3b = kv3.reshape(b, t_len, 768)
            if prompt:
                cc = _compress(kv3b, *cw, g128)
                nsa = _nsa_prompt(qn, z_nsa, cc, kv3, bc, tb, ov, ex, b, t_len)
                z_prev = jnp.zeros((b, RWKV_PROJ), F32)
                s0 = jnp.zeros((b, RWKV_HEADS, HEAD_DIM, HEAD_DIM), F32)
                n_keep = min(WINDOW, t_len)
                win_rows = kv3b[:, t_len - n_keep:, 512:768]
            else:
                nsa = _nsa_sample(pt, pool_cmp, pool_slc, li, kv3, win_all, qn, z_nsa, cw + (g128,),
                                  bc, bs, bw, ov, ex, b, t_len)
                z_prev = st["shift"][:, li]
                s0 = st["wkv"][:, li]
                n_keep = min(WINDOW, wb + t_len)
                win_rows = jnp.concatenate([win_all[:, li], kv3b[:, :, 512:768]], axis=1)[:, -n_keep:]
            pr = (prm["rwkv_mu"][li].reshape(1, -1), prm["rwkv_w0"][li].reshape(1, -1), prm["rwkv_w2"][li],
                  prm["rwkv_a0"][li].reshape(1, -1), prm["rwkv_a2"][li], prm["rwkv_g2"][li],
                  prm["rwkv_kk"][li].reshape(1, -1), prm["rwkv_ka"][li].reshape(1, -1),
                  prm["rwkv_rk"][li].reshape(1, -1), prm["rwkv_ln_w"][li].reshape(1, -1),
                  prm["rwkv_ln_b"][li].reshape(1, -1))
            rw, s_t = _rwkv(z_rw, z_prev, s0, pr, b, t_len, c_len)
            mix = jnp.concatenate([nsa.reshape(m_rows, 512), rw.reshape(m_rows, 512)], axis=1)
            x = _matmul_res(mix, prm["w_out_even"][li].astype(BF16), x, tm, 512)
            cmp_r.append(kv3b[:, :, 0:256].reshape(b, t_len, 2, NSA_KV, HEAD_DIM))
            slc_r.append(kv3b[:, :, 256:512].reshape(b, t_len, 2, NSA_KV, HEAD_DIM))
            win_r.append(win_rows.reshape(b, n_keep, 2, NSA_KV, HEAD_DIM))
            wkv_r.append(s_t)
            sh_r.append(z_rw.reshape(b, t_len, RWKV_PROJ)[:, -1])
        else:
            prev = jnp.zeros((b, 2, D_MODEL), F32) if prompt else st["conv"][:, li]
            gt, cs = _gated(x, prm["norm_mix"][l], prm["w_in_odd"][li].astype(BF16), prm["conv_w"][li],
                            prev, "odd", t_len, tm, 512)
            x = _matmul_res(gt, prm["w_out_odd"][li].astype(BF16), x, tm, 512)
            conv_r.append(cs)
        prev = jnp.zeros((b, 2, D_FF), F32) if prompt else st["ffn"][:, l]
        gt, fs = _gated(x, prm["norm_ffn"][l], prm["ffn_up"][l].astype(BF16), prm["ffn_conv"][l],
                        prev, "ffn", t_len, tm, D_FF // 2)
        x = _matmul_res(gt, prm["ffn_down"][l].astype(BF16), x, tm, 512)
        ffn_r.append(fs)
    stk = lambda a_: jnp.stack(a_, axis=1)
    return x.reshape(b, t_len, d), (stk(cmp_r), stk(slc_r), stk(win_r), stk(wkv_r), stk(sh_r),
                                    stk(conv_r), stk(ffn_r))


def kernel(x_prompt, x_sample, cache_cmp_kv, cache_slc_kv, cache_win_kv, state_rwkv_wkv, state_rwkv_shift, state_conv, state_ffn_conv, page_table, norm_mix, norm_ffn, rel_bias, w_in_even, w_out_even, q_norm, k_norm, phi_pe, phi_w1, phi_w2, rwkv_mu, rwkv_w0, rwkv_w2, rwkv_a0, rwkv_a2, rwkv_g2, rwkv_kk, rwkv_ka, rwkv_rk, rwkv_ln_w, rwkv_ln_b, w_in_odd, conv_w, w_out_odd, ffn_up, ffn_conv, ffn_down):
    prm = dict(norm_mix=norm_mix, norm_ffn=norm_ffn, rel_bias=rel_bias, w_in_even=w_in_even,
               w_out_even=w_out_even, q_norm=q_norm, k_norm=k_norm, phi_pe=phi_pe, phi_w1=phi_w1,
               phi_w2=phi_w2, rwkv_mu=rwkv_mu, rwkv_w0=rwkv_w0, rwkv_w2=rwkv_w2, rwkv_a0=rwkv_a0,
               rwkv_a2=rwkv_a2, rwkv_g2=rwkv_g2, rwkv_kk=rwkv_kk, rwkv_ka=rwkv_ka, rwkv_rk=rwkv_rk,
               rwkv_ln_w=rwkv_ln_w, rwkv_ln_b=rwkv_ln_b, w_in_odd=w_in_odd, conv_w=conv_w,
               w_out_odd=w_out_odd, ffn_up=ffn_up, ffn_conv=ffn_conv, ffn_down=ffn_down)
    st = dict(cmp=cache_cmp_kv, slc=cache_slc_kv, win=cache_win_kv, wkv=state_rwkv_wkv,
              shift=state_rwkv_shift, conv=state_conv, ffn=state_ffn_conv, page_table=page_table)
    y_p, (cmp_p, slc_p, win_p, wkv_p, sh_p, conv_p, ffn_p) = _trunk(x_prompt, prm, None)
    y_s, (cmp_s, slc_s, win_s, wkv_s, sh_s, conv_s, ffn_s) = _trunk(x_sample, prm, st)
    return (y_p, y_s, cmp_p, cmp_s, slc_p, slc_s, win_p, win_s, wkv_p, wkv_s,
            sh_p, sh_s, conv_p, conv_s, ffn_p, ffn_s)
```

```python
import functools
import math

import jax
import jax.numpy as jnp
from jax import lax
from jax.experimental import pallas as pl
from jax.experimental.pallas import tpu as pltpu

F32 = jnp.float32
BF16 = jnp.bfloat16
HI = lax.Precision.HIGHEST

D_MODEL = 1024
HEAD_DIM = 64
NSA_HEADS = 8
NSA_KV = 2
NSA_QPG = 4
NSA_WIDTH = 512
KV_LANES = 2 * NSA_KV * HEAD_DIM
CMP_LEN = 32
CMP_STRIDE = 16
CMP_HIDDEN = 128
SEL_LEN = 64
N_SELECT = 16
WINDOW = 512
FORCE_SCORE = 1e9
RWKV_HEADS = 8
RWKV_WIDTH = 512
DECAY_LORA = 64
AAA_LORA = 64
GATE_LORA = 128
RWKV_PROJ = 3 * RWKV_WIDTH + DECAY_LORA + AAA_LORA + GATE_LORA
NSA_PROJ = NSA_WIDTH + 6 * NSA_KV * HEAD_DIM + 3 * NSA_HEADS
NSA_PROJ_PAD = 1408
GATE_COL_BLOCK = 10
D_FF = 2816
NUM_BUCKETS = 32
MAX_DISTANCE = 128
RMS_EPS = 1e-6
GN_EPS = 64e-5
PAGE = 128
LANE = 128
KT = 128
NEG = -1e30
VMEM_LIMIT = 56 * 1024 * 1024


def _cp(*sem):
    return pltpu.CompilerParams(dimension_semantics=sem, vmem_limit_bytes=VMEM_LIMIT)


def _round_up(a, b):
    return (a + b - 1) // b * b


def _nt(a, b, precision=None):
    return lax.dot_general(a, b, (((1,), (1,)), ((), ())), precision=precision,
                           preferred_element_type=F32)


def _tn(a, b, precision=None):
    return lax.dot_general(a, b, (((0,), (0,)), ((), ())), precision=precision,
                           preferred_element_type=F32)


_NN = ((1,), (0,))
_NT = ((1,), (1,))
_TN = ((0,), (0,))
RWKV_MM = "bf16"


def _mmp(a, b, dims):
    dn = (dims, ((), ()))
    if RWKV_MM == "f32":
        return lax.dot_general(a, b, dn, precision=HI, preferred_element_type=F32)
    if RWKV_MM == "bf16":
        return lax.dot_general(a.astype(BF16), b.astype(BF16), dn, preferred_element_type=F32)
    ah = a.astype(BF16)
    al = (a - ah.astype(F32)).astype(BF16)
    bh = b.astype(BF16)
    bl = (b - bh.astype(F32)).astype(BF16)
    dot = lambda x, y: lax.dot_general(x, y, dn, preferred_element_type=F32)
    return dot(ah, bh) + (dot(ah, bl) + dot(al, bh))


def _norm_mm_kernel(x_ref, g_ref, w_ref, o_ref, xn_ref):
    @pl.when(pl.program_id(1) == 0)
    def _():
        x = x_ref[...]
        ms = jnp.mean(x * x, axis=-1, keepdims=True)
        xn_ref[...] = (x * lax.rsqrt(ms + RMS_EPS) * g_ref[...]).astype(BF16)

    o_ref[...] = _nt(xn_ref[...], w_ref[...])


def _norm_matmul(x, g, w_t, tm, tn):
    m, k = x.shape
    n = w_t.shape[0]
    return pl.pallas_call(
        _norm_mm_kernel, name="norm_mm",
        grid=(m // tm, n // tn),
        in_specs=[pl.BlockSpec((tm, k), lambda i, j: (i, 0)),
                  pl.BlockSpec((1, k), lambda i, j: (0, 0)),
                  pl.BlockSpec((tn, k), lambda i, j: (j, 0))],
        out_specs=pl.BlockSpec((tm, tn), lambda i, j: (i, j)),
        out_shape=jax.ShapeDtypeStruct((m, n), F32),
        scratch_shapes=[pltpu.VMEM((tm, k), BF16)],
        compiler_params=_cp("arbitrary", "arbitrary"),
    )(x, g.reshape(1, k), w_t)


def _mm_res_kernel(a_ref, w_ref, r_ref, o_ref):
    o_ref[...] = r_ref[...] + jnp.dot(a_ref[...], w_ref[...], preferred_element_type=F32)


def _matmul_res(a, w, res, tm, tn):
    m, k = a.shape
    n = w.shape[1]
    return pl.pallas_call(
        _mm_res_kernel, name="mm_res",
        grid=(m // tm, n // tn),
        in_specs=[pl.BlockSpec((tm, k), lambda i, j: (i, 0)),
                  pl.BlockSpec((k, tn), lambda i, j: (0, j)),
                  pl.BlockSpec((tm, tn), lambda i, j: (i, j))],
        out_specs=pl.BlockSpec((tm, tn), lambda i, j: (i, j)),
        out_shape=jax.ShapeDtypeStruct((m, n), F32),
        compiler_params=_cp("arbitrary", "arbitrary"),
    )(a, w, res)


def _mm_res_t_kernel(at_ref, b_ref, wa_ref, wb_ref, r_ref, o_ref):
    o_ref[...] = (r_ref[...] + _tn(at_ref[0], wa_ref[...])
                  + jnp.dot(b_ref[...], wb_ref[...], preferred_element_type=F32))


def _matmul_res_t(a_t, bmat, w, res, tm, tn):
    nb, ka, t_len = a_t.shape
    m, kb = bmat.shape
    n = w.shape[1]
    tpb = t_len // tm
    return pl.pallas_call(
        _mm_res_t_kernel, name="mm_res_t",
        grid=(m // tm, n // tn),
        in_specs=[pl.BlockSpec((1, ka, tm), lambda i, j: (i // tpb, 0, i % tpb)),
                  pl.BlockSpec((tm, kb), lambda i, j: (i, 0)),
                  pl.BlockSpec((ka, tn), lambda i, j: (0, j)),
                  pl.BlockSpec((kb, tn), lambda i, j: (ka // kb, j)),
                  pl.BlockSpec((tm, tn), lambda i, j: (i, j))],
        out_specs=pl.BlockSpec((tm, tn), lambda i, j: (i, j)),
        out_shape=jax.ShapeDtypeStruct((m, n), F32),
        compiler_params=_cp("arbitrary", "arbitrary"),
    )(a_t, bmat, w, w, res)


def _gated_kernel(x_ref, g_ref, *rest, mode, nbr, seq_tiles, nseq, lt):
    w_refs = rest[:nbr]
    cw_ref, prev_ref, o_ref, st_ref, xn_ref, carry_ref = rest[nbr:]
    i = pl.program_id(0)
    j = pl.program_id(1)

    @pl.when(j == 0)
    def _():
        x = x_ref[...]
        ms = jnp.mean(x * x, axis=-1, keepdims=True)
        xn_ref[...] = (x * lax.rsqrt(ms + RMS_EPS) * g_ref[...]).astype(BF16)

    xn = xn_ref[...]
    br = [jnp.dot(xn, w[...], preferred_element_type=F32) for w in w_refs]
    if mode == "ffn":
        cin, other = br
    else:
        other, cg, xi = br
        cin = cg * xi
    tf = cin.shape[-1]
    c = cin.reshape(nseq, lt, tf)
    if seq_tiles == 1:
        p = prev_ref[...]
    else:
        p = jnp.where(i % seq_tiles == 0, prev_ref[...], carry_ref[j])
    p0 = p[:, 0:1, :]
    p1 = p[:, 1:2, :]
    t = lax.broadcasted_iota(jnp.int32, c.shape, 1)
    c1 = jnp.where(t == 0, p1, pltpu.roll(c, 1, 1))
    c2 = jnp.where(t == 0, p0, jnp.where(t == 1, p1, pltpu.roll(c, 2, 1)))
    cw = cw_ref[...]
    y = c2 * cw[0:1, :] + c1 * cw[1:2, :] + c * cw[2:3, :]
    last2 = c[:, lt - 2:lt, :]
    for jj in range(st_ref.shape[-1] // tf):
        @pl.when(j == jj)
        def _(jj=jj):
            st_ref[:, :, jj * tf:(jj + 1) * tf] = last2
    if seq_tiles > 1:
        carry_ref[j] = last2
    o3 = other.reshape(c.shape)
    if mode == "ffn":
        out = (y * jax.nn.sigmoid(y)) * o3
    else:
        out = o3 * y
    o_ref[...] = out.reshape(nseq * lt, tf).astype(BF16)


def _gated(x, g, w, cw, prev, mode, seq_len, tm, tf):
    m, k = x.shape
    nbr = 2 if mode == "ffn" else 3
    f = w.shape[1] // nbr
    nj = f // tf
    if seq_len >= tm:
        seq_tiles, nseq, lt = seq_len // tm, 1, tm
    else:
        seq_tiles, nseq, lt = 1, tm // seq_len, seq_len
    b = prev.shape[0]
    w_specs = [pl.BlockSpec((k, tf), functools.partial(lambda i, j, o: (0, j + o), o=r * nj))
               for r in range(nbr)]
    kern = functools.partial(_gated_kernel, mode=mode, nbr=nbr, seq_tiles=seq_tiles, nseq=nseq, lt=lt)
    return pl.pallas_call(
        kern, name="gated_" + mode,
        grid=(m // tm, nj),
        in_specs=[pl.BlockSpec((tm, k), lambda i, j: (i, 0)),
                  pl.BlockSpec((1, k), lambda i, j: (0, 0))] + w_specs + [
                  pl.BlockSpec((3, tf), lambda i, j: (0, j)),
                  pl.BlockSpec((nseq, 2, tf), lambda i, j: (i // seq_tiles, 0, j))],
        out_specs=[pl.BlockSpec((tm, tf), lambda i, j: (i, j)),
                   pl.BlockSpec((nseq, 2, f), lambda i, j: (i // seq_tiles, 0, 0))],
        out_shape=[jax.ShapeDtypeStruct((m, f), BF16),
                   jax.ShapeDtypeStruct((b, 2, f), F32)],
        scratch_shapes=[pltpu.VMEM((tm, k), BF16),
                        pltpu.VMEM((nj, nseq, 2, tf), F32)],
        compiler_params=_cp("arbitrary", "arbitrary"),
    )(x, g.reshape(1, k), *([w] * nbr), cw, prev)


def _nsa_prep_kernel(z_ref, qg_ref, kg_ref, g512_ref, g128_ref, q_ref, kv_ref):
    z = z_ref[...]
    q = z[:, 0:NSA_WIDTH]
    ms = jnp.dot(q * q, g512_ref[...], precision=HI, preferred_element_type=F32)
    q_ref[...] = (q * lax.rsqrt(ms + RMS_EPS) * qg_ref[...] * (HEAD_DIM ** -0.5)).astype(BF16)
    kv_ref[:, 0:256] = z[:, 512:768]
    for r, off in ((0, 768), (1, 1024)):
        k = z[:, off:off + 128]
        ms = jnp.dot(k * k, g128_ref[...], precision=HI, preferred_element_type=F32)
        kv_ref[:, off - 512:off - 384] = k * lax.rsqrt(ms + RMS_EPS) * kg_ref[r:r + 1, :]
        kv_ref[:, off - 384:off - 256] = z[:, off + 128:off + 256]


def _nsa_prep(z, qg, kg, g512, g128, tm):
    m = z.shape[0]
    const = lambda i: (0, 0)
    return pl.pallas_call(
        _nsa_prep_kernel, name="nsa_prep",
        grid=(m // tm,),
        in_specs=[pl.BlockSpec((tm, NSA_PROJ_PAD), lambda i: (i, 0)),
                  pl.BlockSpec((1, 512), const), pl.BlockSpec((2, 128), const),
                  pl.BlockSpec((512, 512), const), pl.BlockSpec((128, 128), const)],
        out_specs=[pl.BlockSpec((tm, 512), lambda i: (i, 0)),
                   pl.BlockSpec((tm, 768), lambda i: (i, 0))],
        out_shape=[jax.ShapeDtypeStruct((m, 512), BF16),
                   jax.ShapeDtypeStruct((m, 768), F32)],
        compiler_params=_cp("arbitrary"),
    )(z, qg, kg, g512, g128)


def _compress_rows(rows_refs, m, pe_ref, w1_ref, w2_ref, kn_ref, g128_ref):
    outs = []
    for kv, rows_ref in enumerate(rows_refs):
        acc0 = jnp.zeros((m, NSA_KV * CMP_HIDDEN), F32)
        acc1 = jnp.zeros((m, NSA_KV * CMP_HIDDEN), F32)
        for j in range(CMP_STRIDE):
            xj = rows_ref[pl.ds(j, m, stride=CMP_STRIDE), :]
            acc0 = acc0 + jnp.dot((xj + pe_ref[kv, j:j + 1, :]).astype(BF16), w1_ref[kv, j],
                                  preferred_element_type=F32)
            acc1 = acc1 + jnp.dot((xj + pe_ref[kv, CMP_STRIDE + j:CMP_STRIDE + j + 1, :]).astype(BF16),
                                  w1_ref[kv, CMP_STRIDE + j], preferred_element_type=F32)
        hid = acc0 + pltpu.roll(acc1, m - 1, 0)
        outs.append(jnp.dot(jax.nn.gelu(hid).astype(BF16), w2_ref[kv], preferred_element_type=F32))
    kc, vc = outs
    ms = jnp.dot(kc * kc, g128_ref[...], precision=HI, preferred_element_type=F32)
    kc = kc * lax.rsqrt(ms + RMS_EPS) * kn_ref[...]
    return jnp.concatenate([kc, vc], axis=1)


def _compress_kernel(k_ref, v_ref, pe_ref, w1_ref, w2_ref, kn_ref, g128_ref, o_ref, *, m):
    o_ref[0] = _compress_rows((k_ref.at[0], v_ref.at[0]), m, pe_ref, w1_ref, w2_ref, kn_ref, g128_ref)


def _compress(kv, pe, w1, w2, kn, g128):
    b, t, _ = kv.shape
    m = t // CMP_STRIDE
    c2 = lambda i: (0, 0)
    hw = NSA_KV * HEAD_DIM
    return pl.pallas_call(
        functools.partial(_compress_kernel, m=m), name="compress",
        grid=(b,),
        in_specs=[pl.BlockSpec((1, t, hw), lambda i: (i, 0, 0)),
                  pl.BlockSpec((1, t, hw), lambda i: (i, 0, 1)),
                  pl.BlockSpec((2, CMP_LEN, hw), lambda i: (0, 0, 0)),
                  pl.BlockSpec((2, CMP_LEN, hw, NSA_KV * CMP_HIDDEN), lambda i: (0, 0, 0, 0)),
                  pl.BlockSpec((2, NSA_KV * CMP_HIDDEN, hw), lambda i: (0, 0, 0)),
                  pl.BlockSpec((1, 128), c2), pl.BlockSpec((128, 128), c2)],
        out_specs=pl.BlockSpec((1, m, KV_LANES), lambda i: (i, 0, 0)),
        out_shape=jax.ShapeDtypeStruct((b, m, KV_LANES), F32),
        compiler_params=_cp("arbitrary"),
    )(kv, kv, pe, w1, w2, kn, g128)


def _msoftmax(s, mask):
    s = jnp.where(mask, s, NEG)
    mx = jnp.max(s, axis=-1, keepdims=True)
    e = jnp.where(mask, jnp.exp(s - mx), 0.0)
    den = jnp.sum(e, axis=-1, keepdims=True)
    return e / jnp.where(den > 0, den, 1.0)


def _select_blocks(imp, qpos, ns):
    s_io = lax.broadcasted_iota(jnp.int32, imp.shape, 1)
    cur = qpos // SEL_LEN
    forced = (s_io == 0) | (s_io == cur) | (s_io == cur - 1)
    future = s_io * SEL_LEN > qpos
    imp = jnp.where(forced, FORCE_SCORE, imp)
    imp = jnp.where(future, -FORCE_SCORE, imp)
    imp = jnp.where(s_io >= ns, -3e38, imp)
    rank = jnp.zeros(imp.shape, jnp.int32)
    for sp in range(ns):
        col = imp[:, sp:sp + 1]
        beats = (col > imp) | ((col == imp) & (s_io > sp))
        rank = rank + beats.astype(jnp.int32)
    sel = (rank < min(N_SELECT, ns)) & (s_io < ns)
    return sel.astype(F32)


def _heads_to_rows(q, g):
    return jnp.concatenate(
        [q[:, HEAD_DIM * (NSA_QPG * g + h):HEAD_DIM * (NSA_QPG * g + h + 1)] for h in range(NSA_QPG)],
        axis=0)


def _msoftmax0(s, mask):
    s = jnp.where(mask, s, NEG)
    mx = jnp.max(s, axis=0, keepdims=True)
    e = jnp.where(mask, jnp.exp(s - mx), 0.0)
    den = jnp.sum(e, axis=0, keepdims=True)
    return e / jnp.where(den > 0, den, 1.0)


def _select_blocks_t(imp, qpos, ns):
    s_io = lax.broadcasted_iota(jnp.int32, imp.shape, 0)
    cur = qpos // SEL_LEN
    forced = (s_io == 0) | (s_io == cur) | (s_io == cur - 1)
    future = s_io * SEL_LEN > qpos
    imp = jnp.where(forced, FORCE_SCORE, imp)
    imp = jnp.where(future, -FORCE_SCORE, imp)
    imp = jnp.where(s_io >= ns, -3e38, imp)
    rank = jnp.zeros(imp.shape, jnp.int32)
    for sp in range(ns):
        row = imp[sp:sp + 1, :]
        beats = (row > imp) | ((row == imp) & (s_io > sp))
        rank = rank + beats.astype(jnp.int32)
    sel = (rank < min(N_SELECT, ns)) & (s_io < ns)
    return sel.astype(F32)


def _flash_steps_t(chains):
    sts = [_nt(k, qg) for (qg, k, _, _, _, _) in chains]
    mids = []
    for st, (_, _, _, bias, mask, (m_, l_, _)) in zip(sts, chains):
        ps, ms, ls, als = [], [], [], []
        for h in range(NSA_QPG):
            hs = slice(KT * h, KT * (h + 1))
            s = st[:, hs]
            if bias is not None:
                s = s + bias[:, hs]
            s = jnp.where(mask, s, NEG)
            m_new = jnp.maximum(m_[:, hs], jnp.max(s, axis=0, keepdims=True))
            alpha = jnp.exp(m_[:, hs] - m_new)
            e = jnp.where(mask, jnp.exp(s - m_new), 0.0)
            ls.append(alpha * l_[:, hs] + jnp.sum(e, axis=0, keepdims=True))
            ms.append(m_new)
            als.append(alpha)
            ps.append(e.astype(BF16))
        mids.append((jnp.concatenate(ps, axis=1), jnp.concatenate(ms, axis=1),
                     jnp.concatenate(ls, axis=1), jnp.concatenate(als, axis=1)))
    out = []
    for (p, m_new, l_new, alpha), (_, _, v, _, _, (_, _, acc)) in zip(mids, chains):
        out.append((m_new, l_new, alpha * acc + _tn(v, p)))
    return out


def _flash_init_t():
    return (jnp.full((1, NSA_QPG * KT), NEG, F32), jnp.zeros((1, NSA_QPG * KT), F32),
            jnp.zeros((HEAD_DIM, NSA_QPG * KT), F32))


def _flash_out_t(carry):
    _, l_, acc = carry
    return acc / jnp.where(l_ > 0, l_, 1.0)


def _nsa_prompt_kernel(q_ref, gt_ref, cc_ref, kv_ref, bc_ref, tb_ref, ovt_ref, ext_ref, o_ref, selk_s, sc_s,
                       *, m, ns):
    i = pl.program_id(1)
    q = q_ref[0]
    gst = jax.nn.sigmoid(gt_ref[...]).T
    k_io = lax.broadcasted_iota(jnp.int32, (KT, KT), 0)
    q_io = lax.broadcasted_iota(jnp.int32, (KT, KT), 1)
    qpos = i * KT + lax.broadcasted_iota(jnp.int32, (1, KT), 1)
    n_io = lax.broadcasted_iota(jnp.int32, (m, KT), 0)
    mask_c = ((n_io * CMP_STRIDE + (CMP_LEN - 1)) <= qpos) & (n_io < m - 1)
    cc = cc_ref[0]
    groups = range(NSA_KV)
    qgs = [_heads_to_rows(q, g) for g in groups]

    kcs = [cc[:, HEAD_DIM * g:HEAD_DIM * (g + 1)].astype(BF16) for g in groups]
    vcs = [cc[:, 128 + HEAD_DIM * g:128 + HEAD_DIM * (g + 1)].astype(BF16) for g in groups]
    c_off = pl.multiple_of(jnp.maximum(i - 1, 0) * (KT // CMP_STRIDE), KT // CMP_STRIDE)
    sts = []
    for g in groups:
        sc_s[g] = _nt(kcs[g], qgs[g])
        band = jnp.where(i == 0, bc_ref[g, 1], bc_ref[g, 0])
        sc_s[g, pl.ds(c_off, 2 * KT // CMP_STRIDE), :] += band
        sts.append(sc_s[g])
    pcs = [[_msoftmax0(sts[g][:, KT * h:KT * (h + 1)], mask_c) for h in range(NSA_QPG)]
           for g in groups]
    o_cs = [_tn(vcs[g], jnp.concatenate(pcs[g], axis=1).astype(BF16)) for g in groups]
    imps = [jnp.dot(ovt_ref[...], pcs[g][0] + pcs[g][1] + pcs[g][2] + pcs[g][3], precision=HI,
                    preferred_element_type=F32) for g in groups]
    sels = [_select_blocks_t(imps[g], qpos, ns).astype(BF16) for g in groups]
    for g in groups:
        selk_s[g] = jnp.dot(ext_ref[...], sels[g], preferred_element_type=F32)

    def load_kv(g, kt, off):
        r0 = pl.multiple_of(kt * KT, KT)
        k_lo = 256 + off + HEAD_DIM * g
        v_lo = 384 + off + HEAD_DIM * g
        k = kv_ref[0, pl.ds(r0, KT), k_lo:k_lo + HEAD_DIM].astype(BF16)
        v = kv_ref[0, pl.ds(r0, KT), v_lo:v_lo + HEAD_DIM].astype(BF16)
        return k, v

    def sel_mask(g, kt):
        r0 = pl.multiple_of(kt * KT, KT)
        return selk_s[g, pl.ds(r0, KT), :] > 0.5

    def body(kt, carry):
        chains = []
        for g in groups:
            k, v = load_kv(g, kt, 0)
            chains.append((qgs[g], k, v, None, sel_mask(g, kt), carry[g]))
        return tuple(_flash_steps_t(chains))

    c_sel = list(lax.fori_loop(0, jnp.maximum(i - 1, 0), body, tuple(_flash_init_t() for _ in groups)))

    nwt = WINDOW // KT
    c_win = [_flash_init_t() for _ in groups]
    for d in range(nwt, -1, -1):
        kt = i - d
        ktc = jnp.maximum(kt, 0)
        if d == nwt:
            mk = (k_io > q_io) & (kt >= 0)
        elif d == 0:
            mk = k_io <= q_io
        else:
            mk = kt >= 0
        chains = []
        for g in groups:
            k, v = load_kv(g, ktc, 256)
            bias = tb_ref[g, 0] if d == 1 else (tb_ref[g, 1] if d == 0 else None)
            chains.append((qgs[g], k, v, bias, mk, c_win[g]))
        if d <= 1:
            for g in groups:
                k, v = load_kv(g, ktc, 0)
                smk = sel_mask(g, ktc) & ((kt >= 0) if d == 1 else (k_io <= q_io))
                chains.append((qgs[g], k, v, tb_ref[g, 1 - d], smk, c_sel[g]))
        res = _flash_steps_t(chains)
        c_win = res[:NSA_KV]
        if d <= 1:
            c_sel = res[NSA_KV:]
    o_ss = [_flash_out_t(c_sel[g]) for g in groups]
    o_ws = [_flash_out_t(c_win[g]) for g in groups]

    for g in groups:
        for h in range(NSA_QPG):
            hh = NSA_QPG * g + h
            hs = slice(KT * h, KT * (h + 1))
            o = (gst[3 * hh:3 * hh + 1, :] * o_cs[g][:, hs] + gst[3 * hh + 1:3 * hh + 2, :] * o_ss[g][:, hs]
                 + gst[3 * hh + 2:3 * hh + 3, :] * o_ws[g][:, hs])
            o_ref[0, HEAD_DIM * hh:HEAD_DIM * (hh + 1), :] = o.astype(BF16)


def _nsa_prompt(qn, z_nsa, cc, kv, bct, tbt, ovt, ext, b, t_len):
    m = cc.shape[1]
    ns = -(-t_len // SEL_LEN)
    nq = t_len // KT
    nsp = ovt.shape[0]
    c2 = lambda bi, i: (0, 0)
    return pl.pallas_call(
        functools.partial(_nsa_prompt_kernel, m=m, ns=ns), name="nsa_prompt",
        grid=(b, nq),
        in_specs=[pl.BlockSpec((1, KT, 512), lambda bi, i: (bi, i, 0)),
                  pl.BlockSpec((KT, LANE), lambda bi, i: (bi * nq + i, GATE_COL_BLOCK)),
                  pl.BlockSpec((1, m, KV_LANES), lambda bi, i: (bi, 0, 0)),
                  pl.BlockSpec((1, t_len, 768), lambda bi, i: (bi, 0, 0)),
                  pl.BlockSpec((NSA_KV, 2, 2 * KT // CMP_STRIDE, NSA_QPG * KT), lambda bi, i: (0, 0, 0, 0)),
                  pl.BlockSpec((NSA_KV, 2, KT, NSA_QPG * KT), lambda bi, i: (0, 0, 0, 0)),
                  pl.BlockSpec((nsp, m), c2),
                  pl.BlockSpec((t_len, nsp), c2)],
        out_specs=pl.BlockSpec((1, 512, KT), lambda bi, i: (bi, 0, i)),
        out_shape=jax.ShapeDtypeStruct((b, 512, t_len), BF16),
        scratch_shapes=[pltpu.VMEM((NSA_KV, t_len, KT), F32),
                        pltpu.VMEM((NSA_KV, m, NSA_QPG * KT), F32)],
        compiler_params=_cp("arbitrary", "arbitrary"),
    )(qn.reshape(b, t_len, 512), z_nsa, cc, kv.reshape(b, t_len, 768), bct, tbt, ovt, ext)


def _nsa_sample_kernel(pt_ref, cmp_pg, slc_pg, new_ref, win_ref, q_ref, gt_ref, pe_ref, w1_ref, w2_ref,
                       kn_ref, g128_ref, bc_ref, bs_ref, bw_ref, ov_ref, ex_ref, o_ref,
                       cmpk_s, cmpv_s, slc_s, win_s, *, npages, past, wb, tq, m, ns):
    p = pl.program_id(1)
    r0 = pl.multiple_of(p * PAGE, PAGE)
    cmpk_s[pl.ds(r0, PAGE), :] = cmp_pg[0, 0, :, 0:128]
    cmpv_s[pl.ds(r0, PAGE), :] = cmp_pg[0, 0, :, 128:256]
    slc_s[pl.ds(r0, PAGE), :] = slc_pg[0, 0]

    @pl.when(p == npages - 1)
    def _():
        kp = slc_s.shape[0]
        wp = win_s.shape[0]
        new = new_ref[0]
        pad = jnp.zeros((kp - past - tq, KV_LANES), F32)
        cmpk_s[past:past + tq, :] = new[:, 0:128]
        cmpk_s[past + tq:kp, :] = pad[:, 0:128]
        cmpv_s[past:past + tq, :] = new[:, 128:256]
        cmpv_s[past + tq:kp, :] = pad[:, 0:128]
        slc_s[past:past + tq, :] = new[:, 256:512]
        slc_s[past + tq:kp, :] = pad
        win_s[0:wb, :] = win_ref[0, 0]
        win_s[wb:wb + tq, :] = new[:, 512:768]
        win_s[wb + tq:wp, :] = jnp.zeros((wp - wb - tq, KV_LANES), F32)

        cc = _compress_rows((cmpk_s, cmpv_s), m, pe_ref, w1_ref, w2_ref, kn_ref, g128_ref)
        q = q_ref[0]
        gs = jax.nn.sigmoid(gt_ref[0])
        qpos = past + lax.broadcasted_iota(jnp.int32, (tq, 1), 0)
        n_io = lax.broadcasted_iota(jnp.int32, (tq, m), 1)
        mask_c = ((n_io * CMP_STRIDE + (CMP_LEN - 1)) <= qpos) & (n_io < m - 1)
        k_io = lax.broadcasted_iota(jnp.int32, (tq, kp), 1)
        mask_s0 = (k_io <= qpos) & (k_io < past + tq)
        j_io = lax.broadcasted_iota(jnp.int32, (tq, wp), 1)
        dist_w = qpos - (past - wb + j_io)
        mask_w = (dist_w >= 0) & (dist_w < WINDOW) & (j_io < wb + tq)
        outs = []
        for g in range(NSA_KV):
            qg = _heads_to_rows(q, g)
            kc = cc[:, HEAD_DIM * g:HEAD_DIM * (g + 1)].astype(BF16)
            vc = cc[:, 128 + HEAD_DIM * g:128 + HEAD_DIM * (g + 1)].astype(BF16)
            s = _nt(qg, kc).reshape(NSA_QPG, tq, m) + bc_ref[NSA_QPG * g:NSA_QPG * (g + 1)]
            p_c = _msoftmax(s, mask_c[None])
            o_c = jnp.dot(p_c.reshape(NSA_QPG * tq, m).astype(BF16), vc,
                          preferred_element_type=F32).reshape(NSA_QPG, tq, HEAD_DIM)
            psum = p_c[0] + p_c[1] + p_c[2] + p_c[3]
            imp = jnp.dot(psum, ov_ref[...], precision=HI, preferred_element_type=F32)
            sel = _select_blocks(imp, qpos, ns)
            selk = jnp.dot(sel.astype(BF16), ex_ref[...], preferred_element_type=F32)

            ks = slc_s[:, HEAD_DIM * g:HEAD_DIM * (g + 1)].astype(BF16)
            vs = slc_s[:, 128 + HEAD_DIM * g:128 + HEAD_DIM * (g + 1)].astype(BF16)
            s = _nt(qg, ks).reshape(NSA_QPG, tq, kp) + bs_ref[NSA_QPG * g:NSA_QPG * (g + 1)]
            p_s = _msoftmax(s, ((selk > 0.5) & mask_s0)[None])
            o_s = jnp.dot(p_s.reshape(NSA_QPG * tq, kp).astype(BF16), vs,
                          preferred_element_type=F32).reshape(NSA_QPG, tq, HEAD_DIM)

            kw = win_s[:, HEAD_DIM * g:HEAD_DIM * (g + 1)].astype(BF16)
            vw = win_s[:, 128 + HEAD_DIM * g:128 + HEAD_DIM * (g + 1)].astype(BF16)
            s = _nt(qg, kw).reshape(NSA_QPG, tq, wp) + bw_ref[NSA_QPG * g:NSA_QPG * (g + 1)]
            p_w = _msoftmax(s, mask_w[None])
            o_w = jnp.dot(p_w.reshape(NSA_QPG * tq, wp).astype(BF16), vw,
                          preferred_element_type=F32).reshape(NSA_QPG, tq, HEAD_DIM)
            for h in range(NSA_QPG):
                hh = NSA_QPG * g + h
                o = (gs[:, 3 * hh:3 * hh + 1] * o_c[h] + gs[:, 3 * hh + 1:3 * hh + 2] * o_s[h]
                     + gs[:, 3 * hh + 2:3 * hh + 3] * o_w[h])
                outs.append(o)
        o_ref[0] = jnp.concatenate(outs, axis=1).astype(BF16)


def _nsa_sample(pt, pool_cmp, pool_slc, li, kv_new, win, qn, z_nsa, cw, bc, bs, bw, ov, ex, b, tq):
    npages = pt.shape[1]
    past = npages * PAGE
    wb = win.shape[2]
    kp = _round_up(past + tq, LANE)
    wp = _round_up(wb + tq, LANE)
    m = (past + tq) // CMP_STRIDE
    ns = -(-(past + tq) // SEL_LEN)
    pe, w1, w2, kn, g128 = cw
    c2 = lambda bi, p, pt_: (0, 0)
    c3 = lambda bi, p, pt_: (0, 0, 0)
    per_b3 = lambda bi, p, pt_: (bi, 0, 0)
    grid_spec = pltpu.PrefetchScalarGridSpec(
        num_scalar_prefetch=1,
        grid=(b, npages),
        in_specs=[pl.BlockSpec((1, 1, PAGE, KV_LANES), lambda bi, p, pt_: (pt_[bi, p], li, 0, 0)),
                  pl.BlockSpec((1, 1, PAGE, KV_LANES), lambda bi, p, pt_: (pt_[bi, p], li, 0, 0)),
                  pl.BlockSpec((1, tq, 768), per_b3),
                  pl.BlockSpec((1, 1, wb, KV_LANES), lambda bi, p, pt_: (bi, li, 0, 0)),
                  pl.BlockSpec((1, tq, 512), per_b3),
                  pl.BlockSpec((1, tq, LANE), lambda bi, p, pt_: (bi, 0, GATE_COL_BLOCK)),
                  pl.BlockSpec((2, CMP_LEN, LANE), c3),
                  pl.BlockSpec((2, CMP_LEN, LANE, NSA_KV * CMP_HIDDEN), lambda bi, p, pt_: (0, 0, 0, 0)),
                  pl.BlockSpec((2, NSA_KV * CMP_HIDDEN, LANE), c3),
                  pl.BlockSpec((1, 128), c2), pl.BlockSpec((128, 128), c2),
                  pl.BlockSpec((NSA_HEADS, tq, m), c3),
                  pl.BlockSpec((NSA_HEADS, tq, kp), c3),
                  pl.BlockSpec((NSA_HEADS, tq, wp), c3),
                  pl.BlockSpec((m, LANE), c2),
                  pl.BlockSpec((LANE, kp), c2)],
        out_specs=pl.BlockSpec((1, tq, 512), per_b3),
        scratch_shapes=[pltpu.VMEM((kp, LANE), F32), pltpu.VMEM((kp, LANE), F32),
                        pltpu.VMEM((kp, KV_LANES), F32), pltpu.VMEM((wp, KV_LANES), F32)])
    kern = functools.partial(_nsa_sample_kernel, npages=npages, past=past, wb=wb, tq=tq, m=m, ns=ns)
    return pl.pallas_call(
        kern, grid_spec=grid_spec, name="nsa_sample",
        out_shape=jax.ShapeDtypeStruct((b, tq, 512), BF16),
        compiler_params=_cp("arbitrary", "arbitrary"),
    )(pt, pool_cmp, pool_slc, kv_new.reshape(b, tq, 768), win, qn.reshape(b, tq, 512),
      z_nsa.reshape(b, tq, NSA_PROJ_PAD), pe, w1, w2, kn, g128, bc, bs, bw, ov, ex)


def _rwkv_kernel(z_ref, zp_ref, s0_ref, mu_ref, w0_ref, w2_ref, a0_ref, a2_ref, g2_ref, kkp_ref, ka_ref,
                 rk_ref, lnw_ref, lnb_ref, y_ref, st_ref, s_scr, carry_scr, *, c_len):
    c = pl.program_id(1)
    cl = c_len

    @pl.when(c == 0)
    def _():
        s_scr[...] = s0_ref[0]
        carry_scr[...] = zp_ref[0]

    z = z_ref[0]
    row = lax.broadcasted_iota(jnp.int32, z.shape, 0)
    shifted = jnp.where(row == 0, carry_scr[...], pltpu.roll(z, 1, 0))
    carry_scr[...] = z[cl - 1:cl, :]
    zz = z + (shifted - z) * mu_ref[...]
    w = RWKV_WIDTH
    r = zz[:, 0:w]
    k = zz[:, w:2 * w]
    v = zz[:, 2 * w:3 * w]
    zw = zz[:, 3 * w:3 * w + DECAY_LORA]
    za = zz[:, 3 * w + DECAY_LORA:3 * w + DECAY_LORA + AAA_LORA]
    zg = zz[:, 3 * w + DECAY_LORA + AAA_LORA:]
    xw = -(w0_ref[...] + jnp.dot(jnp.tanh(zw), w2_ref[...], precision=HI, preferred_element_type=F32))
    softplus = jnp.maximum(xw, 0.0) + jnp.log(1.0 + jnp.exp(-jnp.abs(xw)))
    logdec = -jnp.exp(-softplus - 0.5)
    a = jax.nn.sigmoid(a0_ref[...] + jnp.dot(za, a2_ref[...], precision=HI, preferred_element_type=F32))
    gate = jnp.dot(jax.nn.sigmoid(zg), g2_ref[...], precision=HI, preferred_element_type=F32)
    k_mod = k * (1.0 + (a - 1.0) * ka_ref[...])
    kku = k * kkp_ref[...]

    ti = lax.broadcasted_iota(jnp.int32, (cl, cl), 0)
    si = lax.broadcasted_iota(jnp.int32, (cl, cl), 1)
    lower = si <= ti
    strict = si < ti
    gcum = jnp.dot(lower.astype(F32), logdec, precision=HI, preferred_element_type=F32)
    glast = gcum[cl - 1:cl, :]
    e_g = jnp.exp(gcum)
    e_gm = jnp.exp(gcum - logdec)
    e_ng = jnp.exp(-gcum)
    e_lg = jnp.exp(glast - gcum)
    e_l = jnp.exp(glast)
    nsteps = max(1, int(math.ceil(math.log2(cl))))
    row2 = lax.broadcasted_iota(jnp.int32, (cl, 2 * cl), 0)
    col2 = lax.broadcasted_iota(jnp.int32, (cl, 2 * cl), 1)
    col2 = jnp.where(col2 >= cl, col2 - cl, col2)
    strict2 = col2 < row2
    lower2 = col2 <= row2
    keep_z = lax.broadcasted_iota(jnp.int32, (cl, cl + HEAD_DIM), 1) >= cl

    heads = range(RWKV_HEADS)
    sls = [slice(HEAD_DIM * h, HEAD_DIM * (h + 1)) for h in heads]
    kks = [kku[:, sl] for sl in sls]
    kks = [kk * lax.rsqrt(jnp.maximum(jnp.sum(kk * kk, axis=-1, keepdims=True), 1e-24)) for kk in kks]
    kkas = [kks[h] * a[:, sls[h]] for h in heads]
    kms = [k_mod[:, sl] for sl in sls]
    vhs = [v[:, sl] for sl in sls]
    rhs_ = [r[:, sl] for sl in sls]
    s_hs = [s_scr[h] for h in heads]
    lrs = [jnp.concatenate([-kks[h] * e_gm[:, sls[h]], rhs_[h] * e_g[:, sls[h]]], axis=0) for h in heads]
    rrs = [jnp.concatenate([kkas[h] * e_ng[:, sls[h]], kms[h] * e_ng[:, sls[h]]], axis=0) for h in heads]
    m1s = [_mmp(lrs[h], rrs[h], _NT) for h in heads]
    m2s = [_mmp(lrs[h], s_hs[h], _NT) for h in heads]
    tops = [jnp.where(strict2, m1[0:cl], 0.0) for m1 in m1s]
    bots = [jnp.where(lower2, m1[cl:2 * cl], 0.0) for m1 in m1s]
    zeros_v = jnp.zeros((cl, HEAD_DIM), F32)
    rhss = [m2s[h][0:cl] + _mmp(tops[h], jnp.concatenate([zeros_v, vhs[h]], axis=0), _NN) for h in heads]
    wms = [jnp.concatenate([tops[h][:, 0:cl], rhss[h]], axis=1) for h in heads]
    for _ in range(nsteps):
        wms = [_mmp(w_[:, 0:cl], w_, _NN) + jnp.where(keep_z, w_, 0.0) for w_ in wms]
    zvs = [jnp.concatenate([wms[h][:, cl:cl + HEAD_DIM], vhs[h]], axis=0) for h in heads]
    ys = [m2s[h][cl:2 * cl] + _mmp(bots[h], zvs[h], _NN) for h in heads]
    bkhs = [jnp.concatenate([kkas[h] * e_lg[:, sls[h]], kms[h] * e_lg[:, sls[h]]], axis=0) for h in heads]
    s_new = [s_hs[h] * e_l[:, sls[h]] + _mmp(zvs[h], bkhs[h], _TN) for h in heads]
    for h in heads:
        s_scr[h] = s_new[h]
    outs = []
    for h in heads:
        y = ys[h]
        sl = sls[h]
        mean = jnp.mean(y, axis=-1, keepdims=True)
        yc = y - mean
        var = jnp.mean(yc * yc, axis=-1, keepdims=True)
        yn = yc * lax.rsqrt(var + GN_EPS) * lnw_ref[:, sl] + lnb_ref[:, sl]
        bonus = jnp.sum(rhs_[h] * kms[h] * rk_ref[:, sl], axis=-1, keepdims=True) * vhs[h]
        outs.append(yn + bonus)
    y_ref[0] = (jnp.concatenate(outs, axis=1) * gate).astype(BF16)
    st_ref[0] = s_scr[...]


def _rwkv(z_rw, z_prev, s0, pr, b, t_len, c_len):
    nchunk = t_len // c_len
    c2 = lambda bi, c: (0, 0)
    vec = lambda n: pl.BlockSpec((1, n), c2)
    return pl.pallas_call(
        functools.partial(_rwkv_kernel, c_len=c_len), name="rwkv",
        grid=(b, nchunk),
        in_specs=[pl.BlockSpec((1, c_len, RWKV_PROJ), lambda bi, c: (bi, c, 0)),
                  pl.BlockSpec((1, 1, RWKV_PROJ), lambda bi, c: (bi, 0, 0)),
                  pl.BlockSpec((1, RWKV_HEADS, HEAD_DIM, HEAD_DIM), lambda bi, c: (bi, 0, 0, 0)),
                  vec(RWKV_PROJ), vec(512), pl.BlockSpec((DECAY_LORA, 512), c2),
                  vec(512), pl.BlockSpec((AAA_LORA, 512), c2), pl.BlockSpec((GATE_LORA, 512), c2),
                  vec(512), vec(512), vec(512), vec(512), vec(512)],
        out_specs=[pl.BlockSpec((1, c_len, 512), lambda bi, c: (bi, c, 0)),
                   pl.BlockSpec((1, RWKV_HEADS, HEAD_DIM, HEAD_DIM), lambda bi, c: (bi, 0, 0, 0))],
        out_shape=[jax.ShapeDtypeStruct((b, t_len, 512), BF16),
                   jax.ShapeDtypeStruct((b, RWKV_HEADS, HEAD_DIM, HEAD_DIM), F32)],
        scratch_shapes=[pltpu.VMEM((RWKV_HEADS, HEAD_DIM, HEAD_DIM), F32),
                        pltpu.VMEM((1, RWKV_PROJ), F32)],
        compiler_params=_cp("arbitrary", "arbitrary"),
    )(z_rw.reshape(b, t_len, RWKV_PROJ), z_prev.reshape(b, 1, RWKV_PROJ), s0, *pr)


def _rel_bucket(dist):
    n = jnp.maximum(dist, 0)
    max_exact = NUM_BUCKETS // 2
    nf = jnp.maximum(n, 1).astype(F32)
    large = max_exact + (jnp.log(nf / max_exact) / math.log(MAX_DISTANCE / max_exact)
                         * (NUM_BUCKETS - max_exact)).astype(jnp.int32)
    large = jnp.minimum(large, NUM_BUCKETS - 1)
    return jnp.where(n < max_exact, n, large)


def _bias_table(rel_bias, dist, delta):
    tb = rel_bias.astype(F32)
    out = jnp.moveaxis(tb[_rel_bucket(dist)], -1, 0)
    if delta:
        out = out - tb[NUM_BUCKETS - 1].reshape((NSA_HEADS,) + (1,) * dist.ndim)
    return out


def _block_diag_ones(n, grp):
    idx = jnp.arange(n) // grp
    return (idx[:, None] == idx[None, :]).astype(F32) / grp


def _compress_weights(phi_pe, phi_w1, phi_w2, kn_cmp):
    pe = jnp.concatenate([phi_pe, phi_pe], axis=-1)
    w1 = jnp.zeros((2, CMP_LEN, NSA_KV * HEAD_DIM, NSA_KV * CMP_HIDDEN), F32)
    w2 = jnp.zeros((2, NSA_KV * CMP_HIDDEN, NSA_KV * HEAD_DIM), F32)
    for g in range(NSA_KV):
        w1 = w1.at[:, :, HEAD_DIM * g:HEAD_DIM * (g + 1), CMP_HIDDEN * g:CMP_HIDDEN * (g + 1)].set(phi_w1)
        w2 = w2.at[:, CMP_HIDDEN * g:CMP_HIDDEN * (g + 1), HEAD_DIM * g:HEAD_DIM * (g + 1)].set(phi_w2)
    kn = jnp.tile(kn_cmp, NSA_KV).reshape(1, 128)
    return pe, w1.astype(BF16), w2.astype(BF16), kn


def _overlap_expand(t_all):
    nc = t_all // CMP_STRIDE - 1
    m = nc + 1
    ns = -(-t_all // SEL_LEN)
    c0 = jnp.arange(m)[:, None] * CMP_STRIDE
    s0 = jnp.arange(LANE)[None, :] * SEL_LEN
    ov = ((c0 < s0 + SEL_LEN) & (c0 + CMP_LEN > s0) & (jnp.arange(m)[:, None] < nc)
          & (jnp.arange(LANE)[None, :] < ns)).astype(F32)
    kp = _round_up(t_all, LANE)
    ex = (jnp.arange(kp)[None, :] // SEL_LEN == jnp.arange(LANE)[:, None]).astype(BF16)
    return ov, ex


def _pick_tm(t_len):
    return 512 if t_len % 512 == 0 else 128


def _trunk(x3, prm, st):
    b, t_len, d = x3.shape
    m_rows = b * t_len
    x = x3.reshape(m_rows, d)
    prompt = st is None
    if prompt:
        tm = _pick_tm(t_len)
        c_len = 64
    else:
        tm = 256 if m_rows % 256 == 0 else m_rows
        c_len = t_len
    g512 = _block_diag_ones(512, HEAD_DIM)
    g128 = _block_diag_ones(128, HEAD_DIM)
    rel_bias = prm["rel_bias"]

    if prompt:
        ov, ex = _overlap_expand(t_len)
        nsp = _round_up(-(-t_len // SEL_LEN), 8)
        ovt = ov[:, :nsp].T
        ext = ex[:nsp, :t_len].T
        kk_ = jnp.arange(KT)[:, None]
        qq_ = jnp.arange(KT)[None, :]

        def band(dist):
            rows = dist.shape[0]
            t_ = _bias_table(rel_bias, dist, True).reshape(NSA_KV, NSA_QPG, rows, KT)
            return jnp.transpose(t_, (0, 2, 1, 3)).reshape(NSA_KV, rows, NSA_QPG * KT)

        tbt = jnp.stack([band(KT + qq_ - kk_), band(qq_ - kk_)], axis=1)
        cj = jnp.arange(2 * KT // CMP_STRIDE)[:, None] * CMP_STRIDE + (CMP_LEN - 1)
        bct = jnp.stack([band(KT + qq_ - cj), band(qq_ - cj)], axis=1)
    else:
        pt = st["page_table"]
        past = pt.shape[1] * PAGE
        wb = st["win"].shape[2]
        t_all = past + t_len
        ov, ex = _overlap_expand(t_all)
        mcmp = t_all // CMP_STRIDE
        kp = _round_up(t_all, LANE)
        wp = _round_up(wb + t_len, LANE)
        qp = past + jnp.arange(t_len)[:, None]
        cend = jnp.arange(mcmp)[None, :] * CMP_STRIDE + (CMP_LEN - 1)
        bc = _bias_table(rel_bias, qp - cend, False)
        bs = _bias_table(rel_bias, qp - jnp.arange(kp)[None, :], True)
        bw = _bias_table(rel_bias, qp - (past - wb + jnp.arange(wp)[None, :]), True)
        pool_cmp = st["cmp"].reshape(st["cmp"].shape[0], st["cmp"].shape[1], PAGE, KV_LANES)
        pool_slc = st["slc"].reshape(st["slc"].shape[0], st["slc"].shape[1], PAGE, KV_LANES)
        win_all = st["win"].reshape(b, st["win"].shape[1], wb, KV_LANES)

    cmp_r, slc_r, win_r, wkv_r, sh_r, conv_r, ffn_r = [], [], [], [], [], [], []
    for l in range(4):
        li = l // 2
        if l % 2 == 0:
            w_in = prm["w_in_even"][li].T
            w_nsa = jnp.pad(w_in[:NSA_PROJ], ((0, NSA_PROJ_PAD - NSA_PROJ), (0, 0))).astype(BF16)
            w_rw = w_in[NSA_PROJ:].astype(BF16)
            z_nsa = _norm_matmul(x, prm["norm_mix"][l], w_nsa, tm, NSA_PROJ_PAD)
            z_rw = _norm_matmul(x, prm["norm_mix"][l], w_rw, tm, RWKV_PROJ // 2)
            qg = jnp.tile(prm["q_norm"][li], NSA_HEADS).reshape(1, 512)
            kg = jnp.stack([jnp.tile(prm["k_norm"][li, 1], NSA_KV), jnp.tile(prm["k_norm"][li, 2], NSA_KV)])
            qn, kv3 = _nsa_prep(z_nsa, qg, kg, g512, g128, tm)
            cw = _compress_weights(prm["phi_pe"][li], prm["phi_w1"][li], prm["phi_w2"][li],
                                   prm["k_norm"][li, 0])
            kv3b = kv3.reshape(b, t_len, 768)
            if prompt:
                cc = _compress(kv3b, *cw, g128)
                nsa = _nsa_prompt(qn, z_nsa, cc, kv3, bct, tbt, ovt, ext, b, t_len)
                z_prev = jnp.zeros((b, RWKV_PROJ), F32)
                s0 = jnp.zeros((b, RWKV_HEADS, HEAD_DIM, HEAD_DIM), F32)
                n_keep = min(WINDOW, t_len)
                win_rows = kv3b[:, t_len - n_keep:, 512:768]
            else:
                nsa = _nsa_sample(pt, pool_cmp, pool_slc, li, kv3, win_all, qn, z_nsa, cw + (g128,),
                                  bc, bs, bw, ov, ex, b, t_len)
                z_prev = st["shift"][:, li]
                s0 = st["wkv"][:, li]
                n_keep = min(WINDOW, wb + t_len)
                win_rows = jnp.concatenate([win_all[:, li], kv3b[:, :, 512:768]], axis=1)[:, -n_keep:]
            pr = (prm["rwkv_mu"][li].reshape(1, -1), prm["rwkv_w0"][li].reshape(1, -1), prm["rwkv_w2"][li],
                  prm["rwkv_a0"][li].reshape(1, -1), prm["rwkv_a2"][li], prm["rwkv_g2"][li],
                  prm["rwkv_kk"][li].reshape(1, -1), prm["rwkv_ka"][li].reshape(1, -1),
                  prm["rwkv_rk"][li].reshape(1, -1), prm["rwkv_ln_w"][li].reshape(1, -1),
                  prm["rwkv_ln_b"][li].reshape(1, -1))
            rw, s_t = _rwkv(z_rw, z_prev, s0, pr, b, t_len, c_len)
            w_out = prm["w_out_even"][li].astype(BF16)
            if prompt:
                x = _matmul_res_t(nsa, rw.reshape(m_rows, 512), w_out, x, tm, 512)
            else:
                mix = jnp.concatenate([nsa.reshape(m_rows, 512), rw.reshape(m_rows, 512)], axis=1)
                x = _matmul_res(mix, w_out, x, tm, 512)
            cmp_r.append(kv3b[:, :, 0:256].reshape(b, t_len, 2, NSA_KV, HEAD_DIM))
            slc_r.append(kv3b[:, :, 256:512].reshape(b, t_len, 2, NSA_KV, HEAD_DIM))
            win_r.append(win_rows.reshape(b, n_keep, 2, NSA_KV, HEAD_DIM))
            wkv_r.append(s_t)
            sh_r.append(z_rw.reshape(b, t_len, RWKV_PROJ)[:, -1])
        else:
            prev = jnp.zeros((b, 2, D_MODEL), F32) if prompt else st["conv"][:, li]
            gt, cs = _gated(x, prm["norm_mix"][l], prm["w_in_odd"][li].astype(BF16), prm["conv_w"][li],
                            prev, "odd", t_len, tm, 512)
            x = _matmul_res(gt, prm["w_out_odd"][li].astype(BF16), x, tm, 512)
            conv_r.append(cs)
        prev = jnp.zeros((b, 2, D_FF), F32) if prompt else st["ffn"][:, l]
        gt, fs = _gated(x, prm["norm_ffn"][l], prm["ffn_up"][l].astype(BF16), prm["ffn_conv"][l],
                        prev, "ffn", t_len, tm, D_FF // 2)
        x = _matmul_res(gt, prm["ffn_down"][l].astype(BF16), x, tm, 512)
        ffn_r.append(fs)
    stk = lambda a_: jnp.stack(a_, axis=1)
    return x.reshape(b, t_len, d), (stk(cmp_r), stk(slc_r), stk(win_r), stk(wkv_r), stk(sh_r),
                                    stk(conv_r), stk(ffn_r))


def kernel(x_prompt, x_sample, cache_cmp_kv, cache_slc_kv, cache_win_kv, state_rwkv_wkv, state_rwkv_shift, state_conv, state_ffn_conv, page_table, norm_mix, norm_ffn, rel_bias, w_in_even, w_out_even, q_norm, k_norm, phi_pe, phi_w1, phi_w2, rwkv_mu, rwkv_w0, rwkv_w2, rwkv_a0, rwkv_a2, rwkv_g2, rwkv_kk, rwkv_ka, rwkv_rk, rwkv_ln_w, rwkv_ln_b, w_in_odd, conv_w, w_out_odd, ffn_up, ffn_conv, ffn_down):
    prm = dict(norm_mix=norm_mix, norm_ffn=norm_ffn, rel_bias=rel_bias, w_in_even=w_in_even,
               w_out_even=w_out_even, q_norm=q_norm, k_norm=k_norm, phi_pe=phi_pe, phi_w1=phi_w1,
               phi_w2=phi_w2, rwkv_mu=rwkv_mu, rwkv_w0=rwkv_w0, rwkv_w2=rwkv_w2, rwkv_a0=rwkv_a0,
               rwkv_a2=rwkv_a2, rwkv_g2=rwkv_g2, rwkv_kk=rwkv_kk, rwkv_ka=rwkv_ka, rwkv_rk=rwkv_rk,
               rwkv_ln_w=rwkv_ln_w, rwkv_ln_b=rwkv_ln_b, w_in_odd=w_in_odd, conv_w=conv_w,
               w_out_odd=w_out_odd, ffn_up=ffn_up, ffn_conv=ffn_conv, ffn_down=ffn_down)
    st = dict(cmp=cache_cmp_kv, slc=cache_slc_kv, win=cache_win_kv, wkv=state_rwkv_wkv,
              shift=state_rwkv_shift, conv=state_conv, ffn=state_ffn_conv, page_table=page_table)
    y_p, (cmp_p, slc_p, win_p, wkv_p, sh_p, conv_p, ffn_p) = _trunk(x_prompt, prm, None)
    y_s, (cmp_s, slc_s, win_s, wkv_s, sh_s, conv_s, ffn_s) = _trunk(x_sample, prm, st)
    return (y_p, y_s, cmp_p, cmp_s, slc_p, slc_s, win_p, win_s, wkv_p, wkv_s,
            sh_p, sh_s, conv_p, conv_s, ffn_p, ffn_s)
```

```python
import functools
import math

import jax
import jax.numpy as jnp
from jax import lax
from jax.experimental import pallas as pl
from jax.experimental.pallas import tpu as pltpu

F32 = jnp.float32
BF16 = jnp.bfloat16
HI = lax.Precision.HIGHEST

D_MODEL = 1024
HEAD_DIM = 64
NSA_HEADS = 8
NSA_KV = 2
NSA_QPG = 4
NSA_WIDTH = 512
KV_LANES = 2 * NSA_KV * HEAD_DIM
CMP_LEN = 32
CMP_STRIDE = 16
CMP_HIDDEN = 128
SEL_LEN = 64
N_SELECT = 16
WINDOW = 512
FORCE_SCORE = 1e9
RWKV_HEADS = 8
RWKV_WIDTH = 512
DECAY_LORA = 64
AAA_LORA = 64
GATE_LORA = 128
RWKV_PROJ = 3 * RWKV_WIDTH + DECAY_LORA + AAA_LORA + GATE_LORA
NSA_PROJ = NSA_WIDTH + 6 * NSA_KV * HEAD_DIM + 3 * NSA_HEADS
NSA_PROJ_PAD = 1408
GATE_COL_BLOCK = 10
D_FF = 2816
NUM_BUCKETS = 32
MAX_DISTANCE = 128
RMS_EPS = 1e-6
GN_EPS = 64e-5
PAGE = 128
LANE = 128
KT = 128
NEG = -1e30
VMEM_LIMIT = 56 * 1024 * 1024


def _cp(*sem):
    return pltpu.CompilerParams(dimension_semantics=sem, vmem_limit_bytes=VMEM_LIMIT)


def _round_up(a, b):
    return (a + b - 1) // b * b


def _nt(a, b, precision=None):
    return lax.dot_general(a, b, (((1,), (1,)), ((), ())), precision=precision,
                           preferred_element_type=F32)


def _tn(a, b, precision=None):
    return lax.dot_general(a, b, (((0,), (0,)), ((), ())), precision=precision,
                           preferred_element_type=F32)


_NN = ((1,), (0,))
_NT = ((1,), (1,))
_TN = ((0,), (0,))
RWKV_MM = "bf16"


def _mmp(a, b, dims):
    dn = (dims, ((), ()))
    if RWKV_MM == "f32":
        return lax.dot_general(a, b, dn, precision=HI, preferred_element_type=F32)
    if RWKV_MM == "bf16":
        return lax.dot_general(a.astype(BF16), b.astype(BF16), dn, preferred_element_type=F32)
    ah = a.astype(BF16)
    al = (a - ah.astype(F32)).astype(BF16)
    bh = b.astype(BF16)
    bl = (b - bh.astype(F32)).astype(BF16)
    dot = lambda x, y: lax.dot_general(x, y, dn, preferred_element_type=F32)
    return dot(ah, bh) + (dot(ah, bl) + dot(al, bh))


def _norm_mm_kernel(x_ref, g_ref, w_ref, o_ref, xn_ref):
    @pl.when(pl.program_id(1) == 0)
    def _():
        x = x_ref[...]
        ms = jnp.mean(x * x, axis=-1, keepdims=True)
        xn_ref[...] = (x * lax.rsqrt(ms + RMS_EPS) * g_ref[...]).astype(BF16)

    o_ref[...] = _nt(xn_ref[...], w_ref[...])


def _norm_matmul(x, g, w_t, tm, tn):
    m, k = x.shape
    n = w_t.shape[0]
    return pl.pallas_call(
        _norm_mm_kernel, name="norm_mm",
        grid=(m // tm, n // tn),
        in_specs=[pl.BlockSpec((tm, k), lambda i, j: (i, 0)),
                  pl.BlockSpec((1, k), lambda i, j: (0, 0)),
                  pl.BlockSpec((tn, k), lambda i, j: (j, 0))],
        out_specs=pl.BlockSpec((tm, tn), lambda i, j: (i, j)),
        out_shape=jax.ShapeDtypeStruct((m, n), F32),
        scratch_shapes=[pltpu.VMEM((tm, k), BF16)],
        compiler_params=_cp("arbitrary", "arbitrary"),
    )(x, g.reshape(1, k), w_t)


def _mm_res_kernel(a_ref, w_ref, r_ref, o_ref):
    o_ref[...] = r_ref[...] + jnp.dot(a_ref[...], w_ref[...], preferred_element_type=F32)


def _matmul_res(a, w, res, tm, tn):
    m, k = a.shape
    n = w.shape[1]
    return pl.pallas_call(
        _mm_res_kernel, name="mm_res",
        grid=(m // tm, n // tn),
        in_specs=[pl.BlockSpec((tm, k), lambda i, j: (i, 0)),
                  pl.BlockSpec((k, tn), lambda i, j: (0, j)),
                  pl.BlockSpec((tm, tn), lambda i, j: (i, j))],
        out_specs=pl.BlockSpec((tm, tn), lambda i, j: (i, j)),
        out_shape=jax.ShapeDtypeStruct((m, n), F32),
        compiler_params=_cp("arbitrary", "arbitrary"),
    )(a, w, res)


def _mm_res_t_kernel(at_ref, b_ref, wa_ref, wb_ref, r_ref, o_ref):
    o_ref[...] = (r_ref[...] + _tn(at_ref[0], wa_ref[...])
                  + jnp.dot(b_ref[...], wb_ref[...], preferred_element_type=F32))


def _matmul_res_t(a_t, bmat, w, res, tm, tn):
    nb, ka, t_len = a_t.shape
    m, kb = bmat.shape
    n = w.shape[1]
    tpb = t_len // tm
    return pl.pallas_call(
        _mm_res_t_kernel, name="mm_res_t",
        grid=(m // tm, n // tn),
        in_specs=[pl.BlockSpec((1, ka, tm), lambda i, j: (i // tpb, 0, i % tpb)),
                  pl.BlockSpec((tm, kb), lambda i, j: (i, 0)),
                  pl.BlockSpec((ka, tn), lambda i, j: (0, j)),
                  pl.BlockSpec((kb, tn), lambda i, j: (ka // kb, j)),
                  pl.BlockSpec((tm, tn), lambda i, j: (i, j))],
        out_specs=pl.BlockSpec((tm, tn), lambda i, j: (i, j)),
        out_shape=jax.ShapeDtypeStruct((m, n), F32),
        compiler_params=_cp("arbitrary", "arbitrary"),
    )(a_t, bmat, w, w, res)


FUSED_CHUNK = 256


def _fused_kernel(x_ref, g_ref, win_ref, wout_ref, cw_ref, prev_ref, o_ref, st_ref, carry_ref,
                  *, mode, nbr, seq_tiles, nseq, lt):
    i = pl.program_id(0)
    x = x_ref[...]
    ms = jnp.mean(x * x, axis=-1, keepdims=True)
    xn = (x * lax.rsqrt(ms + RMS_EPS) * g_ref[...]).astype(BF16)
    f = cw_ref.shape[-1]
    ch = FUSED_CHUNK
    nchunk = f // ch
    if seq_tiles == 1:
        p = prev_ref[...]
    else:
        p = jnp.where(i % seq_tiles == 0, prev_ref[...], carry_ref[...])
    t = lax.broadcasted_iota(jnp.int32, (nseq, lt, ch), 1)

    def project(c):
        return [jnp.dot(xn, win_ref[:, r * f + c * ch:r * f + (c + 1) * ch], preferred_element_type=F32)
                for r in range(nbr)]

    def gate(c, br):
        cs = slice(c * ch, (c + 1) * ch)
        if mode == "ffn":
            cin, other = br
        else:
            other, cg, xi = br
            cin = cg * xi
        c3 = cin.reshape(nseq, lt, ch)
        p0 = p[:, 0:1, cs]
        p1 = p[:, 1:2, cs]
        c1 = jnp.where(t == 0, p1, pltpu.roll(c3, 1, 1))
        c2 = jnp.where(t == 0, p0, jnp.where(t == 1, p1, pltpu.roll(c3, 2, 1)))
        y = c2 * cw_ref[0:1, cs] + c1 * cw_ref[1:2, cs] + c3 * cw_ref[2:3, cs]
        last2 = c3[:, lt - 2:lt, :]
        st_ref[:, :, cs] = last2
        if seq_tiles > 1:
            carry_ref[:, :, cs] = last2
        o3 = other.reshape(c3.shape)
        out = (y * jax.nn.sigmoid(y)) * o3 if mode == "ffn" else o3 * y
        return out.reshape(nseq * lt, ch).astype(BF16)

    acc = jnp.zeros(x.shape, F32)
    br = project(0)
    for c in range(nchunk):
        br_next = project(c + 1) if c + 1 < nchunk else None
        gch = gate(c, br)
        acc = acc + jnp.dot(gch, wout_ref[c * ch:(c + 1) * ch, :], preferred_element_type=F32)
        br = br_next
    o_ref[...] = x + acc


def _fused_block(x, g, w_in, w_out, cw, prev, mode, seq_len, tm):
    m, d = x.shape
    nbr = 2 if mode == "ffn" else 3
    f = w_in.shape[1] // nbr
    if seq_len >= tm:
        seq_tiles, nseq, lt = seq_len // tm, 1, tm
    else:
        seq_tiles, nseq, lt = 1, tm // seq_len, seq_len
    b = prev.shape[0]
    kern = functools.partial(_fused_kernel, mode=mode, nbr=nbr, seq_tiles=seq_tiles, nseq=nseq, lt=lt)
    return pl.pallas_call(
        kern, name="fused_" + mode,
        grid=(m // tm,),
        in_specs=[pl.BlockSpec((tm, d), lambda i: (i, 0)),
                  pl.BlockSpec((1, d), lambda i: (0, 0)),
                  pl.BlockSpec((d, nbr * f), lambda i: (0, 0), pipeline_mode=pl.Buffered(1)),
                  pl.BlockSpec((f, d), lambda i: (0, 0), pipeline_mode=pl.Buffered(1)),
                  pl.BlockSpec((3, f), lambda i: (0, 0)),
                  pl.BlockSpec((nseq, 2, f), lambda i: (i // seq_tiles, 0, 0))],
        out_specs=[pl.BlockSpec((tm, d), lambda i: (i, 0)),
                   pl.BlockSpec((nseq, 2, f), lambda i: (i // seq_tiles, 0, 0))],
        out_shape=[jax.ShapeDtypeStruct((m, d), F32),
                   jax.ShapeDtypeStruct((b, 2, f), F32)],
        scratch_shapes=[pltpu.VMEM((nseq, 2, f), F32)],
        compiler_params=_cp("arbitrary"),
    )(x, g.reshape(1, d), w_in, w_out, cw, prev)


def _gated_kernel(x_ref, g_ref, *rest, mode, nbr, seq_tiles, nseq, lt):
    w_refs = rest[:nbr]
    cw_ref, prev_ref, o_ref, st_ref, xn_ref, carry_ref = rest[nbr:]
    i = pl.program_id(0)
    j = pl.program_id(1)

    @pl.when(j == 0)
    def _():
        x = x_ref[...]
        ms = jnp.mean(x * x, axis=-1, keepdims=True)
        xn_ref[...] = (x * lax.rsqrt(ms + RMS_EPS) * g_ref[...]).astype(BF16)

    xn = xn_ref[...]
    br = [jnp.dot(xn, w[...], preferred_element_type=F32) for w in w_refs]
    if mode == "ffn":
        cin, other = br
    else:
        other, cg, xi = br
        cin = cg * xi
    tf = cin.shape[-1]
    c = cin.reshape(nseq, lt, tf)
    if seq_tiles == 1:
        p = prev_ref[...]
    else:
        p = jnp.where(i % seq_tiles == 0, prev_ref[...], carry_ref[j])
    p0 = p[:, 0:1, :]
    p1 = p[:, 1:2, :]
    t = lax.broadcasted_iota(jnp.int32, c.shape, 1)
    c1 = jnp.where(t == 0, p1, pltpu.roll(c, 1, 1))
    c2 = jnp.where(t == 0, p0, jnp.where(t == 1, p1, pltpu.roll(c, 2, 1)))
    cw = cw_ref[...]
    y = c2 * cw[0:1, :] + c1 * cw[1:2, :] + c * cw[2:3, :]
    last2 = c[:, lt - 2:lt, :]
    for jj in range(st_ref.shape[-1] // tf):
        @pl.when(j == jj)
        def _(jj=jj):
            st_ref[:, :, jj * tf:(jj + 1) * tf] = last2
    if seq_tiles > 1:
        carry_ref[j] = last2
    o3 = other.reshape(c.shape)
    if mode == "ffn":
        out = (y * jax.nn.sigmoid(y)) * o3
    else:
        out = o3 * y
    o_ref[...] = out.reshape(nseq * lt, tf).astype(BF16)


def _gated(x, g, w, cw, prev, mode, seq_len, tm, tf):
    m, k = x.shape
    nbr = 2 if mode == "ffn" else 3
    f = w.shape[1] // nbr
    nj = f // tf
    if seq_len >= tm:
        seq_tiles, nseq, lt = seq_len // tm, 1, tm
    else:
        seq_tiles, nseq, lt = 1, tm // seq_len, seq_len
    b = prev.shape[0]
    w_specs = [pl.BlockSpec((k, tf), functools.partial(lambda i, j, o: (0, j + o), o=r * nj))
               for r in range(nbr)]
    kern = functools.partial(_gated_kernel, mode=mode, nbr=nbr, seq_tiles=seq_tiles, nseq=nseq, lt=lt)
    return pl.pallas_call(
        kern, name="gated_" + mode,
        grid=(m // tm, nj),
        in_specs=[pl.BlockSpec((tm, k), lambda i, j: (i, 0)),
                  pl.BlockSpec((1, k), lambda i, j: (0, 0))] + w_specs + [
                  pl.BlockSpec((3, tf), lambda i, j: (0, j)),
                  pl.BlockSpec((nseq, 2, tf), lambda i, j: (i // seq_tiles, 0, j))],
        out_specs=[pl.BlockSpec((tm, tf), lambda i, j: (i, j)),
                   pl.BlockSpec((nseq, 2, f), lambda i, j: (i // seq_tiles, 0, 0))],
        out_shape=[jax.ShapeDtypeStruct((m, f), BF16),
                   jax.ShapeDtypeStruct((b, 2, f), F32)],
        scratch_shapes=[pltpu.VMEM((tm, k), BF16),
                        pltpu.VMEM((nj, nseq, 2, tf), F32)],
        compiler_params=_cp("arbitrary", "arbitrary"),
    )(x, g.reshape(1, k), *([w] * nbr), cw, prev)


def _nsa_prep_kernel(z_ref, qg_ref, kg_ref, g512_ref, g128_ref, q_ref, kv_ref):
    z = z_ref[...]
    q = z[:, 0:NSA_WIDTH]
    ms = jnp.dot(q * q, g512_ref[...], precision=HI, preferred_element_type=F32)
    q_ref[...] = (q * lax.rsqrt(ms + RMS_EPS) * qg_ref[...] * (HEAD_DIM ** -0.5)).astype(BF16)
    kv_ref[:, 0:256] = z[:, 512:768]
    for r, off in ((0, 768), (1, 1024)):
        k = z[:, off:off + 128]
        ms = jnp.dot(k * k, g128_ref[...], precision=HI, preferred_element_type=F32)
        kv_ref[:, off - 512:off - 384] = k * lax.rsqrt(ms + RMS_EPS) * kg_ref[r:r + 1, :]
        kv_ref[:, off - 384:off - 256] = z[:, off + 128:off + 256]


def _nsa_prep(z, qg, kg, g512, g128, tm):
    m = z.shape[0]
    const = lambda i: (0, 0)
    return pl.pallas_call(
        _nsa_prep_kernel, name="nsa_prep",
        grid=(m // tm,),
        in_specs=[pl.BlockSpec((tm, NSA_PROJ_PAD), lambda i: (i, 0)),
                  pl.BlockSpec((1, 512), const), pl.BlockSpec((2, 128), const),
                  pl.BlockSpec((512, 512), const), pl.BlockSpec((128, 128), const)],
        out_specs=[pl.BlockSpec((tm, 512), lambda i: (i, 0)),
                   pl.BlockSpec((tm, 768), lambda i: (i, 0))],
        out_shape=[jax.ShapeDtypeStruct((m, 512), BF16),
                   jax.ShapeDtypeStruct((m, 768), F32)],
        compiler_params=_cp("arbitrary"),
    )(z, qg, kg, g512, g128)


def _compress_rows(rows_refs, m, pe_ref, w1_ref, w2_ref, kn_ref, g128_ref):
    outs = []
    for kv, rows_ref in enumerate(rows_refs):
        acc0 = jnp.zeros((m, NSA_KV * CMP_HIDDEN), F32)
        acc1 = jnp.zeros((m, NSA_KV * CMP_HIDDEN), F32)
        for j in range(CMP_STRIDE):
            xj = rows_ref[pl.ds(j, m, stride=CMP_STRIDE), :]
            acc0 = acc0 + jnp.dot((xj + pe_ref[kv, j:j + 1, :]).astype(BF16), w1_ref[kv, j],
                                  preferred_element_type=F32)
            acc1 = acc1 + jnp.dot((xj + pe_ref[kv, CMP_STRIDE + j:CMP_STRIDE + j + 1, :]).astype(BF16),
                                  w1_ref[kv, CMP_STRIDE + j], preferred_element_type=F32)
        hid = acc0 + pltpu.roll(acc1, m - 1, 0)
        outs.append(jnp.dot(jax.nn.gelu(hid).astype(BF16), w2_ref[kv], preferred_element_type=F32))
    kc, vc = outs
    ms = jnp.dot(kc * kc, g128_ref[...], precision=HI, preferred_element_type=F32)
    kc = kc * lax.rsqrt(ms + RMS_EPS) * kn_ref[...]
    return jnp.concatenate([kc, vc], axis=1)


def _compress_kernel(k_ref, v_ref, pe_ref, w1_ref, w2_ref, kn_ref, g128_ref, o_ref, *, m):
    o_ref[0] = _compress_rows((k_ref.at[0], v_ref.at[0]), m, pe_ref, w1_ref, w2_ref, kn_ref, g128_ref)


def _compress(kv, pe, w1, w2, kn, g128):
    b, t, _ = kv.shape
    m = t // CMP_STRIDE
    c2 = lambda i: (0, 0)
    hw = NSA_KV * HEAD_DIM
    return pl.pallas_call(
        functools.partial(_compress_kernel, m=m), name="compress",
        grid=(b,),
        in_specs=[pl.BlockSpec((1, t, hw), lambda i: (i, 0, 0)),
                  pl.BlockSpec((1, t, hw), lambda i: (i, 0, 1)),
                  pl.BlockSpec((2, CMP_LEN, hw), lambda i: (0, 0, 0)),
                  pl.BlockSpec((2, CMP_LEN, hw, NSA_KV * CMP_HIDDEN), lambda i: (0, 0, 0, 0)),
                  pl.BlockSpec((2, NSA_KV * CMP_HIDDEN, hw), lambda i: (0, 0, 0)),
                  pl.BlockSpec((1, 128), c2), pl.BlockSpec((128, 128), c2)],
        out_specs=pl.BlockSpec((1, m, KV_LANES), lambda i: (i, 0, 0)),
        out_shape=jax.ShapeDtypeStruct((b, m, KV_LANES), F32),
        compiler_params=_cp("arbitrary"),
    )(kv, kv, pe, w1, w2, kn, g128)


def _msoftmax(s, mask):
    s = jnp.where(mask, s, NEG)
    mx = jnp.max(s, axis=-1, keepdims=True)
    e = jnp.where(mask, jnp.exp(s - mx), 0.0)
    den = jnp.sum(e, axis=-1, keepdims=True)
    return e / jnp.where(den > 0, den, 1.0)


def _select_blocks(imp, qpos, ns):
    s_io = lax.broadcasted_iota(jnp.int32, imp.shape, 1)
    cur = qpos // SEL_LEN
    forced = (s_io == 0) | (s_io == cur) | (s_io == cur - 1)
    future = s_io * SEL_LEN > qpos
    imp = jnp.where(forced, FORCE_SCORE, imp)
    imp = jnp.where(future, -FORCE_SCORE, imp)
    imp = jnp.where(s_io >= ns, -3e38, imp)
    rank = jnp.zeros(imp.shape, jnp.int32)
    for sp in range(ns):
        col = imp[:, sp:sp + 1]
        beats = (col > imp) | ((col == imp) & (s_io > sp))
        rank = rank + beats.astype(jnp.int32)
    sel = (rank < min(N_SELECT, ns)) & (s_io < ns)
    return sel.astype(F32)


def _heads_to_rows(q, g):
    return jnp.concatenate(
        [q[:, HEAD_DIM * (NSA_QPG * g + h):HEAD_DIM * (NSA_QPG * g + h + 1)] for h in range(NSA_QPG)],
        axis=0)


def _msoftmax0(s, mask):
    s = jnp.where(mask, s, NEG)
    mx = jnp.max(s, axis=0, keepdims=True)
    e = jnp.where(mask, jnp.exp(s - mx), 0.0)
    den = jnp.sum(e, axis=0, keepdims=True)
    return e / jnp.where(den > 0, den, 1.0)


def _select_blocks_t(imp, qpos, ns):
    s_io = lax.broadcasted_iota(jnp.int32, imp.shape, 0)
    cur = qpos // SEL_LEN
    forced = (s_io == 0) | (s_io == cur) | (s_io == cur - 1)
    future = s_io * SEL_LEN > qpos
    imp = jnp.where(forced, FORCE_SCORE, imp)
    imp = jnp.where(future, -FORCE_SCORE, imp)
    imp = jnp.where(s_io >= ns, -3e38, imp)
    rank = jnp.zeros(imp.shape, jnp.int32)
    for sp in range(ns):
        row = imp[sp:sp + 1, :]
        beats = (row > imp) | ((row == imp) & (s_io > sp))
        rank = rank + beats.astype(jnp.int32)
    sel = (rank < min(N_SELECT, ns)) & (s_io < ns)
    return sel.astype(F32)


def _flash_steps_t(chains):
    sts = [_nt(k, qg) for (qg, k, _, _, _, _) in chains]
    mids = []
    for st, (_, _, _, bias, mask, (m_, l_, _)) in zip(sts, chains):
        ps, ms, ls, als = [], [], [], []
        for h in range(NSA_QPG):
            hs = slice(KT * h, KT * (h + 1))
            s = st[:, hs]
            if bias is not None:
                s = s + bias[:, hs]
            s = jnp.where(mask, s, NEG)
            m_new = jnp.maximum(m_[:, hs], jnp.max(s, axis=0, keepdims=True))
            alpha = jnp.exp(m_[:, hs] - m_new)
            e = jnp.where(mask, jnp.exp(s - m_new), 0.0)
            ls.append(alpha * l_[:, hs] + jnp.sum(e, axis=0, keepdims=True))
            ms.append(m_new)
            als.append(alpha)
            ps.append(e.astype(BF16))
        mids.append((jnp.concatenate(ps, axis=1), jnp.concatenate(ms, axis=1),
                     jnp.concatenate(ls, axis=1), jnp.concatenate(als, axis=1)))
    out = []
    for (p, m_new, l_new, alpha), (_, _, v, _, _, (_, _, acc)) in zip(mids, chains):
        out.append((m_new, l_new, alpha * acc + _tn(v, p)))
    return out


def _flash_init_t():
    return (jnp.full((1, NSA_QPG * KT), NEG, F32), jnp.zeros((1, NSA_QPG * KT), F32),
            jnp.zeros((HEAD_DIM, NSA_QPG * KT), F32))


def _flash_out_t(carry):
    _, l_, acc = carry
    return acc / jnp.where(l_ > 0, l_, 1.0)


def _nsa_prompt_kernel(q_ref, gt_ref, cc_ref, kv_ref, bc_ref, tb_ref, ovt_ref, ext_ref, o_ref, selk_s, sc_s,
                       *, m, ns):
    i = pl.program_id(1)
    q = q_ref[0]
    gst = jax.nn.sigmoid(gt_ref[...]).T
    k_io = lax.broadcasted_iota(jnp.int32, (KT, KT), 0)
    q_io = lax.broadcasted_iota(jnp.int32, (KT, KT), 1)
    qpos = i * KT + lax.broadcasted_iota(jnp.int32, (1, KT), 1)
    n_io = lax.broadcasted_iota(jnp.int32, (m, KT), 0)
    mask_c = ((n_io * CMP_STRIDE + (CMP_LEN - 1)) <= qpos) & (n_io < m - 1)
    cc = cc_ref[0]
    groups = range(NSA_KV)
    qgs = [_heads_to_rows(q, g) for g in groups]

    kcs = [cc[:, HEAD_DIM * g:HEAD_DIM * (g + 1)].astype(BF16) for g in groups]
    vcs = [cc[:, 128 + HEAD_DIM * g:128 + HEAD_DIM * (g + 1)].astype(BF16) for g in groups]
    c_off = pl.multiple_of(jnp.maximum(i - 1, 0) * (KT // CMP_STRIDE), KT // CMP_STRIDE)
    sts = []
    for g in groups:
        sc_s[g] = _nt(kcs[g], qgs[g])
        band = jnp.where(i == 0, bc_ref[g, 1], bc_ref[g, 0])
        sc_s[g, pl.ds(c_off, 2 * KT // CMP_STRIDE), :] += band
        sts.append(sc_s[g])
    pcs = [[_msoftmax0(sts[g][:, KT * h:KT * (h + 1)], mask_c) for h in range(NSA_QPG)]
           for g in groups]
    o_cs = [_tn(vcs[g], jnp.concatenate(pcs[g], axis=1).astype(BF16)) for g in groups]
    imps = [jnp.dot(ovt_ref[...], pcs[g][0] + pcs[g][1] + pcs[g][2] + pcs[g][3], precision=HI,
                    preferred_element_type=F32) for g in groups]
    sels = [_select_blocks_t(imps[g], qpos, ns).astype(BF16) for g in groups]
    for g in groups:
        selk_s[g] = jnp.dot(ext_ref[...], sels[g], preferred_element_type=F32)

    def load_kv(g, kt, off):
        r0 = pl.multiple_of(kt * KT, KT)
        k_lo = 256 + off + HEAD_DIM * g
        v_lo = 384 + off + HEAD_DIM * g
        k = kv_ref[0, pl.ds(r0, KT), k_lo:k_lo + HEAD_DIM].astype(BF16)
        v = kv_ref[0, pl.ds(r0, KT), v_lo:v_lo + HEAD_DIM].astype(BF16)
        return k, v

    def sel_mask(g, kt):
        r0 = pl.multiple_of(kt * KT, KT)
        return selk_s[g, pl.ds(r0, KT), :] > 0.5

    def body(kt, carry):
        chains = []
        for g in groups:
            k, v = load_kv(g, kt, 0)
            chains.append((qgs[g], k, v, None, sel_mask(g, kt), carry[g]))
        return tuple(_flash_steps_t(chains))

    c_sel = list(lax.fori_loop(0, jnp.maximum(i - 1, 0), body, tuple(_flash_init_t() for _ in groups)))

    nwt = WINDOW // KT
    c_win = [_flash_init_t() for _ in groups]
    for d in range(nwt, -1, -1):
        kt = i - d
        ktc = jnp.maximum(kt, 0)
        if d == nwt:
            mk = (k_io > q_io) & (kt >= 0)
        elif d == 0:
            mk = k_io <= q_io
        else:
            mk = kt >= 0
        chains = []
        for g in groups:
            k, v = load_kv(g, ktc, 256)
            bias = tb_ref[g, 0] if d == 1 else (tb_ref[g, 1] if d == 0 else None)
            chains.append((qgs[g], k, v, bias, mk, c_win[g]))
        if d <= 1:
            for g in groups:
                k, v = load_kv(g, ktc, 0)
                smk = sel_mask(g, ktc) & ((kt >= 0) if d == 1 else (k_io <= q_io))
                chains.append((qgs[g], k, v, tb_ref[g, 1 - d], smk, c_sel[g]))
        res = _flash_steps_t(chains)
        c_win = res[:NSA_KV]
        if d <= 1:
            c_sel = res[NSA_KV:]
    o_ss = [_flash_out_t(c_sel[g]) for g in groups]
    o_ws = [_flash_out_t(c_win[g]) for g in groups]

    for g in groups:
        for h in range(NSA_QPG):
            hh = NSA_QPG * g + h
            hs = slice(KT * h, KT * (h + 1))
            o = (gst[3 * hh:3 * hh + 1, :] * o_cs[g][:, hs] + gst[3 * hh + 1:3 * hh + 2, :] * o_ss[g][:, hs]
                 + gst[3 * hh + 2:3 * hh + 3, :] * o_ws[g][:, hs])
            o_ref[0, HEAD_DIM * hh:HEAD_DIM * (hh + 1), :] = o.astype(BF16)


def _nsa_prompt(qn, z_nsa, cc, kv, bct, tbt, ovt, ext, b, t_len):
    m = cc.shape[1]
    ns = -(-t_len // SEL_LEN)
    nq = t_len // KT
    nsp = ovt.shape[0]
    c2 = lambda bi, i: (0, 0)
    return pl.pallas_call(
        functools.partial(_nsa_prompt_kernel, m=m, ns=ns), name="nsa_prompt",
        grid=(b, nq),
        in_specs=[pl.BlockSpec((1, KT, 512), lambda bi, i: (bi, i, 0)),
                  pl.BlockSpec((KT, LANE), lambda bi, i: (bi * nq + i, GATE_COL_BLOCK)),
                  pl.BlockSpec((1, m, KV_LANES), lambda bi, i: (bi, 0, 0)),
                  pl.BlockSpec((1, t_len, 768), lambda bi, i: (bi, 0, 0)),
                  pl.BlockSpec((NSA_KV, 2, 2 * KT // CMP_STRIDE, NSA_QPG * KT), lambda bi, i: (0, 0, 0, 0)),
                  pl.BlockSpec((NSA_KV, 2, KT, NSA_QPG * KT), lambda bi, i: (0, 0, 0, 0)),
                  pl.BlockSpec((nsp, m), c2),
                  pl.BlockSpec((t_len, nsp), c2)],
        out_specs=pl.BlockSpec((1, 512, KT), lambda bi, i: (bi, 0, i)),
        out_shape=jax.ShapeDtypeStruct((b, 512, t_len), BF16),
        scratch_shapes=[pltpu.VMEM((NSA_KV, t_len, KT), F32),
                        pltpu.VMEM((NSA_KV, m, NSA_QPG * KT), F32)],
        compiler_params=_cp("arbitrary", "arbitrary"),
    )(qn.reshape(b, t_len, 512), z_nsa, cc, kv.reshape(b, t_len, 768), bct, tbt, ovt, ext)


def _nsa_sample_kernel(pt_ref, cmp_pg, slc_pg, new_ref, win_ref, q_ref, gt_ref, pe_ref, w1_ref, w2_ref,
                       kn_ref, g128_ref, bc_ref, bs_ref, bw_ref, ov_ref, ex_ref, o_ref,
                       cmpk_s, cmpv_s, slc_s, win_s, *, npages, past, wb, tq, m, ns):
    p = pl.program_id(1)
    r0 = pl.multiple_of(p * PAGE, PAGE)
    cmpk_s[pl.ds(r0, PAGE), :] = cmp_pg[0, 0, :, 0:128]
    cmpv_s[pl.ds(r0, PAGE), :] = cmp_pg[0, 0, :, 128:256]
    slc_s[pl.ds(r0, PAGE), :] = slc_pg[0, 0]

    @pl.when(p == npages - 1)
    def _():
        kp = slc_s.shape[0]
        wp = win_s.shape[0]
        new = new_ref[0]
        pad = jnp.zeros((kp - past - tq, KV_LANES), F32)
        cmpk_s[past:past + tq, :] = new[:, 0:128]
        cmpk_s[past + tq:kp, :] = pad[:, 0:128]
        cmpv_s[past:past + tq, :] = new[:, 128:256]
        cmpv_s[past + tq:kp, :] = pad[:, 0:128]
        slc_s[past:past + tq, :] = new[:, 256:512]
        slc_s[past + tq:kp, :] = pad
        win_s[0:wb, :] = win_ref[0, 0]
        win_s[wb:wb + tq, :] = new[:, 512:768]
        win_s[wb + tq:wp, :] = jnp.zeros((wp - wb - tq, KV_LANES), F32)

        cc = _compress_rows((cmpk_s, cmpv_s), m, pe_ref, w1_ref, w2_ref, kn_ref, g128_ref)
        q = q_ref[0]
        gs = jax.nn.sigmoid(gt_ref[0])
        qpos = past + lax.broadcasted_iota(jnp.int32, (tq, 1), 0)
        n_io = lax.broadcasted_iota(jnp.int32, (tq, m), 1)
        mask_c = ((n_io * CMP_STRIDE + (CMP_LEN - 1)) <= qpos) & (n_io < m - 1)
        k_io = lax.broadcasted_iota(jnp.int32, (tq, kp), 1)
        mask_s0 = (k_io <= qpos) & (k_io < past + tq)
        j_io = lax.broadcasted_iota(jnp.int32, (tq, wp), 1)
        dist_w = qpos - (past - wb + j_io)
        mask_w = (dist_w >= 0) & (dist_w < WINDOW) & (j_io < wb + tq)
        outs = []
        for g in range(NSA_KV):
            qg = _heads_to_rows(q, g)
            kc = cc[:, HEAD_DIM * g:HEAD_DIM * (g + 1)].astype(BF16)
            vc = cc[:, 128 + HEAD_DIM * g:128 + HEAD_DIM * (g + 1)].astype(BF16)
            s = _nt(qg, kc).reshape(NSA_QPG, tq, m) + bc_ref[NSA_QPG * g:NSA_QPG * (g + 1)]
            p_c = _msoftmax(s, mask_c[None])
            o_c = jnp.dot(p_c.reshape(NSA_QPG * tq, m).astype(BF16), vc,
                          preferred_element_type=F32).reshape(NSA_QPG, tq, HEAD_DIM)
            psum = p_c[0] + p_c[1] + p_c[2] + p_c[3]
            imp = jnp.dot(psum, ov_ref[...], precision=HI, preferred_element_type=F32)
            sel = _select_blocks(imp, qpos, ns)
            selk = jnp.dot(sel.astype(BF16), ex_ref[...], preferred_element_type=F32)

            ks = slc_s[:, HEAD_DIM * g:HEAD_DIM * (g + 1)].astype(BF16)
            vs = slc_s[:, 128 + HEAD_DIM * g:128 + HEAD_DIM * (g + 1)].astype(BF16)
            s = _nt(qg, ks).reshape(NSA_QPG, tq, kp) + bs_ref[NSA_QPG * g:NSA_QPG * (g + 1)]
            p_s = _msoftmax(s, ((selk > 0.5) & mask_s0)[None])
            o_s = jnp.dot(p_s.reshape(NSA_QPG * tq, kp).astype(BF16), vs,
                          preferred_element_type=F32).reshape(NSA_QPG, tq, HEAD_DIM)

            kw = win_s[:, HEAD_DIM * g:HEAD_DIM * (g + 1)].astype(BF16)
            vw = win_s[:, 128 + HEAD_DIM * g:128 + HEAD_DIM * (g + 1)].astype(BF16)
            s = _nt(qg, kw).reshape(NSA_QPG, tq, wp) + bw_ref[NSA_QPG * g:NSA_QPG * (g + 1)]
            p_w = _msoftmax(s, mask_w[None])
            o_w = jnp.dot(p_w.reshape(NSA_QPG * tq, wp).astype(BF16), vw,
                          preferred_element_type=F32).reshape(NSA_QPG, tq, HEAD_DIM)
            for h in range(NSA_QPG):
                hh = NSA_QPG * g + h
                o = (gs[:, 3 * hh:3 * hh + 1] * o_c[h] + gs[:, 3 * hh + 1:3 * hh + 2] * o_s[h]
                     + gs[:, 3 * hh + 2:3 * hh + 3] * o_w[h])
                outs.append(o)
        o_ref[0] = jnp.concatenate(outs, axis=1).astype(BF16)


def _nsa_sample(pt, pool_cmp, pool_slc, li, kv_new, win, qn, z_nsa, cw, bc, bs, bw, ov, ex, b, tq):
    npages = pt.shape[1]
    past = npages * PAGE
    wb = win.shape[2]
    kp = _round_up(past + tq, LANE)
    wp = _round_up(wb + tq, LANE)
    m = (past + tq) // CMP_STRIDE
    ns = -(-(past + tq) // SEL_LEN)
    pe, w1, w2, kn, g128 = cw
    c2 = lambda bi, p, pt_: (0, 0)
    c3 = lambda bi, p, pt_: (0, 0, 0)
    per_b3 = lambda bi, p, pt_: (bi, 0, 0)
    grid_spec = pltpu.PrefetchScalarGridSpec(
        num_scalar_prefetch=1,
        grid=(b, npages),
        in_specs=[pl.BlockSpec((1, 1, PAGE, KV_LANES), lambda bi, p, pt_: (pt_[bi, p], li, 0, 0)),
                  pl.BlockSpec((1, 1, PAGE, KV_LANES), lambda bi, p, pt_: (pt_[bi, p], li, 0, 0)),
                  pl.BlockSpec((1, tq, 768), per_b3),
                  pl.BlockSpec((1, 1, wb, KV_LANES), lambda bi, p, pt_: (bi, li, 0, 0)),
                  pl.BlockSpec((1, tq, 512), per_b3),
                  pl.BlockSpec((1, tq, LANE), lambda bi, p, pt_: (bi, 0, GATE_COL_BLOCK)),
                  pl.BlockSpec((2, CMP_LEN, LANE), c3),
                  pl.BlockSpec((2, CMP_LEN, LANE, NSA_KV * CMP_HIDDEN), lambda bi, p, pt_: (0, 0, 0, 0)),
                  pl.BlockSpec((2, NSA_KV * CMP_HIDDEN, LANE), c3),
                  pl.BlockSpec((1, 128), c2), pl.BlockSpec((128, 128), c2),
                  pl.BlockSpec((NSA_HEADS, tq, m), c3),
                  pl.BlockSpec((NSA_HEADS, tq, kp), c3),
                  pl.BlockSpec((NSA_HEADS, tq, wp), c3),
                  pl.BlockSpec((m, LANE), c2),
                  pl.BlockSpec((LANE, kp), c2)],
        out_specs=pl.BlockSpec((1, tq, 512), per_b3),
        scratch_shapes=[pltpu.VMEM((kp, LANE), F32), pltpu.VMEM((kp, LANE), F32),
                        pltpu.VMEM((kp, KV_LANES), F32), pltpu.VMEM((wp, KV_LANES), F32)])
    kern = functools.partial(_nsa_sample_kernel, npages=npages, past=past, wb=wb, tq=tq, m=m, ns=ns)
    return pl.pallas_call(
        kern, grid_spec=grid_spec, name="nsa_sample",
        out_shape=jax.ShapeDtypeStruct((b, tq, 512), BF16),
        compiler_params=_cp("arbitrary", "arbitrary"),
    )(pt, pool_cmp, pool_slc, kv_new.reshape(b, tq, 768), win, qn.reshape(b, tq, 512),
      z_nsa.reshape(b, tq, NSA_PROJ_PAD), pe, w1, w2, kn, g128, bc, bs, bw, ov, ex)


def _nsa_paged_kernel(pt_ref, cmp_hbm, slc_hbm, new_ref, win_ref, q_ref, gt_ref, pe_ref, w1_ref, w2_ref,
                      kn_ref, g128_ref, bc_ref, bs_ref, bw_ref, ov_ref, ex_ref, o_ref,
                      cmpt_s, slct_s, cmpk_s, cmpv_s, sems, *, li, nb, npages, past, wb, tq, m, ns):
    b = pl.program_id(0)
    slot = b % 2

    def page_copy(bb, p, sl, which):
        hbm, buf = ((cmp_hbm, cmpt_s), (slc_hbm, slct_s))[which]
        return pltpu.make_async_copy(hbm.at[pt_ref[bb, p], li], buf.at[sl, :, pl.ds(p * PAGE, PAGE)],
                                     sems.at[sl, which])

    def start_batch(bb, sl):
        for p in range(npages):
            page_copy(bb, p, sl, 0).start()
            page_copy(bb, p, sl, 1).start()

    @pl.when(b == 0)
    def _():
        start_batch(0, 0)

    @pl.when(b + 1 < nb)
    def _():
        start_batch(b + 1, 1 - slot)

    for p in range(npages):
        page_copy(b, p, slot, 0).wait()
        page_copy(b, p, slot, 1).wait()

    hw = NSA_KV * HEAD_DIM
    for p in range(npages):
        blk = cmpt_s[slot, :, p * PAGE:(p + 1) * PAGE]
        cmpk_s[p * PAGE:(p + 1) * PAGE, :] = blk[0:hw].T
        cmpv_s[p * PAGE:(p + 1) * PAGE, :] = blk[hw:2 * hw].T
    cc = _compress_rows((cmpk_s, cmpv_s), m, pe_ref, w1_ref, w2_ref, kn_ref, g128_ref)

    new = new_ref[0]
    newt = jnp.concatenate([new, jnp.zeros((LANE - tq, 768), F32)], axis=0).T.astype(BF16)
    q = q_ref[0]
    gs = jax.nn.sigmoid(gt_ref[0])
    kp = past + LANE
    wp = wb + LANE
    qpos = past + lax.broadcasted_iota(jnp.int32, (tq, 1), 0)
    n_io = lax.broadcasted_iota(jnp.int32, (tq, m), 1)
    mask_c = ((n_io * CMP_STRIDE + (CMP_LEN - 1)) <= qpos) & (n_io < m - 1)
    k_io = lax.broadcasted_iota(jnp.int32, (tq, kp), 1)
    mask_s0 = (k_io <= qpos) & (k_io < past + tq)
    j_io = lax.broadcasted_iota(jnp.int32, (tq, wp), 1)
    dist_w = qpos - (past - wb + j_io)
    mask_w = (dist_w >= 0) & (dist_w < WINDOW) & (j_io < wb + tq)
    rep = lambda a_: jnp.concatenate([a_] * NSA_QPG, axis=0)
    mask_c4, mask_w4 = rep(mask_c), rep(mask_w)
    groups = range(NSA_KV)
    gsl = lambda g, off: slice(off + HEAD_DIM * g, off + HEAD_DIM * (g + 1))
    qgs = [_heads_to_rows(q, g) for g in groups]

    s_c = [_nt(qgs[g], cc[:, gsl(g, 0)].astype(BF16)) + bc_ref[g] for g in groups]
    p_c = [_msoftmax(s_c[g], mask_c4) for g in groups]
    o_c = [jnp.dot(p_c[g].astype(BF16), cc[:, gsl(g, hw)].astype(BF16), preferred_element_type=F32)
           for g in groups]
    imps = [jnp.dot(p_c[g][0:tq] + p_c[g][tq:2 * tq] + p_c[g][2 * tq:3 * tq] + p_c[g][3 * tq:4 * tq],
                    ov_ref[...], precision=HI, preferred_element_type=F32) for g in groups]
    sels = [_select_blocks(imps[g], qpos, ns).astype(BF16) for g in groups]
    selk = [jnp.dot(sels[g], ex_ref[...], preferred_element_type=F32) for g in groups]

    s_s = [jnp.concatenate(
        [jnp.dot(qgs[g], slct_s[slot, gsl(g, 0), :].astype(BF16), preferred_element_type=F32),
         jnp.dot(qgs[g], newt[gsl(g, 256)], preferred_element_type=F32)], axis=1) + bs_ref[g]
        for g in groups]
    p_s = [_msoftmax(s_s[g], rep((selk[g] > 0.5) & mask_s0)) for g in groups]
    o_s = [_nt(p_s[g][:, 0:past].astype(BF16), slct_s[slot, gsl(g, hw), :].astype(BF16))
           + _nt(p_s[g][:, past:kp].astype(BF16), newt[gsl(g, 256 + hw)]) for g in groups]

    s_w = [jnp.concatenate(
        [jnp.dot(qgs[g], win_ref[0, 0, gsl(g, 0), :].astype(BF16), preferred_element_type=F32),
         jnp.dot(qgs[g], newt[gsl(g, 512)], preferred_element_type=F32)], axis=1) + bw_ref[g]
        for g in groups]
    p_w = [_msoftmax(s_w[g], mask_w4) for g in groups]
    o_w = [_nt(p_w[g][:, 0:wb].astype(BF16), win_ref[0, 0, gsl(g, hw), :].astype(BF16))
           + _nt(p_w[g][:, wb:wp].astype(BF16), newt[gsl(g, 512 + hw)]) for g in groups]

    outs = []
    for g in groups:
        for h in range(NSA_QPG):
            hh = NSA_QPG * g + h
            rs = slice(tq * h, tq * (h + 1))
            outs.append(gs[:, 3 * hh:3 * hh + 1] * o_c[g][rs] + gs[:, 3 * hh + 1:3 * hh + 2] * o_s[g][rs]
                        + gs[:, 3 * hh + 2:3 * hh + 3] * o_w[g][rs])
    o_ref[0] = jnp.concatenate(outs, axis=1).astype(BF16)


def _nsa_paged(pt, pool_cmp_t, pool_slc_t, li, kv_new, win_t, qn, z_nsa, cw, bc, bs, bw, ov, ex, b, tq):
    npages = pt.shape[1]
    past = npages * PAGE
    wb = win_t.shape[3]
    assert past % CMP_STRIDE == 0 and tq < CMP_STRIDE
    kp = past + LANE
    wp = wb + LANE
    m = (past + tq) // CMP_STRIDE
    ns = -(-(past + tq) // SEL_LEN)
    rows = NSA_QPG * tq
    pe, w1, w2, kn, g128 = cw
    hw = NSA_KV * HEAD_DIM
    c2 = lambda bi, pt_: (0, 0)
    c3 = lambda bi, pt_: (0, 0, 0)
    per_b3 = lambda bi, pt_: (bi, 0, 0)
    grid_spec = pltpu.PrefetchScalarGridSpec(
        num_scalar_prefetch=1,
        grid=(b,),
        in_specs=[pl.BlockSpec(memory_space=pl.ANY),
                  pl.BlockSpec(memory_space=pl.ANY),
                  pl.BlockSpec((1, tq, 768), per_b3),
                  pl.BlockSpec((1, 1, KV_LANES, wb), lambda bi, pt_: (bi, li, 0, 0)),
                  pl.BlockSpec((1, tq, 512), per_b3),
                  pl.BlockSpec((1, tq, LANE), lambda bi, pt_: (bi, 0, GATE_COL_BLOCK)),
                  pl.BlockSpec((2, CMP_LEN, hw), c3),
                  pl.BlockSpec((2, CMP_LEN, hw, NSA_KV * CMP_HIDDEN), lambda bi, pt_: (0, 0, 0, 0)),
                  pl.BlockSpec((2, NSA_KV * CMP_HIDDEN, hw), c3),
                  pl.BlockSpec((1, 128), c2), pl.BlockSpec((128, 128), c2),
                  pl.BlockSpec((NSA_KV, rows, m), c3),
                  pl.BlockSpec((NSA_KV, rows, kp), c3),
                  pl.BlockSpec((NSA_KV, rows, wp), c3),
                  pl.BlockSpec((m, LANE), c2),
                  pl.BlockSpec((LANE, kp), c2)],
        out_specs=pl.BlockSpec((1, tq, 512), per_b3),
        scratch_shapes=[pltpu.VMEM((2, KV_LANES, past), F32), pltpu.VMEM((2, KV_LANES, past), F32),
                        pltpu.VMEM((past, hw), F32), pltpu.VMEM((past, hw), F32),
                        pltpu.SemaphoreType.DMA((2, 2))])
    kern = functools.partial(_nsa_paged_kernel, li=li, nb=b, npages=npages, past=past, wb=wb, tq=tq,
                             m=m, ns=ns)
    return pl.pallas_call(
        kern, grid_spec=grid_spec, name="nsa_paged",
        out_shape=jax.ShapeDtypeStruct((b, tq, 512), BF16),
        compiler_params=_cp("arbitrary"),
    )(pt, pool_cmp_t, pool_slc_t, kv_new.reshape(b, tq, 768), win_t, qn.reshape(b, tq, 512),
      z_nsa.reshape(b, tq, NSA_PROJ_PAD), pe, w1, w2, kn, g128, bc, bs, bw, ov, ex)


def _rwkv_kernel(z_ref, zp_ref, s0_ref, mu_ref, w0_ref, w2_ref, a0_ref, a2_ref, g2_ref, kkp_ref, ka_ref,
                 rk_ref, lnw_ref, lnb_ref, y_ref, st_ref, s_scr, carry_scr, *, c_len):
    c = pl.program_id(1)
    cl = c_len

    @pl.when(c == 0)
    def _():
        s_scr[...] = s0_ref[0]
        carry_scr[...] = zp_ref[0]

    z = z_ref[0]
    row = lax.broadcasted_iota(jnp.int32, z.shape, 0)
    shifted = jnp.where(row == 0, carry_scr[...], pltpu.roll(z, 1, 0))
    carry_scr[...] = z[cl - 1:cl, :]
    zz = z + (shifted - z) * mu_ref[...]
    w = RWKV_WIDTH
    r = zz[:, 0:w]
    k = zz[:, w:2 * w]
    v = zz[:, 2 * w:3 * w]
    zw = zz[:, 3 * w:3 * w + DECAY_LORA]
    za = zz[:, 3 * w + DECAY_LORA:3 * w + DECAY_LORA + AAA_LORA]
    zg = zz[:, 3 * w + DECAY_LORA + AAA_LORA:]
    xw = -(w0_ref[...] + jnp.dot(jnp.tanh(zw), w2_ref[...], precision=HI, preferred_element_type=F32))
    softplus = jnp.maximum(xw, 0.0) + jnp.log(1.0 + jnp.exp(-jnp.abs(xw)))
    logdec = -jnp.exp(-softplus - 0.5)
    a = jax.nn.sigmoid(a0_ref[...] + jnp.dot(za, a2_ref[...], precision=HI, preferred_element_type=F32))
    gate = jnp.dot(jax.nn.sigmoid(zg), g2_ref[...], precision=HI, preferred_element_type=F32)
    k_mod = k * (1.0 + (a - 1.0) * ka_ref[...])
    kku = k * kkp_ref[...]

    ti = lax.broadcasted_iota(jnp.int32, (cl, cl), 0)
    si = lax.broadcasted_iota(jnp.int32, (cl, cl), 1)
    lower = si <= ti
    strict = si < ti
    gcum = jnp.dot(lower.astype(F32), logdec, precision=HI, preferred_element_type=F32)
    glast = gcum[cl - 1:cl, :]
    e_g = jnp.exp(gcum)
    e_gm = jnp.exp(gcum - logdec)
    e_ng = jnp.exp(-gcum)
    e_lg = jnp.exp(glast - gcum)
    e_l = jnp.exp(glast)
    nsteps = max(1, int(math.ceil(math.log2(cl))))
    row2 = lax.broadcasted_iota(jnp.int32, (cl, 2 * cl), 0)
    col2 = lax.broadcasted_iota(jnp.int32, (cl, 2 * cl), 1)
    col2 = jnp.where(col2 >= cl, col2 - cl, col2)
    strict2 = col2 < row2
    lower2 = col2 <= row2
    keep_z = lax.broadcasted_iota(jnp.int32, (cl, cl + HEAD_DIM), 1) >= cl

    heads = range(RWKV_HEADS)
    sls = [slice(HEAD_DIM * h, HEAD_DIM * (h + 1)) for h in heads]
    kks = [kku[:, sl] for sl in sls]
    kks = [kk * lax.rsqrt(jnp.maximum(jnp.sum(kk * kk, axis=-1, keepdims=True), 1e-24)) for kk in kks]
    kkas = [kks[h] * a[:, sls[h]] for h in heads]
    kms = [k_mod[:, sl] for sl in sls]
    vhs = [v[:, sl] for sl in sls]
    rhs_ = [r[:, sl] for sl in sls]
    s_hs = [s_scr[h] for h in heads]
    lrs = [jnp.concatenate([-kks[h] * e_gm[:, sls[h]], rhs_[h] * e_g[:, sls[h]]], axis=0) for h in heads]
    rrs = [jnp.concatenate([kkas[h] * e_ng[:, sls[h]], kms[h] * e_ng[:, sls[h]]], axis=0) for h in heads]
    m1s = [_mmp(lrs[h], rrs[h], _NT) for h in heads]
    m2s = [_mmp(lrs[h], s_hs[h], _NT) for h in heads]
    tops = [jnp.where(strict2, m1[0:cl], 0.0) for m1 in m1s]
    bots = [jnp.where(lower2, m1[cl:2 * cl], 0.0) for m1 in m1s]
    zeros_v = jnp.zeros((cl, HEAD_DIM), F32)
    rhss = [m2s[h][0:cl] + _mmp(tops[h], jnp.concatenate([zeros_v, vhs[h]], axis=0), _NN) for h in heads]
    wms = [jnp.concatenate([tops[h][:, 0:cl], rhss[h]], axis=1) for h in heads]
    for _ in range(nsteps):
        wms = [_mmp(w_[:, 0:cl], w_, _NN) + jnp.where(keep_z, w_, 0.0) for w_ in wms]
    zvs = [jnp.concatenate([wms[h][:, cl:cl + HEAD_DIM], vhs[h]], axis=0) for h in heads]
    ys = [m2s[h][cl:2 * cl] + _mmp(bots[h], zvs[h], _NN) for h in heads]
    bkhs = [jnp.concatenate([kkas[h] * e_lg[:, sls[h]], kms[h] * e_lg[:, sls[h]]], axis=0) for h in heads]
    s_new = [s_hs[h] * e_l[:, sls[h]] + _mmp(zvs[h], bkhs[h], _TN) for h in heads]
    for h in heads:
        s_scr[h] = s_new[h]
    outs = []
    for h in heads:
        y = ys[h]
        sl = sls[h]
        mean = jnp.mean(y, axis=-1, keepdims=True)
        yc = y - mean
        var = jnp.mean(yc * yc, axis=-1, keepdims=True)
        yn = yc * lax.rsqrt(var + GN_EPS) * lnw_ref[:, sl] + lnb_ref[:, sl]
        bonus = jnp.sum(rhs_[h] * kms[h] * rk_ref[:, sl], axis=-1, keepdims=True) * vhs[h]
        outs.append(yn + bonus)
    y_ref[0] = (jnp.concatenate(outs, axis=1) * gate).astype(BF16)
    st_ref[0] = s_scr[...]


def _rwkv(z_rw, z_prev, s0, pr, b, t_len, c_len):
    nchunk = t_len // c_len
    c2 = lambda bi, c: (0, 0)
    vec = lambda n: pl.BlockSpec((1, n), c2)
    return pl.pallas_call(
        functools.partial(_rwkv_kernel, c_len=c_len), name="rwkv",
        grid=(b, nchunk),
        in_specs=[pl.BlockSpec((1, c_len, RWKV_PROJ), lambda bi, c: (bi, c, 0)),
                  pl.BlockSpec((1, 1, RWKV_PROJ), lambda bi, c: (bi, 0, 0)),
                  pl.BlockSpec((1, RWKV_HEADS, HEAD_DIM, HEAD_DIM), lambda bi, c: (bi, 0, 0, 0)),
                  vec(RWKV_PROJ), vec(512), pl.BlockSpec((DECAY_LORA, 512), c2),
                  vec(512), pl.BlockSpec((AAA_LORA, 512), c2), pl.BlockSpec((GATE_LORA, 512), c2),
                  vec(512), vec(512), vec(512), vec(512), vec(512)],
        out_specs=[pl.BlockSpec((1, c_len, 512), lambda bi, c: (bi, c, 0)),
                   pl.BlockSpec((1, RWKV_HEADS, HEAD_DIM, HEAD_DIM), lambda bi, c: (bi, 0, 0, 0))],
        out_shape=[jax.ShapeDtypeStruct((b, t_len, 512), BF16),
                   jax.ShapeDtypeStruct((b, RWKV_HEADS, HEAD_DIM, HEAD_DIM), F32)],
        scratch_shapes=[pltpu.VMEM((RWKV_HEADS, HEAD_DIM, HEAD_DIM), F32),
                        pltpu.VMEM((1, RWKV_PROJ), F32)],
        compiler_params=_cp("arbitrary", "arbitrary"),
    )(z_rw.reshape(b, t_len, RWKV_PROJ), z_prev.reshape(b, 1, RWKV_PROJ), s0, *pr)


def _rel_bucket(dist):
    n = jnp.maximum(dist, 0)
    max_exact = NUM_BUCKETS // 2
    nf = jnp.maximum(n, 1).astype(F32)
    large = max_exact + (jnp.log(nf / max_exact) / math.log(MAX_DISTANCE / max_exact)
                         * (NUM_BUCKETS - max_exact)).astype(jnp.int32)
    large = jnp.minimum(large, NUM_BUCKETS - 1)
    return jnp.where(n < max_exact, n, large)


def _bias_table(rel_bias, dist, delta):
    tb = rel_bias.astype(F32)
    out = jnp.moveaxis(tb[_rel_bucket(dist)], -1, 0)
    if delta:
        out = out - tb[NUM_BUCKETS - 1].reshape((NSA_HEADS,) + (1,) * dist.ndim)
    return out


def _block_diag_ones(n, grp):
    idx = jnp.arange(n) // grp
    return (idx[:, None] == idx[None, :]).astype(F32) / grp


def _compress_weights(phi_pe, phi_w1, phi_w2, kn_cmp):
    pe = jnp.concatenate([phi_pe, phi_pe], axis=-1)
    w1 = jnp.zeros((2, CMP_LEN, NSA_KV * HEAD_DIM, NSA_KV * CMP_HIDDEN), F32)
    w2 = jnp.zeros((2, NSA_KV * CMP_HIDDEN, NSA_KV * HEAD_DIM), F32)
    for g in range(NSA_KV):
        w1 = w1.at[:, :, HEAD_DIM * g:HEAD_DIM * (g + 1), CMP_HIDDEN * g:CMP_HIDDEN * (g + 1)].set(phi_w1)
        w2 = w2.at[:, CMP_HIDDEN * g:CMP_HIDDEN * (g + 1), HEAD_DIM * g:HEAD_DIM * (g + 1)].set(phi_w2)
    kn = jnp.tile(kn_cmp, NSA_KV).reshape(1, 128)
    return pe, w1.astype(BF16), w2.astype(BF16), kn


def _overlap_expand(t_all):
    nc = t_all // CMP_STRIDE - 1
    m = nc + 1
    ns = -(-t_all // SEL_LEN)
    c0 = jnp.arange(m)[:, None] * CMP_STRIDE
    s0 = jnp.arange(LANE)[None, :] * SEL_LEN
    ov = ((c0 < s0 + SEL_LEN) & (c0 + CMP_LEN > s0) & (jnp.arange(m)[:, None] < nc)
          & (jnp.arange(LANE)[None, :] < ns)).astype(F32)
    kp = _round_up(t_all, LANE)
    ex = (jnp.arange(kp)[None, :] // SEL_LEN == jnp.arange(LANE)[:, None]).astype(BF16)
    return ov, ex


def _pick_tm(t_len):
    return 512 if t_len % 512 == 0 else 128


def _trunk(x3, prm, st):
    b, t_len, d = x3.shape
    m_rows = b * t_len
    x = x3.reshape(m_rows, d)
    prompt = st is None
    if prompt:
        tm = _pick_tm(t_len)
        c_len = 64
    else:
        tm = 256 if m_rows % 256 == 0 else m_rows
        c_len = t_len
    g512 = _block_diag_ones(512, HEAD_DIM)
    g128 = _block_diag_ones(128, HEAD_DIM)
    rel_bias = prm["rel_bias"]

    if prompt:
        ov, ex = _overlap_expand(t_len)
        nsp = _round_up(-(-t_len // SEL_LEN), 8)
        ovt = ov[:, :nsp].T
        ext = ex[:nsp, :t_len].T
        kk_ = jnp.arange(KT)[:, None]
        qq_ = jnp.arange(KT)[None, :]

        def band(dist):
            rows = dist.shape[0]
            t_ = _bias_table(rel_bias, dist, True).reshape(NSA_KV, NSA_QPG, rows, KT)
            return jnp.transpose(t_, (0, 2, 1, 3)).reshape(NSA_KV, rows, NSA_QPG * KT)

        tbt = jnp.stack([band(KT + qq_ - kk_), band(qq_ - kk_)], axis=1)
        cj = jnp.arange(2 * KT // CMP_STRIDE)[:, None] * CMP_STRIDE + (CMP_LEN - 1)
        bct = jnp.stack([band(KT + qq_ - cj), band(qq_ - cj)], axis=1)
    else:
        pt = st["page_table"]
        past = pt.shape[1] * PAGE
        wb = st["win"].shape[2]
        t_all = past + t_len
        ov, ex = _overlap_expand(t_all)
        mcmp = t_all // CMP_STRIDE
        kp = past + LANE
        wp = wb + LANE
        qp = past + jnp.arange(t_len)[:, None]
        cend = jnp.arange(mcmp)[None, :] * CMP_STRIDE + (CMP_LEN - 1)
        rows_ = NSA_QPG * t_len
        bc = _bias_table(rel_bias, qp - cend, False).reshape(NSA_KV, rows_, mcmp)
        bs = _bias_table(rel_bias, qp - jnp.arange(kp)[None, :], True).reshape(NSA_KV, rows_, kp)
        bw = _bias_table(rel_bias, qp - (past - wb + jnp.arange(wp)[None, :]), True).reshape(NSA_KV, rows_, wp)
        pool_cmp = jnp.swapaxes(st["cmp"].reshape(st["cmp"].shape[0], st["cmp"].shape[1], PAGE, KV_LANES), 2, 3)
        pool_slc = jnp.swapaxes(st["slc"].reshape(st["slc"].shape[0], st["slc"].shape[1], PAGE, KV_LANES), 2, 3)
        win_all = st["win"].reshape(b, st["win"].shape[1], wb, KV_LANES)
        win_all_t = jnp.swapaxes(win_all, 2, 3)

    cmp_r, slc_r, win_r, wkv_r, sh_r, conv_r, ffn_r = [], [], [], [], [], [], []
    for l in range(4):
        li = l // 2
        if l % 2 == 0:
            w_in = prm["w_in_even"][li].T
            w_nsa = jnp.pad(w_in[:NSA_PROJ], ((0, NSA_PROJ_PAD - NSA_PROJ), (0, 0))).astype(BF16)
            w_rw = w_in[NSA_PROJ:].astype(BF16)
            z_nsa = _norm_matmul(x, prm["norm_mix"][l], w_nsa, tm, NSA_PROJ_PAD)
            z_rw = _norm_matmul(x, prm["norm_mix"][l], w_rw, tm, RWKV_PROJ // 2)
            qg = jnp.tile(prm["q_norm"][li], NSA_HEADS).reshape(1, 512)
            kg = jnp.stack([jnp.tile(prm["k_norm"][li, 1], NSA_KV), jnp.tile(prm["k_norm"][li, 2], NSA_KV)])
            qn, kv3 = _nsa_prep(z_nsa, qg, kg, g512, g128, tm)
            cw = _compress_weights(prm["phi_pe"][li], prm["phi_w1"][li], prm["phi_w2"][li],
                                   prm["k_norm"][li, 0])
            kv3b = kv3.reshape(b, t_len, 768)
            if prompt:
                cc = _compress(kv3b, *cw, g128)
                nsa = _nsa_prompt(qn, z_nsa, cc, kv3, bct, tbt, ovt, ext, b, t_len)
                z_prev = jnp.zeros((b, RWKV_PROJ), F32)
                s0 = jnp.zeros((b, RWKV_HEADS, HEAD_DIM, HEAD_DIM), F32)
                n_keep = min(WINDOW, t_len)
                win_rows = kv3b[:, t_len - n_keep:, 512:768]
            else:
                nsa = _nsa_paged(pt, pool_cmp, pool_slc, li, kv3, win_all_t, qn, z_nsa, cw + (g128,),
                                 bc, bs, bw, ov, ex, b, t_len)
                z_prev = st["shift"][:, li]
                s0 = st["wkv"][:, li]
                n_keep = min(WINDOW, wb + t_len)
                win_rows = jnp.concatenate([win_all[:, li], kv3b[:, :, 512:768]], axis=1)[:, -n_keep:]
            pr = (prm["rwkv_mu"][li].reshape(1, -1), prm["rwkv_w0"][li].reshape(1, -1), prm["rwkv_w2"][li],
                  prm["rwkv_a0"][li].reshape(1, -1), prm["rwkv_a2"][li], prm["rwkv_g2"][li],
                  prm["rwkv_kk"][li].reshape(1, -1), prm["rwkv_ka"][li].reshape(1, -1),
                  prm["rwkv_rk"][li].reshape(1, -1), prm["rwkv_ln_w"][li].reshape(1, -1),
                  prm["rwkv_ln_b"][li].reshape(1, -1))
            rw, s_t = _rwkv(z_rw, z_prev, s0, pr, b, t_len, c_len)
            w_out = prm["w_out_even"][li].astype(BF16)
            if prompt:
                x = _matmul_res_t(nsa, rw.reshape(m_rows, 512), w_out, x, tm, 512)
            else:
                mix = jnp.concatenate([nsa.reshape(m_rows, 512), rw.reshape(m_rows, 512)], axis=1)
                x = _matmul_res(mix, w_out, x, tm, 512)
            cmp_r.append(kv3b[:, :, 0:256].reshape(b, t_len, 2, NSA_KV, HEAD_DIM))
            slc_r.append(kv3b[:, :, 256:512].reshape(b, t_len, 2, NSA_KV, HEAD_DIM))
            win_r.append(win_rows.reshape(b, n_keep, 2, NSA_KV, HEAD_DIM))
            wkv_r.append(s_t)
            sh_r.append(z_rw.reshape(b, t_len, RWKV_PROJ)[:, -1])
        else:
            prev = jnp.zeros((b, 2, D_MODEL), F32) if prompt else st["conv"][:, li]
            x, cs = _fused_block(x, prm["norm_mix"][l], prm["w_in_odd"][li].astype(BF16),
                                 prm["w_out_odd"][li].astype(BF16), prm["conv_w"][li], prev, "odd", t_len, tm)
            conv_r.append(cs)
        prev = jnp.zeros((b, 2, D_FF), F32) if prompt else st["ffn"][:, l]
        x, fs = _fused_block(x, prm["norm_ffn"][l], prm["ffn_up"][l].astype(BF16),
                             prm["ffn_down"][l].astype(BF16), prm["ffn_conv"][l], prev, "ffn", t_len, tm)
        ffn_r.append(fs)
    stk = lambda a_: jnp.stack(a_, axis=1)
    return x.reshape(b, t_len, d), (stk(cmp_r), stk(slc_r), stk(win_r), stk(wkv_r), stk(sh_r),
                                    stk(conv_r), stk(ffn_r))


def kernel(x_prompt, x_sample, cache_cmp_kv, cache_slc_kv, cache_win_kv, state_rwkv_wkv, state_rwkv_shift, state_conv, state_ffn_conv, page_table, norm_mix, norm_ffn, rel_bias, w_in_even, w_out_even, q_norm, k_norm, phi_pe, phi_w1, phi_w2, rwkv_mu, rwkv_w0, rwkv_w2, rwkv_a0, rwkv_a2, rwkv_g2, rwkv_kk, rwkv_ka, rwkv_rk, rwkv_ln_w, rwkv_ln_b, w_in_odd, conv_w, w_out_odd, ffn_up, ffn_conv, ffn_down):
    prm = dict(norm_mix=norm_mix, norm_ffn=norm_ffn, rel_bias=rel_bias, w_in_even=w_in_even,
               w_out_even=w_out_even, q_norm=q_norm, k_norm=k_norm, phi_pe=phi_pe, phi_w1=phi_w1,
               phi_w2=phi_w2, rwkv_mu=rwkv_mu, rwkv_w0=rwkv_w0, rwkv_w2=rwkv_w2, rwkv_a0=rwkv_a0,
               rwkv_a2=rwkv_a2, rwkv_g2=rwkv_g2, rwkv_kk=rwkv_kk, rwkv_ka=rwkv_ka, rwkv_rk=rwkv_rk,
               rwkv_ln_w=rwkv_ln_w, rwkv_ln_b=rwkv_ln_b, w_in_odd=w_in_odd, conv_w=conv_w,
               w_out_odd=w_out_odd, ffn_up=ffn_up, ffn_conv=ffn_conv, ffn_down=ffn_down)
    st = dict(cmp=cache_cmp_kv, slc=cache_slc_kv, win=cache_win_kv, wkv=state_rwkv_wkv,
              shift=state_rwkv_shift, conv=state_conv, ffn=state_ffn_conv, page_table=page_table)
    y_p, (cmp_p, slc_p, win_p, wkv_p, sh_p, conv_p, ffn_p) = _trunk(x_prompt, prm, None)
    y_s, (cmp_s, slc_s, win_s, wkv_s, sh_s, conv_s, ffn_s) = _trunk(x_sample, prm, st)
    return (y_p, y_s, cmp_p, cmp_s, slc_p, slc_s, win_p, win_s, wkv_p, wkv_s,
            sh_p, sh_s, conv_p, conv_s, ffn_p, ffn_s)
```

```python
import functools
import math

import jax
import jax.numpy as jnp
from jax import lax
from jax.experimental import pallas as pl
from jax.experimental.pallas import tpu as pltpu

F32 = jnp.float32
BF16 = jnp.bfloat16
HI = lax.Precision.HIGHEST

D_MODEL = 1024
HEAD_DIM = 64
NSA_HEADS = 8
NSA_KV = 2
NSA_QPG = 4
NSA_WIDTH = 512
KV_LANES = 2 * NSA_KV * HEAD_DIM
CMP_LEN = 32
CMP_STRIDE = 16
CMP_HIDDEN = 128
SEL_LEN = 64
N_SELECT = 16
WINDOW = 512
FORCE_SCORE = 1e9
RWKV_HEADS = 8
RWKV_WIDTH = 512
DECAY_LORA = 64
AAA_LORA = 64
GATE_LORA = 128
RWKV_PROJ = 3 * RWKV_WIDTH + DECAY_LORA + AAA_LORA + GATE_LORA
NSA_PROJ = NSA_WIDTH + 6 * NSA_KV * HEAD_DIM + 3 * NSA_HEADS
NSA_PROJ_PAD = 1408
GATE_COL_BLOCK = 10
D_FF = 2816
NUM_BUCKETS = 32
MAX_DISTANCE = 128
RMS_EPS = 1e-6
GN_EPS = 64e-5
PAGE = 128
LANE = 128
KT = 128
NEG = -1e30
VMEM_LIMIT = 56 * 1024 * 1024


def _cp(*sem):
    return pltpu.CompilerParams(dimension_semantics=sem, vmem_limit_bytes=VMEM_LIMIT)


def _round_up(a, b):
    return (a + b - 1) // b * b


def _nt(a, b, precision=None):
    return lax.dot_general(a, b, (((1,), (1,)), ((), ())), precision=precision,
                           preferred_element_type=F32)


def _tn(a, b, precision=None):
    return lax.dot_general(a, b, (((0,), (0,)), ((), ())), precision=precision,
                           preferred_element_type=F32)


_NN = ((1,), (0,))
_NT = ((1,), (1,))
_TN = ((0,), (0,))
RWKV_MM = "bf16"
RWKV_LORA_MM = "bf16"


def _mmp(a, b, dims, mode=None):
    mode = RWKV_MM if mode is None else mode
    dn = (dims, ((), ()))
    if mode == "f32":
        return lax.dot_general(a, b, dn, precision=HI, preferred_element_type=F32)
    if mode == "bf16":
        return lax.dot_general(a.astype(BF16), b.astype(BF16), dn, preferred_element_type=F32)
    ah = a.astype(BF16)
    al = (a - ah.astype(F32)).astype(BF16)
    bh = b.astype(BF16)
    bl = (b - bh.astype(F32)).astype(BF16)
    dot = lambda x, y: lax.dot_general(x, y, dn, preferred_element_type=F32)
    return dot(ah, bh) + (dot(ah, bl) + dot(al, bh))


def _norm_mm_kernel(x_ref, g_ref, w_ref, o_ref, xn_ref):
    @pl.when(pl.program_id(1) == 0)
    def _():
        x = x_ref[...]
        ms = jnp.mean(x * x, axis=-1, keepdims=True)
        xn_ref[...] = (x * lax.rsqrt(ms + RMS_EPS) * g_ref[...]).astype(BF16)

    o_ref[...] = _nt(xn_ref[...], w_ref[...])


def _norm_matmul(x, g, w_t, tm, tn):
    m, k = x.shape
    n = w_t.shape[0]
    return pl.pallas_call(
        _norm_mm_kernel, name="norm_mm",
        grid=(m // tm, n // tn),
        in_specs=[pl.BlockSpec((tm, k), lambda i, j: (i, 0)),
                  pl.BlockSpec((1, k), lambda i, j: (0, 0)),
                  pl.BlockSpec((tn, k), lambda i, j: (j, 0))],
        out_specs=pl.BlockSpec((tm, tn), lambda i, j: (i, j)),
        out_shape=jax.ShapeDtypeStruct((m, n), F32),
        scratch_shapes=[pltpu.VMEM((tm, k), BF16)],
        compiler_params=_cp("arbitrary", "arbitrary"),
    )(x, g.reshape(1, k), w_t)


def _mm_res_kernel(a_ref, w_ref, r_ref, o_ref):
    o_ref[...] = r_ref[...] + jnp.dot(a_ref[...], w_ref[...], preferred_element_type=F32)


def _matmul_res(a, w, res, tm, tn):
    m, k = a.shape
    n = w.shape[1]
    return pl.pallas_call(
        _mm_res_kernel, name="mm_res",
        grid=(m // tm, n // tn),
        in_specs=[pl.BlockSpec((tm, k), lambda i, j: (i, 0)),
                  pl.BlockSpec((k, tn), lambda i, j: (0, j)),
                  pl.BlockSpec((tm, tn), lambda i, j: (i, j))],
        out_specs=pl.BlockSpec((tm, tn), lambda i, j: (i, j)),
        out_shape=jax.ShapeDtypeStruct((m, n), F32),
        compiler_params=_cp("arbitrary", "arbitrary"),
    )(a, w, res)


def _mm_res_t_kernel(at_ref, b_ref, wa_ref, wb_ref, r_ref, o_ref):
    o_ref[...] = (r_ref[...] + _tn(at_ref[0], wa_ref[...])
                  + jnp.dot(b_ref[...], wb_ref[...], preferred_element_type=F32))


def _matmul_res_t(a_t, bmat, w, res, tm, tn):
    nb, ka, t_len = a_t.shape
    m, kb = bmat.shape
    n = w.shape[1]
    tpb = t_len // tm
    return pl.pallas_call(
        _mm_res_t_kernel, name="mm_res_t",
        grid=(m // tm, n // tn),
        in_specs=[pl.BlockSpec((1, ka, tm), lambda i, j: (i // tpb, 0, i % tpb)),
                  pl.BlockSpec((tm, kb), lambda i, j: (i, 0)),
                  pl.BlockSpec((ka, tn), lambda i, j: (0, j)),
                  pl.BlockSpec((kb, tn), lambda i, j: (ka // kb, j)),
                  pl.BlockSpec((tm, tn), lambda i, j: (i, j))],
        out_specs=pl.BlockSpec((tm, tn), lambda i, j: (i, j)),
        out_shape=jax.ShapeDtypeStruct((m, n), F32),
        compiler_params=_cp("arbitrary", "arbitrary"),
    )(a_t, bmat, w, w, res)


FUSED_CHUNK = 256


def _fused_kernel(x_ref, g_ref, win_ref, wout_ref, cw_ref, prev_ref, o_ref, st_ref, carry_ref,
                  *, mode, nbr, seq_tiles, nseq, lt):
    i = pl.program_id(0)
    x = x_ref[...]
    ms = jnp.mean(x * x, axis=-1, keepdims=True)
    xn = (x * lax.rsqrt(ms + RMS_EPS) * g_ref[...]).astype(BF16)
    f = cw_ref.shape[-1]
    ch = FUSED_CHUNK
    nchunk = f // ch
    if seq_tiles == 1:
        p = prev_ref[...]
    else:
        p = jnp.where(i % seq_tiles == 0, prev_ref[...], carry_ref[...])
    t = lax.broadcasted_iota(jnp.int32, (nseq, lt, ch), 1)

    def project(c):
        return [jnp.dot(xn, win_ref[:, r * f + c * ch:r * f + (c + 1) * ch], preferred_element_type=F32)
                for r in range(nbr)]

    def gate(c, br):
        cs = slice(c * ch, (c + 1) * ch)
        if mode == "ffn":
            cin, other = br
        else:
            other, cg, xi = br
            cin = cg * xi
        c3 = cin.reshape(nseq, lt, ch)
        p0 = p[:, 0:1, cs]
        p1 = p[:, 1:2, cs]
        c1 = jnp.where(t == 0, p1, pltpu.roll(c3, 1, 1))
        c2 = jnp.where(t == 0, p0, jnp.where(t == 1, p1, pltpu.roll(c3, 2, 1)))
        y = c2 * cw_ref[0:1, cs] + c1 * cw_ref[1:2, cs] + c3 * cw_ref[2:3, cs]
        last2 = c3[:, lt - 2:lt, :]
        st_ref[:, :, cs] = last2
        if seq_tiles > 1:
            carry_ref[:, :, cs] = last2
        o3 = other.reshape(c3.shape)
        out = (y * jax.nn.sigmoid(y)) * o3 if mode == "ffn" else o3 * y
        return out.reshape(nseq * lt, ch).astype(BF16)

    acc = jnp.zeros(x.shape, F32)
    br = project(0)
    for c in range(nchunk):
        br_next = project(c + 1) if c + 1 < nchunk else None
        gch = gate(c, br)
        acc = acc + jnp.dot(gch, wout_ref[c * ch:(c + 1) * ch, :], preferred_element_type=F32)
        br = br_next
    o_ref[...] = x + acc


def _fused_block(x, g, w_in, w_out, cw, prev, mode, seq_len, tm):
    m, d = x.shape
    nbr = 2 if mode == "ffn" else 3
    f = w_in.shape[1] // nbr
    if seq_len >= tm:
        seq_tiles, nseq, lt = seq_len // tm, 1, tm
    else:
        seq_tiles, nseq, lt = 1, tm // seq_len, seq_len
    b = prev.shape[0]
    kern = functools.partial(_fused_kernel, mode=mode, nbr=nbr, seq_tiles=seq_tiles, nseq=nseq, lt=lt)
    return pl.pallas_call(
        kern, name="fused_" + mode,
        grid=(m // tm,),
        in_specs=[pl.BlockSpec((tm, d), lambda i: (i, 0)),
                  pl.BlockSpec((1, d), lambda i: (0, 0)),
                  pl.BlockSpec((d, nbr * f), lambda i: (0, 0), pipeline_mode=pl.Buffered(1)),
                  pl.BlockSpec((f, d), lambda i: (0, 0), pipeline_mode=pl.Buffered(1)),
                  pl.BlockSpec((3, f), lambda i: (0, 0)),
                  pl.BlockSpec((nseq, 2, f), lambda i: (i // seq_tiles, 0, 0))],
        out_specs=[pl.BlockSpec((tm, d), lambda i: (i, 0)),
                   pl.BlockSpec((nseq, 2, f), lambda i: (i // seq_tiles, 0, 0))],
        out_shape=[jax.ShapeDtypeStruct((m, d), F32),
                   jax.ShapeDtypeStruct((b, 2, f), F32)],
        scratch_shapes=[pltpu.VMEM((nseq, 2, f), F32)],
        compiler_params=_cp("arbitrary"),
    )(x, g.reshape(1, d), w_in, w_out, cw, prev)


def _gated_kernel(x_ref, g_ref, *rest, mode, nbr, seq_tiles, nseq, lt):
    w_refs = rest[:nbr]
    cw_ref, prev_ref, o_ref, st_ref, xn_ref, carry_ref = rest[nbr:]
    i = pl.program_id(0)
    j = pl.program_id(1)

    @pl.when(j == 0)
    def _():
        x = x_ref[...]
        ms = jnp.mean(x * x, axis=-1, keepdims=True)
        xn_ref[...] = (x * lax.rsqrt(ms + RMS_EPS) * g_ref[...]).astype(BF16)

    xn = xn_ref[...]
    br = [jnp.dot(xn, w[...], preferred_element_type=F32) for w in w_refs]
    if mode == "ffn":
        cin, other = br
    else:
        other, cg, xi = br
        cin = cg * xi
    tf = cin.shape[-1]
    c = cin.reshape(nseq, lt, tf)
    if seq_tiles == 1:
        p = prev_ref[...]
    else:
        p = jnp.where(i % seq_tiles == 0, prev_ref[...], carry_ref[j])
    p0 = p[:, 0:1, :]
    p1 = p[:, 1:2, :]
    t = lax.broadcasted_iota(jnp.int32, c.shape, 1)
    c1 = jnp.where(t == 0, p1, pltpu.roll(c, 1, 1))
    c2 = jnp.where(t == 0, p0, jnp.where(t == 1, p1, pltpu.roll(c, 2, 1)))
    cw = cw_ref[...]
    y = c2 * cw[0:1, :] + c1 * cw[1:2, :] + c * cw[2:3, :]
    last2 = c[:, lt - 2:lt, :]
    for jj in range(st_ref.shape[-1] // tf):
        @pl.when(j == jj)
        def _(jj=jj):
            st_ref[:, :, jj * tf:(jj + 1) * tf] = last2
    if seq_tiles > 1:
        carry_ref[j] = last2
    o3 = other.reshape(c.shape)
    if mode == "ffn":
        out = (y * jax.nn.sigmoid(y)) * o3
    else:
        out = o3 * y
    o_ref[...] = out.reshape(nseq * lt, tf).astype(BF16)


def _gated(x, g, w, cw, prev, mode, seq_len, tm, tf):
    m, k = x.shape
    nbr = 2 if mode == "ffn" else 3
    f = w.shape[1] // nbr
    nj = f // tf
    if seq_len >= tm:
        seq_tiles, nseq, lt = seq_len // tm, 1, tm
    else:
        seq_tiles, nseq, lt = 1, tm // seq_len, seq_len
    b = prev.shape[0]
    w_specs = [pl.BlockSpec((k, tf), functools.partial(lambda i, j, o: (0, j + o), o=r * nj))
               for r in range(nbr)]
    kern = functools.partial(_gated_kernel, mode=mode, nbr=nbr, seq_tiles=seq_tiles, nseq=nseq, lt=lt)
    return pl.pallas_call(
        kern, name="gated_" + mode,
        grid=(m // tm, nj),
        in_specs=[pl.BlockSpec((tm, k), lambda i, j: (i, 0)),
                  pl.BlockSpec((1, k), lambda i, j: (0, 0))] + w_specs + [
                  pl.BlockSpec((3, tf), lambda i, j: (0, j)),
                  pl.BlockSpec((nseq, 2, tf), lambda i, j: (i // seq_tiles, 0, j))],
        out_specs=[pl.BlockSpec((tm, tf), lambda i, j: (i, j)),
                   pl.BlockSpec((nseq, 2, f), lambda i, j: (i // seq_tiles, 0, 0))],
        out_shape=[jax.ShapeDtypeStruct((m, f), BF16),
                   jax.ShapeDtypeStruct((b, 2, f), F32)],
        scratch_shapes=[pltpu.VMEM((tm, k), BF16),
                        pltpu.VMEM((nj, nseq, 2, tf), F32)],
        compiler_params=_cp("arbitrary", "arbitrary"),
    )(x, g.reshape(1, k), *([w] * nbr), cw, prev)


def _nsa_prep_kernel(z_ref, qg_ref, kg_ref, g512_ref, g128_ref, q_ref, kv_ref):
    z = z_ref[...]
    q = z[:, 0:NSA_WIDTH]
    ms = jnp.dot(q * q, g512_ref[...], precision=HI, preferred_element_type=F32)
    q_ref[...] = (q * lax.rsqrt(ms + RMS_EPS) * qg_ref[...] * (HEAD_DIM ** -0.5)).astype(BF16)
    kv_ref[:, 0:256] = z[:, 512:768]
    for r, off in ((0, 768), (1, 1024)):
        k = z[:, off:off + 128]
        ms = jnp.dot(k * k, g128_ref[...], precision=HI, preferred_element_type=F32)
        kv_ref[:, off - 512:off - 384] = k * lax.rsqrt(ms + RMS_EPS) * kg_ref[r:r + 1, :]
        kv_ref[:, off - 384:off - 256] = z[:, off + 128:off + 256]


def _nsa_prep(z, qg, kg, g512, g128, tm):
    m = z.shape[0]
    const = lambda i: (0, 0)
    return pl.pallas_call(
        _nsa_prep_kernel, name="nsa_prep",
        grid=(m // tm,),
        in_specs=[pl.BlockSpec((tm, NSA_PROJ_PAD), lambda i: (i, 0)),
                  pl.BlockSpec((1, 512), const), pl.BlockSpec((2, 128), const),
                  pl.BlockSpec((512, 512), const), pl.BlockSpec((128, 128), const)],
        out_specs=[pl.BlockSpec((tm, 512), lambda i: (i, 0)),
                   pl.BlockSpec((tm, 768), lambda i: (i, 0))],
        out_shape=[jax.ShapeDtypeStruct((m, 512), BF16),
                   jax.ShapeDtypeStruct((m, 768), F32)],
        compiler_params=_cp("arbitrary"),
    )(z, qg, kg, g512, g128)


def _compress_rows(rows_refs, m, pe_ref, w1_ref, w2_ref, kn_ref, g128_ref):
    outs = []
    for kv, rows_ref in enumerate(rows_refs):
        acc0 = jnp.zeros((m, NSA_KV * CMP_HIDDEN), F32)
        acc1 = jnp.zeros((m, NSA_KV * CMP_HIDDEN), F32)
        for j in range(CMP_STRIDE):
            xj = rows_ref[pl.ds(j, m, stride=CMP_STRIDE), :]
            acc0 = acc0 + jnp.dot((xj + pe_ref[kv, j:j + 1, :]).astype(BF16), w1_ref[kv, j],
                                  preferred_element_type=F32)
            acc1 = acc1 + jnp.dot((xj + pe_ref[kv, CMP_STRIDE + j:CMP_STRIDE + j + 1, :]).astype(BF16),
                                  w1_ref[kv, CMP_STRIDE + j], preferred_element_type=F32)
        hid = acc0 + pltpu.roll(acc1, m - 1, 0)
        outs.append(jnp.dot(jax.nn.gelu(hid).astype(BF16), w2_ref[kv], preferred_element_type=F32))
    kc, vc = outs
    ms = jnp.dot(kc * kc, g128_ref[...], precision=HI, preferred_element_type=F32)
    kc = kc * lax.rsqrt(ms + RMS_EPS) * kn_ref[...]
    return jnp.concatenate([kc, vc], axis=1)


def _compress_kernel(k_ref, v_ref, pe_ref, w1_ref, w2_ref, kn_ref, g128_ref, o_ref, *, m):
    o_ref[0] = _compress_rows((k_ref.at[0], v_ref.at[0]), m, pe_ref, w1_ref, w2_ref, kn_ref, g128_ref)


def _compress(kv, pe, w1, w2, kn, g128):
    b, t, _ = kv.shape
    m = t // CMP_STRIDE
    c2 = lambda i: (0, 0)
    hw = NSA_KV * HEAD_DIM
    return pl.pallas_call(
        functools.partial(_compress_kernel, m=m), name="compress",
        grid=(b,),
        in_specs=[pl.BlockSpec((1, t, hw), lambda i: (i, 0, 0)),
                  pl.BlockSpec((1, t, hw), lambda i: (i, 0, 1)),
                  pl.BlockSpec((2, CMP_LEN, hw), lambda i: (0, 0, 0)),
                  pl.BlockSpec((2, CMP_LEN, hw, NSA_KV * CMP_HIDDEN), lambda i: (0, 0, 0, 0)),
                  pl.BlockSpec((2, NSA_KV * CMP_HIDDEN, hw), lambda i: (0, 0, 0)),
                  pl.BlockSpec((1, 128), c2), pl.BlockSpec((128, 128), c2)],
        out_specs=pl.BlockSpec((1, m, KV_LANES), lambda i: (i, 0, 0)),
        out_shape=jax.ShapeDtypeStruct((b, m, KV_LANES), F32),
        compiler_params=_cp("arbitrary"),
    )(kv, kv, pe, w1, w2, kn, g128)


def _msoftmax(s, mask):
    s = jnp.where(mask, s, NEG)
    mx = jnp.max(s, axis=-1, keepdims=True)
    e = jnp.where(mask, jnp.exp(s - mx), 0.0)
    den = jnp.sum(e, axis=-1, keepdims=True)
    return e / jnp.where(den > 0, den, 1.0)


def _select_blocks(imp, qpos, ns):
    s_io = lax.broadcasted_iota(jnp.int32, imp.shape, 1)
    cur = qpos // SEL_LEN
    forced = (s_io == 0) | (s_io == cur) | (s_io == cur - 1)
    future = s_io * SEL_LEN > qpos
    imp = jnp.where(forced, FORCE_SCORE, imp)
    imp = jnp.where(future, -FORCE_SCORE, imp)
    imp = jnp.where(s_io >= ns, -3e38, imp)
    rank = jnp.zeros(imp.shape, jnp.int32)
    for sp in range(ns):
        col = imp[:, sp:sp + 1]
        beats = (col > imp) | ((col == imp) & (s_io > sp))
        rank = rank + beats.astype(jnp.int32)
    sel = (rank < min(N_SELECT, ns)) & (s_io < ns)
    return sel.astype(F32)


def _heads_to_rows(q, g):
    return jnp.concatenate(
        [q[:, HEAD_DIM * (NSA_QPG * g + h):HEAD_DIM * (NSA_QPG * g + h + 1)] for h in range(NSA_QPG)],
        axis=0)


def _msoftmax0(s, mask):
    s = jnp.where(mask, s, NEG)
    mx = jnp.max(s, axis=0, keepdims=True)
    e = jnp.where(mask, jnp.exp(s - mx), 0.0)
    den = jnp.sum(e, axis=0, keepdims=True)
    return e / jnp.where(den > 0, den, 1.0)


def _select_blocks_t(imp, qpos, ns):
    s_io = lax.broadcasted_iota(jnp.int32, imp.shape, 0)
    cur = qpos // SEL_LEN
    forced = (s_io == 0) | (s_io == cur) | (s_io == cur - 1)
    future = s_io * SEL_LEN > qpos
    imp = jnp.where(forced, FORCE_SCORE, imp)
    imp = jnp.where(future, -FORCE_SCORE, imp)
    imp = jnp.where(s_io >= ns, -3e38, imp)
    rank = jnp.zeros(imp.shape, jnp.int32)
    for sp in range(ns):
        row = imp[sp:sp + 1, :]
        beats = (row > imp) | ((row == imp) & (s_io > sp))
        rank = rank + beats.astype(jnp.int32)
    sel = (rank < min(N_SELECT, ns)) & (s_io < ns)
    return sel.astype(F32)


def _flash_steps_t(chains):
    sts = [_nt(k, qg) for (qg, k, _, _, _, _) in chains]
    mids = []
    for st, (_, _, _, bias, mask, (m_, l_, _)) in zip(sts, chains):
        ps, ms, ls, als = [], [], [], []
        for h in range(NSA_QPG):
            hs = slice(KT * h, KT * (h + 1))
            s = st[:, hs]
            if bias is not None:
                s = s + bias[:, hs]
            s = jnp.where(mask, s, NEG)
            m_new = jnp.maximum(m_[:, hs], jnp.max(s, axis=0, keepdims=True))
            alpha = jnp.exp(m_[:, hs] - m_new)
            e = jnp.where(mask, jnp.exp(s - m_new), 0.0)
            ls.append(alpha * l_[:, hs] + jnp.sum(e, axis=0, keepdims=True))
            ms.append(m_new)
            als.append(alpha)
            ps.append(e.astype(BF16))
        mids.append((jnp.concatenate(ps, axis=1), jnp.concatenate(ms, axis=1),
                     jnp.concatenate(ls, axis=1), jnp.concatenate(als, axis=1)))
    out = []
    for (p, m_new, l_new, alpha), (_, _, v, _, _, (_, _, acc)) in zip(mids, chains):
        out.append((m_new, l_new, alpha * acc + _tn(v, p)))
    return out


def _flash_init_t():
    return (jnp.full((1, NSA_QPG * KT), NEG, F32), jnp.zeros((1, NSA_QPG * KT), F32),
            jnp.zeros((HEAD_DIM, NSA_QPG * KT), F32))


def _flash_out_t(carry):
    _, l_, acc = carry
    return acc / jnp.where(l_ > 0, l_, 1.0)


def _nsa_prompt_kernel(q_ref, gt_ref, cc_ref, kv_ref, bc_ref, tb_ref, ovt_ref, ext_ref, o_ref, selk_s, sc_s,
                       *, m, ns):
    i = pl.program_id(1)
    q = q_ref[0]
    gst = jax.nn.sigmoid(gt_ref[...]).T
    k_io = lax.broadcasted_iota(jnp.int32, (KT, KT), 0)
    q_io = lax.broadcasted_iota(jnp.int32, (KT, KT), 1)
    qpos = i * KT + lax.broadcasted_iota(jnp.int32, (1, KT), 1)
    n_io = lax.broadcasted_iota(jnp.int32, (m, KT), 0)
    mask_c = ((n_io * CMP_STRIDE + (CMP_LEN - 1)) <= qpos) & (n_io < m - 1)
    cc = cc_ref[0]
    groups = range(NSA_KV)
    qgs = [_heads_to_rows(q, g) for g in groups]

    kcs = [cc[:, HEAD_DIM * g:HEAD_DIM * (g + 1)].astype(BF16) for g in groups]
    vcs = [cc[:, 128 + HEAD_DIM * g:128 + HEAD_DIM * (g + 1)].astype(BF16) for g in groups]
    c_off = pl.multiple_of(jnp.maximum(i - 1, 0) * (KT // CMP_STRIDE), KT // CMP_STRIDE)
    sts = []
    for g in groups:
        sc_s[g] = _nt(kcs[g], qgs[g])
        band = jnp.where(i == 0, bc_ref[g, 1], bc_ref[g, 0])
        sc_s[g, pl.ds(c_off, 2 * KT // CMP_STRIDE), :] += band
        sts.append(sc_s[g])
    pcs = [[_msoftmax0(sts[g][:, KT * h:KT * (h + 1)], mask_c) for h in range(NSA_QPG)]
           for g in groups]
    o_cs = [_tn(vcs[g], jnp.concatenate(pcs[g], axis=1).astype(BF16)) for g in groups]
    imps = [jnp.dot(ovt_ref[...], pcs[g][0] + pcs[g][1] + pcs[g][2] + pcs[g][3], precision=HI,
                    preferred_element_type=F32) for g in groups]
    sels = [_select_blocks_t(imps[g], qpos, ns).astype(BF16) for g in groups]
    for g in groups:
        selk_s[g] = jnp.dot(ext_ref[...], sels[g], preferred_element_type=F32)

    def load_kv(g, kt, off, ntile=1):
        r0 = pl.multiple_of(kt * KT, KT)
        k_lo = 256 + off + HEAD_DIM * g
        v_lo = 384 + off + HEAD_DIM * g
        k = kv_ref[0, pl.ds(r0, ntile * KT), k_lo:k_lo + HEAD_DIM].astype(BF16)
        v = kv_ref[0, pl.ds(r0, ntile * KT), v_lo:v_lo + HEAD_DIM].astype(BF16)
        return k, v

    def sel_mask(g, kt, ntile=1):
        r0 = pl.multiple_of(kt * KT, KT)
        return selk_s[g, pl.ds(r0, ntile * KT), :] > 0.5

    n_main = jnp.maximum(i - 1, 0)

    def body(pr, carry):
        chains = []
        for g in groups:
            k, v = load_kv(g, 2 * pr, 0, 2)
            chains.append((qgs[g], k, v, None, sel_mask(g, 2 * pr, 2), carry[g]))
        return tuple(_flash_steps_t(chains))

    c_sel = list(lax.fori_loop(0, n_main // 2, body, tuple(_flash_init_t() for _ in groups)))
    kt_odd = (n_main // 2) * 2
    has_odd = n_main % 2 == 1

    nwt = WINDOW // KT
    c_win = [_flash_init_t() for _ in groups]
    for d in range(nwt, -1, -1):
        kt = i - d
        ktc = jnp.maximum(kt, 0)
        if d == nwt:
            mk = (k_io > q_io) & (kt >= 0)
        elif d == 0:
            mk = k_io <= q_io
        else:
            mk = kt >= 0
        chains = []
        for g in groups:
            k, v = load_kv(g, ktc, 256)
            bias = tb_ref[g, 0] if d == 1 else (tb_ref[g, 1] if d == 0 else None)
            chains.append((qgs[g], k, v, bias, mk, c_win[g]))
        if d == nwt:
            for g in groups:
                k, v = load_kv(g, kt_odd, 0)
                chains.append((qgs[g], k, v, None, sel_mask(g, kt_odd) & has_odd, c_sel[g]))
        elif d <= 1:
            for g in groups:
                k, v = load_kv(g, ktc, 0)
                smk = sel_mask(g, ktc) & ((kt >= 0) if d == 1 else (k_io <= q_io))
                chains.append((qgs[g], k, v, tb_ref[g, 1 - d], smk, c_sel[g]))
        res = _flash_steps_t(chains)
        c_win = res[:NSA_KV]
        if len(res) > NSA_KV:
            c_sel = res[NSA_KV:]
    o_ss = [_flash_out_t(c_sel[g]) for g in groups]
    o_ws = [_flash_out_t(c_win[g]) for g in groups]

    for g in groups:
        for h in range(NSA_QPG):
            hh = NSA_QPG * g + h
            hs = slice(KT * h, KT * (h + 1))
            o = (gst[3 * hh:3 * hh + 1, :] * o_cs[g][:, hs] + gst[3 * hh + 1:3 * hh + 2, :] * o_ss[g][:, hs]
                 + gst[3 * hh + 2:3 * hh + 3, :] * o_ws[g][:, hs])
            o_ref[0, HEAD_DIM * hh:HEAD_DIM * (hh + 1), :] = o.astype(BF16)


def _nsa_prompt(qn, z_nsa, cc, kv, bct, tbt, ovt, ext, b, t_len):
    m = cc.shape[1]
    ns = -(-t_len // SEL_LEN)
    nq = t_len // KT
    nsp = ovt.shape[0]
    c2 = lambda bi, i: (0, 0)
    return pl.pallas_call(
        functools.partial(_nsa_prompt_kernel, m=m, ns=ns), name="nsa_prompt",
        grid=(b, nq),
        in_specs=[pl.BlockSpec((1, KT, 512), lambda bi, i: (bi, i, 0)),
                  pl.BlockSpec((KT, LANE), lambda bi, i: (bi * nq + i, GATE_COL_BLOCK)),
                  pl.BlockSpec((1, m, KV_LANES), lambda bi, i: (bi, 0, 0)),
                  pl.BlockSpec((1, t_len, 768), lambda bi, i: (bi, 0, 0)),
                  pl.BlockSpec((NSA_KV, 2, 2 * KT // CMP_STRIDE, NSA_QPG * KT), lambda bi, i: (0, 0, 0, 0)),
                  pl.BlockSpec((NSA_KV, 2, KT, NSA_QPG * KT), lambda bi, i: (0, 0, 0, 0)),
                  pl.BlockSpec((nsp, m), c2),
                  pl.BlockSpec((t_len, nsp), c2)],
        out_specs=pl.BlockSpec((1, 512, KT), lambda bi, i: (bi, 0, i)),
        out_shape=jax.ShapeDtypeStruct((b, 512, t_len), BF16),
        scratch_shapes=[pltpu.VMEM((NSA_KV, t_len, KT), F32),
                        pltpu.VMEM((NSA_KV, m, NSA_QPG * KT), F32)],
        compiler_params=_cp("arbitrary", "arbitrary"),
    )(qn.reshape(b, t_len, 512), z_nsa, cc, kv.reshape(b, t_len, 768), bct, tbt, ovt, ext)


def _nsa_sample_kernel(pt_ref, cmp_pg, slc_pg, new_ref, win_ref, q_ref, gt_ref, pe_ref, w1_ref, w2_ref,
                       kn_ref, g128_ref, bc_ref, bs_ref, bw_ref, ov_ref, ex_ref, o_ref,
                       cmpk_s, cmpv_s, slc_s, win_s, *, npages, past, wb, tq, m, ns):
    p = pl.program_id(1)
    r0 = pl.multiple_of(p * PAGE, PAGE)
    cmpk_s[pl.ds(r0, PAGE), :] = cmp_pg[0, 0, :, 0:128]
    cmpv_s[pl.ds(r0, PAGE), :] = cmp_pg[0, 0, :, 128:256]
    slc_s[pl.ds(r0, PAGE), :] = slc_pg[0, 0]

    @pl.when(p == npages - 1)
    def _():
        kp = slc_s.shape[0]
        wp = win_s.shape[0]
        new = new_ref[0]
        pad = jnp.zeros((kp - past - tq, KV_LANES), F32)
        cmpk_s[past:past + tq, :] = new[:, 0:128]
        cmpk_s[past + tq:kp, :] = pad[:, 0:128]
        cmpv_s[past:past + tq, :] = new[:, 128:256]
        cmpv_s[past + tq:kp, :] = pad[:, 0:128]
        slc_s[past:past + tq, :] = new[:, 256:512]
        slc_s[past + tq:kp, :] = pad
        win_s[0:wb, :] = win_ref[0, 0]
        win_s[wb:wb + tq, :] = new[:, 512:768]
        win_s[wb + tq:wp, :] = jnp.zeros((wp - wb - tq, KV_LANES), F32)

        cc = _compress_rows((cmpk_s, cmpv_s), m, pe_ref, w1_ref, w2_ref, kn_ref, g128_ref)
        q = q_ref[0]
        gs = jax.nn.sigmoid(gt_ref[0])
        qpos = past + lax.broadcasted_iota(jnp.int32, (tq, 1), 0)
        n_io = lax.broadcasted_iota(jnp.int32, (tq, m), 1)
        mask_c = ((n_io * CMP_STRIDE + (CMP_LEN - 1)) <= qpos) & (n_io < m - 1)
        k_io = lax.broadcasted_iota(jnp.int32, (tq, kp), 1)
        mask_s0 = (k_io <= qpos) & (k_io < past + tq)
        j_io = lax.broadcasted_iota(jnp.int32, (tq, wp), 1)
        dist_w = qpos - (past - wb + j_io)
        mask_w = (dist_w >= 0) & (dist_w < WINDOW) & (j_io < wb + tq)
        outs = []
        for g in range(NSA_KV):
            qg = _heads_to_rows(q, g)
            kc = cc[:, HEAD_DIM * g:HEAD_DIM * (g + 1)].astype(BF16)
            vc = cc[:, 128 + HEAD_DIM * g:128 + HEAD_DIM * (g + 1)].astype(BF16)
            s = _nt(qg, kc).reshape(NSA_QPG, tq, m) + bc_ref[NSA_QPG * g:NSA_QPG * (g + 1)]
            p_c = _msoftmax(s, mask_c[None])
            o_c = jnp.dot(p_c.reshape(NSA_QPG * tq, m).astype(BF16), vc,
                          preferred_element_type=F32).reshape(NSA_QPG, tq, HEAD_DIM)
            psum = p_c[0] + p_c[1] + p_c[2] + p_c[3]
            imp = jnp.dot(psum, ov_ref[...], precision=HI, preferred_element_type=F32)
            sel = _select_blocks(imp, qpos, ns)
            selk = jnp.dot(sel.astype(BF16), ex_ref[...], preferred_element_type=F32)

            ks = slc_s[:, HEAD_DIM * g:HEAD_DIM * (g + 1)].astype(BF16)
            vs = slc_s[:, 128 + HEAD_DIM * g:128 + HEAD_DIM * (g + 1)].astype(BF16)
            s = _nt(qg, ks).reshape(NSA_QPG, tq, kp) + bs_ref[NSA_QPG * g:NSA_QPG * (g + 1)]
            p_s = _msoftmax(s, ((selk > 0.5) & mask_s0)[None])
            o_s = jnp.dot(p_s.reshape(NSA_QPG * tq, kp).astype(BF16), vs,
                          preferred_element_type=F32).reshape(NSA_QPG, tq, HEAD_DIM)

            kw = win_s[:, HEAD_DIM * g:HEAD_DIM * (g + 1)].astype(BF16)
            vw = win_s[:, 128 + HEAD_DIM * g:128 + HEAD_DIM * (g + 1)].astype(BF16)
            s = _nt(qg, kw).reshape(NSA_QPG, tq, wp) + bw_ref[NSA_QPG * g:NSA_QPG * (g + 1)]
            p_w = _msoftmax(s, mask_w[None])
            o_w = jnp.dot(p_w.reshape(NSA_QPG * tq, wp).astype(BF16), vw,
                          preferred_element_type=F32).reshape(NSA_QPG, tq, HEAD_DIM)
            for h in range(NSA_QPG):
                hh = NSA_QPG * g + h
                o = (gs[:, 3 * hh:3 * hh + 1] * o_c[h] + gs[:, 3 * hh + 1:3 * hh + 2] * o_s[h]
                     + gs[:, 3 * hh + 2:3 * hh + 3] * o_w[h])
                outs.append(o)
        o_ref[0] = jnp.concatenate(outs, axis=1).astype(BF16)


def _nsa_sample(pt, pool_cmp, pool_slc, li, kv_new, win, qn, z_nsa, cw, bc, bs, bw, ov, ex, b, tq):
    npages = pt.shape[1]
    past = npages * PAGE
    wb = win.shape[2]
    kp = _round_up(past + tq, LANE)
    wp = _round_up(wb + tq, LANE)
    m = (past + tq) // CMP_STRIDE
    ns = -(-(past + tq) // SEL_LEN)
    pe, w1, w2, kn, g128 = cw
    c2 = lambda bi, p, pt_: (0, 0)
    c3 = lambda bi, p, pt_: (0, 0, 0)
    per_b3 = lambda bi, p, pt_: (bi, 0, 0)
    grid_spec = pltpu.PrefetchScalarGridSpec(
        num_scalar_prefetch=1,
        grid=(b, npages),
        in_specs=[pl.BlockSpec((1, 1, PAGE, KV_LANES), lambda bi, p, pt_: (pt_[bi, p], li, 0, 0)),
                  pl.BlockSpec((1, 1, PAGE, KV_LANES), lambda bi, p, pt_: (pt_[bi, p], li, 0, 0)),
                  pl.BlockSpec((1, tq, 768), per_b3),
                  pl.BlockSpec((1, 1, wb, KV_LANES), lambda bi, p, pt_: (bi, li, 0, 0)),
                  pl.BlockSpec((1, tq, 512), per_b3),
                  pl.BlockSpec((1, tq, LANE), lambda bi, p, pt_: (bi, 0, GATE_COL_BLOCK)),
                  pl.BlockSpec((2, CMP_LEN, LANE), c3),
                  pl.BlockSpec((2, CMP_LEN, LANE, NSA_KV * CMP_HIDDEN), lambda bi, p, pt_: (0, 0, 0, 0)),
                  pl.BlockSpec((2, NSA_KV * CMP_HIDDEN, LANE), c3),
                  pl.BlockSpec((1, 128), c2), pl.BlockSpec((128, 128), c2),
                  pl.BlockSpec((NSA_HEADS, tq, m), c3),
                  pl.BlockSpec((NSA_HEADS, tq, kp), c3),
                  pl.BlockSpec((NSA_HEADS, tq, wp), c3),
                  pl.BlockSpec((m, LANE), c2),
                  pl.BlockSpec((LANE, kp), c2)],
        out_specs=pl.BlockSpec((1, tq, 512), per_b3),
        scratch_shapes=[pltpu.VMEM((kp, LANE), F32), pltpu.VMEM((kp, LANE), F32),
                        pltpu.VMEM((kp, KV_LANES), F32), pltpu.VMEM((wp, KV_LANES), F32)])
    kern = functools.partial(_nsa_sample_kernel, npages=npages, past=past, wb=wb, tq=tq, m=m, ns=ns)
    return pl.pallas_call(
        kern, grid_spec=grid_spec, name="nsa_sample",
        out_shape=jax.ShapeDtypeStruct((b, tq, 512), BF16),
        compiler_params=_cp("arbitrary", "arbitrary"),
    )(pt, pool_cmp, pool_slc, kv_new.reshape(b, tq, 768), win, qn.reshape(b, tq, 512),
      z_nsa.reshape(b, tq, NSA_PROJ_PAD), pe, w1, w2, kn, g128, bc, bs, bw, ov, ex)


def _nsa_paged_kernel(pt_ref, cmp_hbm, slc_hbm, new_ref, win_ref, q_ref, gt_ref, pe_ref, w1_ref, w2_ref,
                      kn_ref, g128_ref, bc_ref, bs_ref, bw_ref, ov_ref, ex_ref, o_ref,
                      cmpt_s, slct_s, cmpk_s, cmpv_s, sems, *, li, nb, npages, past, wb, tq, m, ns):
    b = pl.program_id(0)
    slot = b % 2

    def page_copy(bb, p, sl, which):
        hbm, buf = ((cmp_hbm, cmpt_s), (slc_hbm, slct_s))[which]
        return pltpu.make_async_copy(hbm.at[pt_ref[bb, p], li], buf.at[sl, :, pl.ds(p * PAGE, PAGE)],
                                     sems.at[sl, which])

    def start_batch(bb, sl):
        for p in range(npages):
            page_copy(bb, p, sl, 0).start()
            page_copy(bb, p, sl, 1).start()

    @pl.when(b == 0)
    def _():
        start_batch(0, 0)

    @pl.when(b + 1 < nb)
    def _():
        start_batch(b + 1, 1 - slot)

    for p in range(npages):
        page_copy(b, p, slot, 0).wait()
        page_copy(b, p, slot, 1).wait()

    hw = NSA_KV * HEAD_DIM
    for p in range(npages):
        blk = cmpt_s[slot, :, p * PAGE:(p + 1) * PAGE]
        cmpk_s[p * PAGE:(p + 1) * PAGE, :] = blk[0:hw].T
        cmpv_s[p * PAGE:(p + 1) * PAGE, :] = blk[hw:2 * hw].T
    cc = _compress_rows((cmpk_s, cmpv_s), m, pe_ref, w1_ref, w2_ref, kn_ref, g128_ref)

    new = new_ref[0]
    newt = jnp.concatenate([new, jnp.zeros((LANE - tq, 768), F32)], axis=0).T.astype(BF16)
    q = q_ref[0]
    gs = jax.nn.sigmoid(gt_ref[0])
    kp = past + LANE
    wp = wb + LANE
    qpos = past + lax.broadcasted_iota(jnp.int32, (tq, 1), 0)
    n_io = lax.broadcasted_iota(jnp.int32, (tq, m), 1)
    mask_c = ((n_io * CMP_STRIDE + (CMP_LEN - 1)) <= qpos) & (n_io < m - 1)
    k_io = lax.broadcasted_iota(jnp.int32, (tq, kp), 1)
    mask_s0 = (k_io <= qpos) & (k_io < past + tq)
    j_io = lax.broadcasted_iota(jnp.int32, (tq, wp), 1)
    dist_w = qpos - (past - wb + j_io)
    mask_w = (dist_w >= 0) & (dist_w < WINDOW) & (j_io < wb + tq)
    rep = lambda a_: jnp.concatenate([a_] * NSA_QPG, axis=0)
    mask_c4, mask_w4 = rep(mask_c), rep(mask_w)
    groups = range(NSA_KV)
    gsl = lambda g, off: slice(off + HEAD_DIM * g, off + HEAD_DIM * (g + 1))
    qgs = [_heads_to_rows(q, g) for g in groups]

    s_c = [_nt(qgs[g], cc[:, gsl(g, 0)].astype(BF16)) + bc_ref[g] for g in groups]
    p_c = [_msoftmax(s_c[g], mask_c4) for g in groups]
    o_c = [jnp.dot(p_c[g].astype(BF16), cc[:, gsl(g, hw)].astype(BF16), preferred_element_type=F32)
           for g in groups]
    imps = [jnp.dot(p_c[g][0:tq] + p_c[g][tq:2 * tq] + p_c[g][2 * tq:3 * tq] + p_c[g][3 * tq:4 * tq],
                    ov_ref[...], precision=HI, preferred_element_type=F32) for g in groups]
    sels = [_select_blocks(imps[g], qpos, ns).astype(BF16) for g in groups]
    selk = [jnp.dot(sels[g], ex_ref[...], preferred_element_type=F32) for g in groups]

    s_s = [jnp.concatenate(
        [jnp.dot(qgs[g], slct_s[slot, gsl(g, 0), :].astype(BF16), preferred_element_type=F32),
         jnp.dot(qgs[g], newt[gsl(g, 256)], preferred_element_type=F32)], axis=1) + bs_ref[g]
        for g in groups]
    p_s = [_msoftmax(s_s[g], rep((selk[g] > 0.5) & mask_s0)) for g in groups]
    o_s = [_nt(p_s[g][:, 0:past].astype(BF16), slct_s[slot, gsl(g, hw), :].astype(BF16))
           + _nt(p_s[g][:, past:kp].astype(BF16), newt[gsl(g, 256 + hw)]) for g in groups]

    s_w = [jnp.concatenate(
        [jnp.dot(qgs[g], win_ref[0, 0, gsl(g, 0), :].astype(BF16), preferred_element_type=F32),
         jnp.dot(qgs[g], newt[gsl(g, 512)], preferred_element_type=F32)], axis=1) + bw_ref[g]
        for g in groups]
    p_w = [_msoftmax(s_w[g], mask_w4) for g in groups]
    o_w = [_nt(p_w[g][:, 0:wb].astype(BF16), win_ref[0, 0, gsl(g, hw), :].astype(BF16))
           + _nt(p_w[g][:, wb:wp].astype(BF16), newt[gsl(g, 512 + hw)]) for g in groups]

    outs = []
    for g in groups:
        for h in range(NSA_QPG):
            hh = NSA_QPG * g + h
            rs = slice(tq * h, tq * (h + 1))
            outs.append(gs[:, 3 * hh:3 * hh + 1] * o_c[g][rs] + gs[:, 3 * hh + 1:3 * hh + 2] * o_s[g][rs]
                        + gs[:, 3 * hh + 2:3 * hh + 3] * o_w[g][rs])
    o_ref[0] = jnp.concatenate(outs, axis=1).astype(BF16)


def _nsa_paged(pt, pool_cmp_t, pool_slc_t, li, kv_new, win_t, qn, z_nsa, cw, bc, bs, bw, ov, ex, b, tq):
    npages = pt.shape[1]
    past = npages * PAGE
    wb = win_t.shape[3]
    assert past % CMP_STRIDE == 0 and tq < CMP_STRIDE
    kp = past + LANE
    wp = wb + LANE
    m = (past + tq) // CMP_STRIDE
    ns = -(-(past + tq) // SEL_LEN)
    rows = NSA_QPG * tq
    pe, w1, w2, kn, g128 = cw
    hw = NSA_KV * HEAD_DIM
    c2 = lambda bi, pt_: (0, 0)
    c3 = lambda bi, pt_: (0, 0, 0)
    per_b3 = lambda bi, pt_: (bi, 0, 0)
    grid_spec = pltpu.PrefetchScalarGridSpec(
        num_scalar_prefetch=1,
        grid=(b,),
        in_specs=[pl.BlockSpec(memory_space=pl.ANY),
                  pl.BlockSpec(memory_space=pl.ANY),
                  pl.BlockSpec((1, tq, 768), per_b3),
                  pl.BlockSpec((1, 1, KV_LANES, wb), lambda bi, pt_: (bi, li, 0, 0)),
                  pl.BlockSpec((1, tq, 512), per_b3),
                  pl.BlockSpec((1, tq, LANE), lambda bi, pt_: (bi, 0, GATE_COL_BLOCK)),
                  pl.BlockSpec((2, CMP_LEN, hw), c3),
                  pl.BlockSpec((2, CMP_LEN, hw, NSA_KV * CMP_HIDDEN), lambda bi, pt_: (0, 0, 0, 0)),
                  pl.BlockSpec((2, NSA_KV * CMP_HIDDEN, hw), c3),
                  pl.BlockSpec((1, 128), c2), pl.BlockSpec((128, 128), c2),
                  pl.BlockSpec((NSA_KV, rows, m), c3),
                  pl.BlockSpec((NSA_KV, rows, kp), c3),
                  pl.BlockSpec((NSA_KV, rows, wp), c3),
                  pl.BlockSpec((m, LANE), c2),
                  pl.BlockSpec((LANE, kp), c2)],
        out_specs=pl.BlockSpec((1, tq, 512), per_b3),
        scratch_shapes=[pltpu.VMEM((2, KV_LANES, past), F32), pltpu.VMEM((2, KV_LANES, past), F32),
                        pltpu.VMEM((past, hw), F32), pltpu.VMEM((past, hw), F32),
                        pltpu.SemaphoreType.DMA((2, 2))])
    kern = functools.partial(_nsa_paged_kernel, li=li, nb=b, npages=npages, past=past, wb=wb, tq=tq,
                             m=m, ns=ns)
    return pl.pallas_call(
        kern, grid_spec=grid_spec, name="nsa_paged",
        out_shape=jax.ShapeDtypeStruct((b, tq, 512), BF16),
        compiler_params=_cp("arbitrary"),
    )(pt, pool_cmp_t, pool_slc_t, kv_new.reshape(b, tq, 768), win_t, qn.reshape(b, tq, 512),
      z_nsa.reshape(b, tq, NSA_PROJ_PAD), pe, w1, w2, kn, g128, bc, bs, bw, ov, ex)


def _rwkv_kernel(z_ref, zp_ref, s0_ref, mu_ref, w0_ref, w2_ref, a0_ref, a2_ref, g2_ref, kkp_ref, ka_ref,
                 rk_ref, lnw_ref, lnb_ref, y_ref, st_ref, s_scr, carry_scr, *, c_len, nb):
    c = pl.program_id(1)
    cl = c_len
    nr = nb * cl

    @pl.when(c == 0)
    def _():
        s_scr[...] = s0_ref[...]
        carry_scr[...] = zp_ref[...]

    z3 = z_ref[...]
    row = lax.broadcasted_iota(jnp.int32, z3.shape, 1)
    shifted = jnp.where(row == 0, carry_scr[...], pltpu.roll(z3, 1, 1))
    carry_scr[...] = z3[:, cl - 1:cl, :]
    zz = (z3 + (shifted - z3) * mu_ref[...]).reshape(nr, RWKV_PROJ)
    w = RWKV_WIDTH
    r = zz[:, 0:w]
    k = zz[:, w:2 * w]
    v = zz[:, 2 * w:3 * w]
    zw = zz[:, 3 * w:3 * w + DECAY_LORA]
    za = zz[:, 3 * w + DECAY_LORA:3 * w + DECAY_LORA + AAA_LORA]
    zg = zz[:, 3 * w + DECAY_LORA + AAA_LORA:]
    xw = -(w0_ref[...] + _mmp(jnp.tanh(zw), w2_ref[...], _NN, RWKV_LORA_MM))
    softplus = jnp.maximum(xw, 0.0) + jnp.log(1.0 + jnp.exp(-jnp.abs(xw)))
    logdec = -jnp.exp(-softplus - 0.5)
    a = jax.nn.sigmoid(a0_ref[...] + _mmp(za, a2_ref[...], _NN, RWKV_LORA_MM))
    gate = _mmp(jax.nn.sigmoid(zg), g2_ref[...], _NN, RWKV_LORA_MM)
    k_mod = k * (1.0 + (a - 1.0) * ka_ref[...])
    kku = k * kkp_ref[...]

    ti = lax.broadcasted_iota(jnp.int32, (nr, nr), 0)
    si = lax.broadcasted_iota(jnp.int32, (nr, nr), 1)
    tri = ((si <= ti) & (si >= (ti // cl) * cl)).astype(BF16)
    ld_hi = logdec.astype(BF16)
    ld_r = logdec - ld_hi.astype(F32)
    ld_mid = ld_r.astype(BF16)
    ld_lo = (ld_r - ld_mid.astype(F32)).astype(BF16)
    gcum = (jnp.dot(tri, ld_hi, preferred_element_type=F32)
            + (jnp.dot(tri, ld_mid, preferred_element_type=F32)
               + jnp.dot(tri, ld_lo, preferred_element_type=F32)))
    e_g_all = jnp.exp(gcum)
    e_gm_all = jnp.exp(gcum - logdec)
    e_ng_all = jnp.exp(-gcum)
    nsteps = max(1, int(math.ceil(math.log2(cl))))
    row2 = lax.broadcasted_iota(jnp.int32, (cl, 2 * cl), 0)
    col2 = lax.broadcasted_iota(jnp.int32, (cl, 2 * cl), 1)
    col2 = jnp.where(col2 >= cl, col2 - cl, col2)
    strict2 = col2 < row2
    lower2 = col2 <= row2
    keep_z = lax.broadcasted_iota(jnp.int32, (cl, cl + HEAD_DIM), 1) >= cl

    pairs = [(bb, hd) for bb in range(nb) for hd in range(RWKV_HEADS)]
    heads = range(len(pairs))
    rws = [slice(cl * bb, cl * (bb + 1)) for bb, _ in pairs]
    sls = [slice(HEAD_DIM * hd, HEAD_DIM * (hd + 1)) for _, hd in pairs]
    kks = [kku[rws[h], sls[h]] for h in heads]
    kks = [kk * lax.rsqrt(jnp.maximum(jnp.sum(kk * kk, axis=-1, keepdims=True), 1e-24)) for kk in kks]
    kkas = [kks[h] * a[rws[h], sls[h]] for h in heads]
    kms = [k_mod[rws[h], sls[h]] for h in heads]
    vhs = [v[rws[h], sls[h]] for h in heads]
    rhs_ = [r[rws[h], sls[h]] for h in heads]
    s_hs = [s_scr[bb, hd] for bb, hd in pairs]
    e_g = [e_g_all[rws[h], sls[h]] for h in heads]
    e_gm = [e_gm_all[rws[h], sls[h]] for h in heads]
    e_ng = [e_ng_all[rws[h], sls[h]] for h in heads]
    glast = [gcum[cl * (bb + 1) - 1:cl * (bb + 1), sls[h]] for h, (bb, _) in enumerate(pairs)]
    e_lg = [jnp.exp(glast[h] - gcum[rws[h], sls[h]]) for h in heads]
    e_l = [jnp.exp(glast[h]) for h in heads]
    lrs = [jnp.concatenate([-kks[h] * e_gm[h], rhs_[h] * e_g[h]], axis=0) for h in heads]
    rrs = [jnp.concatenate([kkas[h] * e_ng[h], kms[h] * e_ng[h]], axis=0) for h in heads]
    m1s = [_mmp(lrs[h], rrs[h], _NT) for h in heads]
    m2s = [_mmp(lrs[h], s_hs[h], _NT) for h in heads]
    tops = [jnp.where(strict2, m1[0:cl], 0.0) for m1 in m1s]
    bots = [jnp.where(lower2, m1[cl:2 * cl], 0.0) for m1 in m1s]
    zeros_v = jnp.zeros((cl, HEAD_DIM), F32)
    rhss = [m2s[h][0:cl] + _mmp(tops[h], jnp.concatenate([zeros_v, vhs[h]], axis=0), _NN) for h in heads]
    wms = [jnp.concatenate([tops[h][:, 0:cl], rhss[h]], axis=1) for h in heads]
    for _ in range(nsteps):
        wms = [_mmp(w_[:, 0:cl], w_, _NN) + jnp.where(keep_z, w_, 0.0) for w_ in wms]
    zvs = [jnp.concatenate([wms[h][:, cl:cl + HEAD_DIM], vhs[h]], axis=0) for h in heads]
    ys = [m2s[h][cl:2 * cl] + _mmp(bots[h], zvs[h], _NN) for h in heads]
    bkhs = [jnp.concatenate([kkas[h] * e_lg[h], kms[h] * e_lg[h]], axis=0) for h in heads]
    s_new = [s_hs[h] * e_l[h] + _mmp(zvs[h], bkhs[h], _TN) for h in heads]
    for h, (bb, hd) in enumerate(pairs):
        s_scr[bb, hd] = s_new[h]
    outs = []
    for h in heads:
        y = ys[h]
        sl = sls[h]
        mean = jnp.mean(y, axis=-1, keepdims=True)
        yc = y - mean
        var = jnp.mean(yc * yc, axis=-1, keepdims=True)
        yn = yc * lax.rsqrt(var + GN_EPS) * lnw_ref[:, sl] + lnb_ref[:, sl]
        bonus = jnp.sum(rhs_[h] * kms[h] * rk_ref[:, sl], axis=-1, keepdims=True) * vhs[h]
        outs.append(yn + bonus)
    for bb in range(nb):
        yb = jnp.concatenate(outs[RWKV_HEADS * bb:RWKV_HEADS * (bb + 1)], axis=1)
        y_ref[bb] = (yb * gate[cl * bb:cl * (bb + 1)]).astype(BF16)
    st_ref[...] = s_scr[...]


def _rwkv(z_rw, z_prev, s0, pr, b, t_len, c_len, nb):
    nchunk = t_len // c_len
    c2 = lambda bi, c: (0, 0)
    vec = lambda n: pl.BlockSpec((1, n), c2)
    return pl.pallas_call(
        functools.partial(_rwkv_kernel, c_len=c_len, nb=nb), name="rwkv",
        grid=(b // nb, nchunk),
        in_specs=[pl.BlockSpec((nb, c_len, RWKV_PROJ), lambda bi, c: (bi, c, 0)),
                  pl.BlockSpec((nb, 1, RWKV_PROJ), lambda bi, c: (bi, 0, 0)),
                  pl.BlockSpec((nb, RWKV_HEADS, HEAD_DIM, HEAD_DIM), lambda bi, c: (bi, 0, 0, 0)),
                  vec(RWKV_PROJ), vec(512), pl.BlockSpec((DECAY_LORA, 512), c2),
                  vec(512), pl.BlockSpec((AAA_LORA, 512), c2), pl.BlockSpec((GATE_LORA, 512), c2),
                  vec(512), vec(512), vec(512), vec(512), vec(512)],
        out_specs=[pl.BlockSpec((nb, c_len, 512), lambda bi, c: (bi, c, 0)),
                   pl.BlockSpec((nb, RWKV_HEADS, HEAD_DIM, HEAD_DIM), lambda bi, c: (bi, 0, 0, 0))],
        out_shape=[jax.ShapeDtypeStruct((b, t_len, 512), BF16),
                   jax.ShapeDtypeStruct((b, RWKV_HEADS, HEAD_DIM, HEAD_DIM), F32)],
        scratch_shapes=[pltpu.VMEM((nb, RWKV_HEADS, HEAD_DIM, HEAD_DIM), F32),
                        pltpu.VMEM((nb, 1, RWKV_PROJ), F32)],
        compiler_params=_cp("arbitrary", "arbitrary"),
    )(z_rw.reshape(b, t_len, RWKV_PROJ), z_prev.reshape(b, 1, RWKV_PROJ), s0, *pr)


def _rel_bucket(dist):
    n = jnp.maximum(dist, 0)
    max_exact = NUM_BUCKETS // 2
    nf = jnp.maximum(n, 1).astype(F32)
    large = max_exact + (jnp.log(nf / max_exact) / math.log(MAX_DISTANCE / max_exact)
                         * (NUM_BUCKETS - max_exact)).astype(jnp.int32)
    large = jnp.minimum(large, NUM_BUCKETS - 1)
    return jnp.where(n < max_exact, n, large)


def _bias_table(rel_bias, dist, delta):
    tb = rel_bias.astype(F32)
    out = jnp.moveaxis(tb[_rel_bucket(dist)], -1, 0)
    if delta:
        out = out - tb[NUM_BUCKETS - 1].reshape((NSA_HEADS,) + (1,) * dist.ndim)
    return out


def _block_diag_ones(n, grp):
    idx = jnp.arange(n) // grp
    return (idx[:, None] == idx[None, :]).astype(F32) / grp


def _compress_weights(phi_pe, phi_w1, phi_w2, kn_cmp):
    pe = jnp.concatenate([phi_pe, phi_pe], axis=-1)
    w1 = jnp.zeros((2, CMP_LEN, NSA_KV * HEAD_DIM, NSA_KV * CMP_HIDDEN), F32)
    w2 = jnp.zeros((2, NSA_KV * CMP_HIDDEN, NSA_KV * HEAD_DIM), F32)
    for g in range(NSA_KV):
        w1 = w1.at[:, :, HEAD_DIM * g:HEAD_DIM * (g + 1), CMP_HIDDEN * g:CMP_HIDDEN * (g + 1)].set(phi_w1)
        w2 = w2.at[:, CMP_HIDDEN * g:CMP_HIDDEN * (g + 1), HEAD_DIM * g:HEAD_DIM * (g + 1)].set(phi_w2)
    kn = jnp.tile(kn_cmp, NSA_KV).reshape(1, 128)
    return pe, w1.astype(BF16), w2.astype(BF16), kn


def _overlap_expand(t_all):
    nc = t_all // CMP_STRIDE - 1
    m = nc + 1
    ns = -(-t_all // SEL_LEN)
    c0 = jnp.arange(m)[:, None] * CMP_STRIDE
    s0 = jnp.arange(LANE)[None, :] * SEL_LEN
    ov = ((c0 < s0 + SEL_LEN) & (c0 + CMP_LEN > s0) & (jnp.arange(m)[:, None] < nc)
          & (jnp.arange(LANE)[None, :] < ns)).astype(F32)
    kp = _round_up(t_all, LANE)
    ex = (jnp.arange(kp)[None, :] // SEL_LEN == jnp.arange(LANE)[:, None]).astype(BF16)
    return ov, ex


def _pick_tm(t_len):
    return 512 if t_len % 512 == 0 else 128


def _trunk(x3, prm, st):
    b, t_len, d = x3.shape
    m_rows = b * t_len
    x = x3.reshape(m_rows, d)
    prompt = st is None
    if prompt:
        tm = _pick_tm(t_len)
        c_len = 64
    else:
        tm = 256 if m_rows % 256 == 0 else m_rows
        c_len = t_len
    tm_f = tm
    g512 = _block_diag_ones(512, HEAD_DIM)
    g128 = _block_diag_ones(128, HEAD_DIM)
    rel_bias = prm["rel_bias"]

    if prompt:
        ov, ex = _overlap_expand(t_len)
        nsp = _round_up(-(-t_len // SEL_LEN), 8)
        ovt = ov[:, :nsp].T
        ext = ex[:nsp, :t_len].T
        kk_ = jnp.arange(KT)[:, None]
        qq_ = jnp.arange(KT)[None, :]

        def band(dist):
            rows = dist.shape[0]
            t_ = _bias_table(rel_bias, dist, True).reshape(NSA_KV, NSA_QPG, rows, KT)
            return jnp.transpose(t_, (0, 2, 1, 3)).reshape(NSA_KV, rows, NSA_QPG * KT)

        tbt = jnp.stack([band(KT + qq_ - kk_), band(qq_ - kk_)], axis=1)
        cj = jnp.arange(2 * KT // CMP_STRIDE)[:, None] * CMP_STRIDE + (CMP_LEN - 1)
        bct = jnp.stack([band(KT + qq_ - cj), band(qq_ - cj)], axis=1)
    else:
        pt = st["page_table"]
        past = pt.shape[1] * PAGE
        wb = st["win"].shape[2]
        t_all = past + t_len
        ov, ex = _overlap_expand(t_all)
        mcmp = t_all // CMP_STRIDE
        kp = past + LANE
        wp = wb + LANE
        qp = past + jnp.arange(t_len)[:, None]
        cend = jnp.arange(mcmp)[None, :] * CMP_STRIDE + (CMP_LEN - 1)
        rows_ = NSA_QPG * t_len
        bc = _bias_table(rel_bias, qp - cend, False).reshape(NSA_KV, rows_, mcmp)
        bs = _bias_table(rel_bias, qp - jnp.arange(kp)[None, :], True).reshape(NSA_KV, rows_, kp)
        bw = _bias_table(rel_bias, qp - (past - wb + jnp.arange(wp)[None, :]), True).reshape(NSA_KV, rows_, wp)
        pool_cmp = jnp.swapaxes(st["cmp"].reshape(st["cmp"].shape[0], st["cmp"].shape[1], PAGE, KV_LANES), 2, 3)
        pool_slc = jnp.swapaxes(st["slc"].reshape(st["slc"].shape[0], st["slc"].shape[1], PAGE, KV_LANES), 2, 3)
        win_all = st["win"].reshape(b, st["win"].shape[1], wb, KV_LANES)
        win_all_t = jnp.swapaxes(win_all, 2, 3)

    cmp_r, slc_r, win_r, wkv_r, sh_r, conv_r, ffn_r = [], [], [], [], [], [], []
    for l in range(4):
        li = l // 2
        if l % 2 == 0:
            w_in = prm["w_in_even"][li].T
            w_nsa = jnp.pad(w_in[:NSA_PROJ], ((0, NSA_PROJ_PAD - NSA_PROJ), (0, 0))).astype(BF16)
            w_rw = w_in[NSA_PROJ:].astype(BF16)
            z_nsa = _norm_matmul(x, prm["norm_mix"][l], w_nsa, tm, NSA_PROJ_PAD)
            z_rw = _norm_matmul(x, prm["norm_mix"][l], w_rw, tm, RWKV_PROJ // 2)
            qg = jnp.tile(prm["q_norm"][li], NSA_HEADS).reshape(1, 512)
            kg = jnp.stack([jnp.tile(prm["k_norm"][li, 1], NSA_KV), jnp.tile(prm["k_norm"][li, 2], NSA_KV)])
            qn, kv3 = _nsa_prep(z_nsa, qg, kg, g512, g128, tm)
            cw = _compress_weights(prm["phi_pe"][li], prm["phi_w1"][li], prm["phi_w2"][li],
                                   prm["k_norm"][li, 0])
            kv3b = kv3.reshape(b, t_len, 768)
            if prompt:
                cc = _compress(kv3b, *cw, g128)
                nsa = _nsa_prompt(qn, z_nsa, cc, kv3, bct, tbt, ovt, ext, b, t_len)
                z_prev = jnp.zeros((b, RWKV_PROJ), F32)
                s0 = jnp.zeros((b, RWKV_HEADS, HEAD_DIM, HEAD_DIM), F32)
                n_keep = min(WINDOW, t_len)
                win_rows = kv3b[:, t_len - n_keep:, 512:768]
            else:
                nsa = _nsa_paged(pt, pool_cmp, pool_slc, li, kv3, win_all_t, qn, z_nsa, cw + (g128,),
                                 bc, bs, bw, ov, ex, b, t_len)
                z_prev = st["shift"][:, li]
                s0 = st["wkv"][:, li]
                n_keep = min(WINDOW, wb + t_len)
                win_rows = jnp.concatenate([win_all[:, li], kv3b[:, :, 512:768]], axis=1)[:, -n_keep:]
            pr = (prm["rwkv_mu"][li].reshape(1, -1), prm["rwkv_w0"][li].reshape(1, -1), prm["rwkv_w2"][li],
                  prm["rwkv_a0"][li].reshape(1, -1), prm["rwkv_a2"][li], prm["rwkv_g2"][li],
                  prm["rwkv_kk"][li].reshape(1, -1), prm["rwkv_ka"][li].reshape(1, -1),
                  prm["rwkv_rk"][li].reshape(1, -1), prm["rwkv_ln_w"][li].reshape(1, -1),
                  prm["rwkv_ln_b"][li].reshape(1, -1))
            nb_rw = max(n_ for n_ in ((2, 1) if prompt else (8, 4, 2, 1)) if b % n_ == 0)
            rw, s_t = _rwkv(z_rw, z_prev, s0, pr, b, t_len, c_len, nb_rw)
            w_out = prm["w_out_even"][li].astype(BF16)
            if prompt:
                x = _matmul_res_t(nsa, rw.reshape(m_rows, 512), w_out, x, tm, 512)
            else:
                mix = jnp.concatenate([nsa.reshape(m_rows, 512), rw.reshape(m_rows, 512)], axis=1)
                x = _matmul_res(mix, w_out, x, tm, 512)
            cmp_r.append(kv3b[:, :, 0:256].reshape(b, t_len, 2, NSA_KV, HEAD_DIM))
            slc_r.append(kv3b[:, :, 256:512].reshape(b, t_len, 2, NSA_KV, HEAD_DIM))
            win_r.append(win_rows.reshape(b, n_keep, 2, NSA_KV, HEAD_DIM))
            wkv_r.append(s_t)
            sh_r.append(z_rw.reshape(b, t_len, RWKV_PROJ)[:, -1])
        else:
            prev = jnp.zeros((b, 2, D_MODEL), F32) if prompt else st["conv"][:, li]
            x, cs = _fused_block(x, prm["norm_mix"][l], prm["w_in_odd"][li].astype(BF16),
                                 prm["w_out_odd"][li].astype(BF16), prm["conv_w"][li], prev, "odd", t_len, tm_f)
            conv_r.append(cs)
        prev = jnp.zeros((b, 2, D_FF), F32) if prompt else st["ffn"][:, l]
        x, fs = _fused_block(x, prm["norm_ffn"][l], prm["ffn_up"][l].astype(BF16),
                             prm["ffn_down"][l].astype(BF16), prm["ffn_conv"][l], prev, "ffn", t_len, tm_f)
        ffn_r.append(fs)
    stk = lambda a_: jnp.stack(a_, axis=1)
    return x.reshape(b, t_len, d), (stk(cmp_r), stk(slc_r), stk(win_r), stk(wkv_r), stk(sh_r),
                                    stk(conv_r), stk(ffn_r))


def kernel(x_prompt, x_sample, cache_cmp_kv, cache_slc_kv, cache_win_kv, state_rwkv_wkv, state_rwkv_shift, state_conv, state_ffn_conv, page_table, norm_mix, norm_ffn, rel_bias, w_in_even, w_out_even, q_norm, k_norm, phi_pe, phi_w1, phi_w2, rwkv_mu, rwkv_w0, rwkv_w2, rwkv_a0, rwkv_a2, rwkv_g2, rwkv_kk, rwkv_ka, rwkv_rk, rwkv_ln_w, rwkv_ln_b, w_in_odd, conv_w, w_out_odd, ffn_up, ffn_conv, ffn_down):
    prm = dict(norm_mix=norm_mix, norm_ffn=norm_ffn, rel_bias=rel_bias, w_in_even=w_in_even,
               w_out_even=w_out_even, q_norm=q_norm, k_norm=k_norm, phi_pe=phi_pe, phi_w1=phi_w1,
               phi_w2=phi_w2, rwkv_mu=rwkv_mu, rwkv_w0=rwkv_w0, rwkv_w2=rwkv_w2, rwkv_a0=rwkv_a0,
               rwkv_a2=rwkv_a2, rwkv_g2=rwkv_g2, rwkv_kk=rwkv_kk, rwkv_ka=rwkv_ka, rwkv_rk=rwkv_rk,
               rwkv_ln_w=rwkv_ln_w, rwkv_ln_b=rwkv_ln_b, w_in_odd=w_in_odd, conv_w=conv_w,
               w_out_odd=w_out_odd, ffn_up=ffn_up, ffn_conv=ffn_conv, ffn_down=ffn_down)
    st = dict(cmp=cache_cmp_kv, slc=cache_slc_kv, win=cache_win_kv, wkv=state_rwkv_wkv,
              shift=state_rwkv_shift, conv=state_conv, ffn=state_ffn_conv, page_table=page_table)
    y_p, (cmp_p, slc_p, win_p, wkv_p, sh_p, conv_p, ffn_p) = _trunk(x_prompt, prm, None)
    y_s, (cmp_s, slc_s, win_s, wkv_s, sh_s, conv_s, ffn_s) = _trunk(x_sample, prm, st)
    return (y_p, y_s, cmp_p, cmp_s, slc_p, slc_s, win_p, win_s, wkv_p, wkv_s,
            sh_p, sh_s, conv_p, conv_s, ffn_p, ffn_s)
```

```python
import functools
import math

import jax
import jax.numpy as jnp
from jax import lax
from jax.experimental import pallas as pl
from jax.experimental.pallas import tpu as pltpu

F32 = jnp.float32
BF16 = jnp.bfloat16
HI = lax.Precision.HIGHEST

D_MODEL = 1024
HEAD_DIM = 64
NSA_HEADS = 8
NSA_KV = 2
NSA_QPG = 4
NSA_WIDTH = 512
KV_LANES = 2 * NSA_KV * HEAD_DIM
CMP_LEN = 32
CMP_STRIDE = 16
CMP_HIDDEN = 128
SEL_LEN = 64
N_SELECT = 16
WINDOW = 512
FORCE_SCORE = 1e9
RWKV_HEADS = 8
RWKV_WIDTH = 512
DECAY_LORA = 64
AAA_LORA = 64
GATE_LORA = 128
RWKV_PROJ = 3 * RWKV_WIDTH + DECAY_LORA + AAA_LORA + GATE_LORA
NSA_PROJ = NSA_WIDTH + 6 * NSA_KV * HEAD_DIM + 3 * NSA_HEADS
NSA_PROJ_PAD = 1408
GATE_COL_BLOCK = 10
D_FF = 2816
NUM_BUCKETS = 32
MAX_DISTANCE = 128
RMS_EPS = 1e-6
GN_EPS = 64e-5
PAGE = 128
LANE = 128
KT = 128
NEG = -1e30
VMEM_LIMIT = 56 * 1024 * 1024


def _cp(*sem):
    return pltpu.CompilerParams(dimension_semantics=sem, vmem_limit_bytes=VMEM_LIMIT)


def _round_up(a, b):
    return (a + b - 1) // b * b


def _nt(a, b, precision=None):
    return lax.dot_general(a, b, (((1,), (1,)), ((), ())), precision=precision,
                           preferred_element_type=F32)


def _tn(a, b, precision=None):
    return lax.dot_general(a, b, (((0,), (0,)), ((), ())), precision=precision,
                           preferred_element_type=F32)


_NN = ((1,), (0,))
_NT = ((1,), (1,))
_TN = ((0,), (0,))
RWKV_MM = "bf16"
RWKV_LORA_MM = "bf16"


def _mmp(a, b, dims, mode=None):
    mode = RWKV_MM if mode is None else mode
    dn = (dims, ((), ()))
    if mode == "f32":
        return lax.dot_general(a, b, dn, precision=HI, preferred_element_type=F32)
    if mode == "bf16":
        return lax.dot_general(a.astype(BF16), b.astype(BF16), dn, preferred_element_type=F32)
    ah = a.astype(BF16)
    al = (a - ah.astype(F32)).astype(BF16)
    bh = b.astype(BF16)
    bl = (b - bh.astype(F32)).astype(BF16)
    dot = lambda x, y: lax.dot_general(x, y, dn, preferred_element_type=F32)
    return dot(ah, bh) + (dot(ah, bl) + dot(al, bh))


def _norm_mm_kernel(x_ref, g_ref, w_ref, o_ref, xn_ref):
    @pl.when(pl.program_id(1) == 0)
    def _():
        x = x_ref[...]
        ms = jnp.mean(x * x, axis=-1, keepdims=True)
        xn_ref[...] = (x * lax.rsqrt(ms + RMS_EPS) * g_ref[...]).astype(BF16)

    o_ref[...] = _nt(xn_ref[...], w_ref[...])


def _norm_matmul(x, g, w_t, tm, tn):
    m, k = x.shape
    n = w_t.shape[0]
    return pl.pallas_call(
        _norm_mm_kernel, name="norm_mm",
        grid=(m // tm, n // tn),
        in_specs=[pl.BlockSpec((tm, k), lambda i, j: (i, 0)),
                  pl.BlockSpec((1, k), lambda i, j: (0, 0)),
                  pl.BlockSpec((tn, k), lambda i, j: (j, 0))],
        out_specs=pl.BlockSpec((tm, tn), lambda i, j: (i, j)),
        out_shape=jax.ShapeDtypeStruct((m, n), F32),
        scratch_shapes=[pltpu.VMEM((tm, k), BF16)],
        compiler_params=_cp("arbitrary", "arbitrary"),
    )(x, g.reshape(1, k), w_t)


def _mm_res_kernel(a_ref, w_ref, r_ref, o_ref):
    o_ref[...] = r_ref[...] + jnp.dot(a_ref[...], w_ref[...], preferred_element_type=F32)


def _matmul_res(a, w, res, tm, tn):
    m, k = a.shape
    n = w.shape[1]
    return pl.pallas_call(
        _mm_res_kernel, name="mm_res",
        grid=(m // tm, n // tn),
        in_specs=[pl.BlockSpec((tm, k), lambda i, j: (i, 0)),
                  pl.BlockSpec((k, tn), lambda i, j: (0, j)),
                  pl.BlockSpec((tm, tn), lambda i, j: (i, j))],
        out_specs=pl.BlockSpec((tm, tn), lambda i, j: (i, j)),
        out_shape=jax.ShapeDtypeStruct((m, n), F32),
        compiler_params=_cp("arbitrary", "arbitrary"),
    )(a, w, res)


def _mm_res_t_kernel(at_ref, b_ref, wa_ref, wb_ref, r_ref, o_ref):
    o_ref[...] = (r_ref[...] + _tn(at_ref[0], wa_ref[...])
                  + jnp.dot(b_ref[...], wb_ref[...], preferred_element_type=F32))


def _matmul_res_t(a_t, bmat, w, res, tm, tn):
    nb, ka, t_len = a_t.shape
    m, kb = bmat.shape
    n = w.shape[1]
    tpb = t_len // tm
    return pl.pallas_call(
        _mm_res_t_kernel, name="mm_res_t",
        grid=(m // tm, n // tn),
        in_specs=[pl.BlockSpec((1, ka, tm), lambda i, j: (i // tpb, 0, i % tpb)),
                  pl.BlockSpec((tm, kb), lambda i, j: (i, 0)),
                  pl.BlockSpec((ka, tn), lambda i, j: (0, j)),
                  pl.BlockSpec((kb, tn), lambda i, j: (ka // kb, j)),
                  pl.BlockSpec((tm, tn), lambda i, j: (i, j))],
        out_specs=pl.BlockSpec((tm, tn), lambda i, j: (i, j)),
        out_shape=jax.ShapeDtypeStruct((m, n), F32),
        compiler_params=_cp("arbitrary", "arbitrary"),
    )(a_t, bmat, w, w, res)


FUSED_CHUNK = 256


def _fused_kernel(x_ref, g_ref, win_ref, wout_ref, cw_ref, prev_ref, o_ref, st_ref, carry_ref,
                  *, mode, nbr, seq_tiles, nseq, lt):
    i = pl.program_id(0)
    x = x_ref[...]
    ms = jnp.mean(x * x, axis=-1, keepdims=True)
    xn = (x * lax.rsqrt(ms + RMS_EPS) * g_ref[...]).astype(BF16)
    f = cw_ref.shape[-1]
    ch = FUSED_CHUNK
    nchunk = f // ch
    if seq_tiles == 1:
        p = prev_ref[...]
    else:
        p = jnp.where(i % seq_tiles == 0, prev_ref[...], carry_ref[...])
    t = lax.broadcasted_iota(jnp.int32, (nseq, lt, ch), 1)

    def project(c):
        return [jnp.dot(xn, win_ref[:, r * f + c * ch:r * f + (c + 1) * ch], preferred_element_type=F32)
                for r in range(nbr)]

    def gate(c, br):
        cs = slice(c * ch, (c + 1) * ch)
        if mode == "ffn":
            cin, other = br
        else:
            other, cg, xi = br
            cin = cg * xi
        c3 = cin.reshape(nseq, lt, ch)
        p0 = p[:, 0:1, cs]
        p1 = p[:, 1:2, cs]
        c1 = jnp.where(t == 0, p1, pltpu.roll(c3, 1, 1))
        c2 = jnp.where(t == 0, p0, jnp.where(t == 1, p1, pltpu.roll(c3, 2, 1)))
        y = c2 * cw_ref[0:1, cs] + c1 * cw_ref[1:2, cs] + c3 * cw_ref[2:3, cs]
        last2 = c3[:, lt - 2:lt, :]
        st_ref[:, :, cs] = last2
        if seq_tiles > 1:
            carry_ref[:, :, cs] = last2
        o3 = other.reshape(c3.shape)
        out = (y * jax.nn.sigmoid(y)) * o3 if mode == "ffn" else o3 * y
        return out.reshape(nseq * lt, ch).astype(BF16)

    acc = jnp.zeros(x.shape, F32)
    br = project(0)
    for c in range(nchunk):
        br_next = project(c + 1) if c + 1 < nchunk else None
        gch = gate(c, br)
        acc = acc + jnp.dot(gch, wout_ref[c * ch:(c + 1) * ch, :], preferred_element_type=F32)
        br = br_next
    o_ref[...] = x + acc


def _fused_block(x, g, w_in, w_out, cw, prev, mode, seq_len, tm):
    m, d = x.shape
    nbr = 2 if mode == "ffn" else 3
    f = w_in.shape[1] // nbr
    if seq_len >= tm:
        seq_tiles, nseq, lt = seq_len // tm, 1, tm
    else:
        seq_tiles, nseq, lt = 1, tm // seq_len, seq_len
    b = prev.shape[0]
    kern = functools.partial(_fused_kernel, mode=mode, nbr=nbr, seq_tiles=seq_tiles, nseq=nseq, lt=lt)
    return pl.pallas_call(
        kern, name="fused_" + mode,
        grid=(m // tm,),
        in_specs=[pl.BlockSpec((tm, d), lambda i: (i, 0)),
                  pl.BlockSpec((1, d), lambda i: (0, 0)),
                  pl.BlockSpec((d, nbr * f), lambda i: (0, 0), pipeline_mode=pl.Buffered(1)),
                  pl.BlockSpec((f, d), lambda i: (0, 0), pipeline_mode=pl.Buffered(1)),
                  pl.BlockSpec((3, f), lambda i: (0, 0)),
                  pl.BlockSpec((nseq, 2, f), lambda i: (i // seq_tiles, 0, 0))],
        out_specs=[pl.BlockSpec((tm, d), lambda i: (i, 0)),
                   pl.BlockSpec((nseq, 2, f), lambda i: (i // seq_tiles, 0, 0))],
        out_shape=[jax.ShapeDtypeStruct((m, d), F32),
                   jax.ShapeDtypeStruct((b, 2, f), F32)],
        scratch_shapes=[pltpu.VMEM((nseq, 2, f), F32)],
        compiler_params=_cp("arbitrary"),
    )(x, g.reshape(1, d), w_in, w_out, cw, prev)


def _gated_kernel(x_ref, g_ref, *rest, mode, nbr, seq_tiles, nseq, lt):
    w_refs = rest[:nbr]
    cw_ref, prev_ref, o_ref, st_ref, xn_ref, carry_ref = rest[nbr:]
    i = pl.program_id(0)
    j = pl.program_id(1)

    @pl.when(j == 0)
    def _():
        x = x_ref[...]
        ms = jnp.mean(x * x, axis=-1, keepdims=True)
        xn_ref[...] = (x * lax.rsqrt(ms + RMS_EPS) * g_ref[...]).astype(BF16)

    xn = xn_ref[...]
    br = [jnp.dot(xn, w[...], preferred_element_type=F32) for w in w_refs]
    if mode == "ffn":
        cin, other = br
    else:
        other, cg, xi = br
        cin = cg * xi
    tf = cin.shape[-1]
    c = cin.reshape(nseq, lt, tf)
    if seq_tiles == 1:
        p = prev_ref[...]
    else:
        p = jnp.where(i % seq_tiles == 0, prev_ref[...], carry_ref[j])
    p0 = p[:, 0:1, :]
    p1 = p[:, 1:2, :]
    t = lax.broadcasted_iota(jnp.int32, c.shape, 1)
    c1 = jnp.where(t == 0, p1, pltpu.roll(c, 1, 1))
    c2 = jnp.where(t == 0, p0, jnp.where(t == 1, p1, pltpu.roll(c, 2, 1)))
    cw = cw_ref[...]
    y = c2 * cw[0:1, :] + c1 * cw[1:2, :] + c * cw[2:3, :]
    last2 = c[:, lt - 2:lt, :]
    for jj in range(st_ref.shape[-1] // tf):
        @pl.when(j == jj)
        def _(jj=jj):
            st_ref[:, :, jj * tf:(jj + 1) * tf] = last2
    if seq_tiles > 1:
        carry_ref[j] = last2
    o3 = other.reshape(c.shape)
    if mode == "ffn":
        out = (y * jax.nn.sigmoid(y)) * o3
    else:
        out = o3 * y
    o_ref[...] = out.reshape(nseq * lt, tf).astype(BF16)


def _gated(x, g, w, cw, prev, mode, seq_len, tm, tf):
    m, k = x.shape
    nbr = 2 if mode == "ffn" else 3
    f = w.shape[1] // nbr
    nj = f // tf
    if seq_len >= tm:
        seq_tiles, nseq, lt = seq_len // tm, 1, tm
    else:
        seq_tiles, nseq, lt = 1, tm // seq_len, seq_len
    b = prev.shape[0]
    w_specs = [pl.BlockSpec((k, tf), functools.partial(lambda i, j, o: (0, j + o), o=r * nj))
               for r in range(nbr)]
    kern = functools.partial(_gated_kernel, mode=mode, nbr=nbr, seq_tiles=seq_tiles, nseq=nseq, lt=lt)
    return pl.pallas_call(
        kern, name="gated_" + mode,
        grid=(m // tm, nj),
        in_specs=[pl.BlockSpec((tm, k), lambda i, j: (i, 0)),
                  pl.BlockSpec((1, k), lambda i, j: (0, 0))] + w_specs + [
                  pl.BlockSpec((3, tf), lambda i, j: (0, j)),
                  pl.BlockSpec((nseq, 2, tf), lambda i, j: (i // seq_tiles, 0, j))],
        out_specs=[pl.BlockSpec((tm, tf), lambda i, j: (i, j)),
                   pl.BlockSpec((nseq, 2, f), lambda i, j: (i // seq_tiles, 0, 0))],
        out_shape=[jax.ShapeDtypeStruct((m, f), BF16),
                   jax.ShapeDtypeStruct((b, 2, f), F32)],
        scratch_shapes=[pltpu.VMEM((tm, k), BF16),
                        pltpu.VMEM((nj, nseq, 2, tf), F32)],
        compiler_params=_cp("arbitrary", "arbitrary"),
    )(x, g.reshape(1, k), *([w] * nbr), cw, prev)


def _group_mean(x2, g_ref):
    hi = x2.astype(BF16)
    r1 = x2 - hi.astype(F32)
    mid = r1.astype(BF16)
    lo = (r1 - mid.astype(F32)).astype(BF16)
    g = g_ref[...]
    return (jnp.dot(hi, g, preferred_element_type=F32)
            + (jnp.dot(mid, g, preferred_element_type=F32) + jnp.dot(lo, g, preferred_element_type=F32)))


def _even_in_kernel(x_ref, g_ref, wn_ref, wr_ref, qg_ref, kg_ref, g512_ref, g128_ref,
                    q_ref, kv_ref, gt_ref, zrw_ref):
    x = x_ref[...]
    ms = jnp.mean(x * x, axis=-1, keepdims=True)
    xn = (x * lax.rsqrt(ms + RMS_EPS) * g_ref[...]).astype(BF16)
    zrw_ref[...] = _nt(xn, wr_ref[...])
    z = _nt(xn, wn_ref[...])
    q = z[:, 0:NSA_WIDTH]
    q_ref[...] = (q * lax.rsqrt(_group_mean(q * q, g512_ref) + RMS_EPS) * qg_ref[...]
                  * (HEAD_DIM ** -0.5)).astype(BF16)
    kv_ref[:, 0:256] = z[:, 512:768]
    for r, off in ((0, 768), (1, 1024)):
        k = z[:, off:off + 128]
        kv_ref[:, off - 512:off - 384] = (k * lax.rsqrt(_group_mean(k * k, g128_ref) + RMS_EPS)
                                          * kg_ref[r:r + 1, :])
        kv_ref[:, off - 384:off - 256] = z[:, off + 128:off + 256]
    gt_ref[...] = z[:, 1280:NSA_PROJ_PAD]


def _even_in(x, g, wn_t, wr_t, qg, kg, g512, g128, tm):
    m, d = x.shape
    c2 = lambda i: (0, 0)
    row = lambda n: pl.BlockSpec((tm, n), lambda i: (i, 0))
    return pl.pallas_call(
        _even_in_kernel, name="even_in",
        grid=(m // tm,),
        in_specs=[row(d), pl.BlockSpec((1, d), c2),
                  pl.BlockSpec((NSA_PROJ_PAD, d), c2, pipeline_mode=pl.Buffered(1)),
                  pl.BlockSpec((RWKV_PROJ, d), c2, pipeline_mode=pl.Buffered(1)),
                  pl.BlockSpec((1, 512), c2), pl.BlockSpec((2, 128), c2),
                  pl.BlockSpec((512, 512), c2), pl.BlockSpec((128, 128), c2)],
        out_specs=[row(512), row(768), row(LANE), row(RWKV_PROJ)],
        out_shape=[jax.ShapeDtypeStruct((m, 512), BF16), jax.ShapeDtypeStruct((m, 768), F32),
                   jax.ShapeDtypeStruct((m, LANE), F32), jax.ShapeDtypeStruct((m, RWKV_PROJ), F32)],
        compiler_params=_cp("arbitrary"),
    )(x, g.reshape(1, d), wn_t, wr_t, qg, kg, g512, g128)


def _nsa_prep_kernel(z_ref, qg_ref, kg_ref, g512_ref, g128_ref, q_ref, kv_ref):
    z = z_ref[...]
    q = z[:, 0:NSA_WIDTH]
    ms = jnp.dot(q * q, g512_ref[...], precision=HI, preferred_element_type=F32)
    q_ref[...] = (q * lax.rsqrt(ms + RMS_EPS) * qg_ref[...] * (HEAD_DIM ** -0.5)).astype(BF16)
    kv_ref[:, 0:256] = z[:, 512:768]
    for r, off in ((0, 768), (1, 1024)):
        k = z[:, off:off + 128]
        ms = jnp.dot(k * k, g128_ref[...], precision=HI, preferred_element_type=F32)
        kv_ref[:, off - 512:off - 384] = k * lax.rsqrt(ms + RMS_EPS) * kg_ref[r:r + 1, :]
        kv_ref[:, off - 384:off - 256] = z[:, off + 128:off + 256]


def _nsa_prep(z, qg, kg, g512, g128, tm):
    m = z.shape[0]
    const = lambda i: (0, 0)
    return pl.pallas_call(
        _nsa_prep_kernel, name="nsa_prep",
        grid=(m // tm,),
        in_specs=[pl.BlockSpec((tm, NSA_PROJ_PAD), lambda i: (i, 0)),
                  pl.BlockSpec((1, 512), const), pl.BlockSpec((2, 128), const),
                  pl.BlockSpec((512, 512), const), pl.BlockSpec((128, 128), const)],
        out_specs=[pl.BlockSpec((tm, 512), lambda i: (i, 0)),
                   pl.BlockSpec((tm, 768), lambda i: (i, 0))],
        out_shape=[jax.ShapeDtypeStruct((m, 512), BF16),
                   jax.ShapeDtypeStruct((m, 768), F32)],
        compiler_params=_cp("arbitrary"),
    )(z, qg, kg, g512, g128)


def _compress_rows(rows_refs, m, pe_ref, w1_ref, w2_ref, kn_ref, g128_ref):
    outs = []
    for kv, rows_ref in enumerate(rows_refs):
        acc0 = jnp.zeros((m, NSA_KV * CMP_HIDDEN), F32)
        acc1 = jnp.zeros((m, NSA_KV * CMP_HIDDEN), F32)
        for j in range(CMP_STRIDE):
            xj = rows_ref[pl.ds(j, m, stride=CMP_STRIDE), :]
            acc0 = acc0 + jnp.dot((xj + pe_ref[kv, j:j + 1, :]).astype(BF16), w1_ref[kv, j],
                                  preferred_element_type=F32)
            acc1 = acc1 + jnp.dot((xj + pe_ref[kv, CMP_STRIDE + j:CMP_STRIDE + j + 1, :]).astype(BF16),
                                  w1_ref[kv, CMP_STRIDE + j], preferred_element_type=F32)
        hid = acc0 + pltpu.roll(acc1, m - 1, 0)
        outs.append(jnp.dot(jax.nn.gelu(hid).astype(BF16), w2_ref[kv], preferred_element_type=F32))
    kc, vc = outs
    ms = jnp.dot(kc * kc, g128_ref[...], precision=HI, preferred_element_type=F32)
    kc = kc * lax.rsqrt(ms + RMS_EPS) * kn_ref[...]
    return jnp.concatenate([kc, vc], axis=1)


def _compress_kernel(k_ref, v_ref, pe_ref, w1_ref, w2_ref, kn_ref, g128_ref, o_ref, *, m):
    o_ref[0] = _compress_rows((k_ref.at[0], v_ref.at[0]), m, pe_ref, w1_ref, w2_ref, kn_ref, g128_ref)


def _compress(kv, pe, w1, w2, kn, g128):
    b, t, _ = kv.shape
    m = t // CMP_STRIDE
    c2 = lambda i: (0, 0)
    hw = NSA_KV * HEAD_DIM
    return pl.pallas_call(
        functools.partial(_compress_kernel, m=m), name="compress",
        grid=(b,),
        in_specs=[pl.BlockSpec((1, t, hw), lambda i: (i, 0, 0)),
                  pl.BlockSpec((1, t, hw), lambda i: (i, 0, 1)),
                  pl.BlockSpec((2, CMP_LEN, hw), lambda i: (0, 0, 0)),
                  pl.BlockSpec((2, CMP_LEN, hw, NSA_KV * CMP_HIDDEN), lambda i: (0, 0, 0, 0)),
                  pl.BlockSpec((2, NSA_KV * CMP_HIDDEN, hw), lambda i: (0, 0, 0)),
                  pl.BlockSpec((1, 128), c2), pl.BlockSpec((128, 128), c2)],
        out_specs=pl.BlockSpec((1, m, KV_LANES), lambda i: (i, 0, 0)),
        out_shape=jax.ShapeDtypeStruct((b, m, KV_LANES), F32),
        compiler_params=_cp("arbitrary"),
    )(kv, kv, pe, w1, w2, kn, g128)


def _msoftmax(s, mask):
    s = jnp.where(mask, s, NEG)
    mx = jnp.max(s, axis=-1, keepdims=True)
    e = jnp.where(mask, jnp.exp(s - mx), 0.0)
    den = jnp.sum(e, axis=-1, keepdims=True)
    return e / jnp.where(den > 0, den, 1.0)


def _select_blocks(imp, qpos, ns):
    s_io = lax.broadcasted_iota(jnp.int32, imp.shape, 1)
    cur = qpos // SEL_LEN
    forced = (s_io == 0) | (s_io == cur) | (s_io == cur - 1)
    future = s_io * SEL_LEN > qpos
    imp = jnp.where(forced, FORCE_SCORE, imp)
    imp = jnp.where(future, -FORCE_SCORE, imp)
    imp = jnp.where(s_io >= ns, -3e38, imp)
    rank = jnp.zeros(imp.shape, jnp.int32)
    for sp in range(ns):
        col = imp[:, sp:sp + 1]
        beats = (col > imp) | ((col == imp) & (s_io > sp))
        rank = rank + beats.astype(jnp.int32)
    sel = (rank < min(N_SELECT, ns)) & (s_io < ns)
    return sel.astype(F32)


def _heads_to_rows(q, g):
    return jnp.concatenate(
        [q[:, HEAD_DIM * (NSA_QPG * g + h):HEAD_DIM * (NSA_QPG * g + h + 1)] for h in range(NSA_QPG)],
        axis=0)


def _msoftmax0(s, mask):
    s = jnp.where(mask, s, NEG)
    mx = jnp.max(s, axis=0, keepdims=True)
    e = jnp.where(mask, jnp.exp(s - mx), 0.0)
    den = jnp.sum(e, axis=0, keepdims=True)
    return e / jnp.where(den > 0, den, 1.0)


def _select_blocks_t(imp, qpos, ns):
    s_io = lax.broadcasted_iota(jnp.int32, imp.shape, 0)
    cur = qpos // SEL_LEN
    forced = (s_io == 0) | (s_io == cur) | (s_io == cur - 1)
    future = s_io * SEL_LEN > qpos
    imp = jnp.where(forced, FORCE_SCORE, imp)
    imp = jnp.where(future, -FORCE_SCORE, imp)
    imp = jnp.where(s_io >= ns, -3e38, imp)
    rank = jnp.zeros(imp.shape, jnp.int32)
    for sp in range(ns):
        row = imp[sp:sp + 1, :]
        beats = (row > imp) | ((row == imp) & (s_io > sp))
        rank = rank + beats.astype(jnp.int32)
    sel = (rank < min(N_SELECT, ns)) & (s_io < ns)
    return sel.astype(F32)


def _flash_steps_t(chains):
    sts = [_nt(k, qg) for (qg, k, _, _, _, _) in chains]
    mids = []
    for st, (_, _, _, bias, mask, (m_, l_, _)) in zip(sts, chains):
        ps, ms, ls, als = [], [], [], []
        for h in range(NSA_QPG):
            hs = slice(KT * h, KT * (h + 1))
            s = st[:, hs]
            if bias is not None:
                s = s + bias[:, hs]
            s = jnp.where(mask, s, NEG)
            m_new = jnp.maximum(m_[:, hs], jnp.max(s, axis=0, keepdims=True))
            alpha = jnp.exp(m_[:, hs] - m_new)
            e = jnp.where(mask, jnp.exp(s - m_new), 0.0)
            ls.append(alpha * l_[:, hs] + jnp.sum(e, axis=0, keepdims=True))
            ms.append(m_new)
            als.append(alpha)
            ps.append(e.astype(BF16))
        mids.append((jnp.concatenate(ps, axis=1), jnp.concatenate(ms, axis=1),
                     jnp.concatenate(ls, axis=1), jnp.concatenate(als, axis=1)))
    out = []
    for (p, m_new, l_new, alpha), (_, _, v, _, _, (_, _, acc)) in zip(mids, chains):
        out.append((m_new, l_new, alpha * acc + _tn(v, p)))
    return out


def _flash_init_t():
    return (jnp.full((1, NSA_QPG * KT), NEG, F32), jnp.zeros((1, NSA_QPG * KT), F32),
            jnp.zeros((HEAD_DIM, NSA_QPG * KT), F32))


def _flash_out_t(carry):
    _, l_, acc = carry
    return acc / jnp.where(l_ > 0, l_, 1.0)


def _nsa_prompt_kernel(q_ref, gt_ref, cc_ref, kv_ref, bc_ref, tb_ref, ovt_ref, ext_ref, o_ref, selk_s, sc_s,
                       *, m, ns):
    i = pl.program_id(1)
    q = q_ref[0]
    gst = jax.nn.sigmoid(gt_ref[...]).T
    k_io = lax.broadcasted_iota(jnp.int32, (KT, KT), 0)
    q_io = lax.broadcasted_iota(jnp.int32, (KT, KT), 1)
    qpos = i * KT + lax.broadcasted_iota(jnp.int32, (1, KT), 1)
    n_io = lax.broadcasted_iota(jnp.int32, (m, KT), 0)
    mask_c = ((n_io * CMP_STRIDE + (CMP_LEN - 1)) <= qpos) & (n_io < m - 1)
    cc = cc_ref[0]
    groups = range(NSA_KV)
    qgs = [_heads_to_rows(q, g) for g in groups]

    kcs = [cc[:, HEAD_DIM * g:HEAD_DIM * (g + 1)].astype(BF16) for g in groups]
    vcs = [cc[:, 128 + HEAD_DIM * g:128 + HEAD_DIM * (g + 1)].astype(BF16) for g in groups]
    c_off = pl.multiple_of(jnp.maximum(i - 1, 0) * (KT // CMP_STRIDE), KT // CMP_STRIDE)
    sts = []
    for g in groups:
        sc_s[g] = _nt(kcs[g], qgs[g])
        band = jnp.where(i == 0, bc_ref[g, 1], bc_ref[g, 0])
        sc_s[g, pl.ds(c_off, 2 * KT // CMP_STRIDE), :] += band
        sts.append(sc_s[g])
    pcs = [[_msoftmax0(sts[g][:, KT * h:KT * (h + 1)], mask_c) for h in range(NSA_QPG)]
           for g in groups]
    o_cs = [_tn(vcs[g], jnp.concatenate(pcs[g], axis=1).astype(BF16)) for g in groups]
    imps = [jnp.dot(ovt_ref[...], pcs[g][0] + pcs[g][1] + pcs[g][2] + pcs[g][3], precision=HI,
                    preferred_element_type=F32) for g in groups]
    sels = [_select_blocks_t(imps[g], qpos, ns).astype(BF16) for g in groups]
    for g in groups:
        selk_s[g] = jnp.dot(ext_ref[...], sels[g], preferred_element_type=F32)

    def load_kv(g, kt, off, ntile=1):
        r0 = pl.multiple_of(kt * KT, KT)
        k_lo = 256 + off + HEAD_DIM * g
        v_lo = 384 + off + HEAD_DIM * g
        k = kv_ref[0, pl.ds(r0, ntile * KT), k_lo:k_lo + HEAD_DIM].astype(BF16)
        v = kv_ref[0, pl.ds(r0, ntile * KT), v_lo:v_lo + HEAD_DIM].astype(BF16)
        return k, v

    def sel_mask(g, kt, ntile=1):
        r0 = pl.multiple_of(kt * KT, KT)
        return selk_s[g, pl.ds(r0, ntile * KT), :] > 0.5

    n_main = jnp.maximum(i - 1, 0)

    def body(pr, carry):
        chains = []
        for g in groups:
            k, v = load_kv(g, 2 * pr, 0, 2)
            chains.append((qgs[g], k, v, None, sel_mask(g, 2 * pr, 2), carry[g]))
        return tuple(_flash_steps_t(chains))

    c_sel = list(lax.fori_loop(0, n_main // 2, body, tuple(_flash_init_t() for _ in groups)))
    kt_odd = (n_main // 2) * 2
    has_odd = n_main % 2 == 1

    nwt = WINDOW // KT
    c_win = [_flash_init_t() for _ in groups]
    for d in range(nwt, -1, -1):
        kt = i - d
        ktc = jnp.maximum(kt, 0)
        if d == nwt:
            mk = (k_io > q_io) & (kt >= 0)
        elif d == 0:
            mk = k_io <= q_io
        else:
            mk = kt >= 0
        chains = []
        for g in groups:
            k, v = load_kv(g, ktc, 256)
            bias = tb_ref[g, 0] if d == 1 else (tb_ref[g, 1] if d == 0 else None)
            chains.append((qgs[g], k, v, bias, mk, c_win[g]))
        if d == nwt:
            for g in groups:
                k, v = load_kv(g, kt_odd, 0)
                chains.append((qgs[g], k, v, None, sel_mask(g, kt_odd) & has_odd, c_sel[g]))
        elif d <= 1:
            for g in groups:
                k, v = load_kv(g, ktc, 0)
                smk = sel_mask(g, ktc) & ((kt >= 0) if d == 1 else (k_io <= q_io))
                chains.append((qgs[g], k, v, tb_ref[g, 1 - d], smk, c_sel[g]))
        res = _flash_steps_t(chains)
        c_win = res[:NSA_KV]
        if len(res) > NSA_KV:
            c_sel = res[NSA_KV:]
    o_ss = [_flash_out_t(c_sel[g]) for g in groups]
    o_ws = [_flash_out_t(c_win[g]) for g in groups]

    for g in groups:
        for h in range(NSA_QPG):
            hh = NSA_QPG * g + h
            hs = slice(KT * h, KT * (h + 1))
            o = (gst[3 * hh:3 * hh + 1, :] * o_cs[g][:, hs] + gst[3 * hh + 1:3 * hh + 2, :] * o_ss[g][:, hs]
                 + gst[3 * hh + 2:3 * hh + 3, :] * o_ws[g][:, hs])
            o_ref[0, HEAD_DIM * hh:HEAD_DIM * (hh + 1), :] = o.astype(BF16)


def _nsa_prompt(qn, z_nsa, cc, kv, bct, tbt, ovt, ext, b, t_len):
    m = cc.shape[1]
    ns = -(-t_len // SEL_LEN)
    nq = t_len // KT
    nsp = ovt.shape[0]
    c2 = lambda bi, i: (0, 0)
    return pl.pallas_call(
        functools.partial(_nsa_prompt_kernel, m=m, ns=ns), name="nsa_prompt",
        grid=(b, nq),
        in_specs=[pl.BlockSpec((1, KT, 512), lambda bi, i: (bi, i, 0)),
                  pl.BlockSpec((KT, LANE), lambda bi, i: (bi * nq + i, 0)),
                  pl.BlockSpec((1, m, KV_LANES), lambda bi, i: (bi, 0, 0)),
                  pl.BlockSpec((1, t_len, 768), lambda bi, i: (bi, 0, 0)),
                  pl.BlockSpec((NSA_KV, 2, 2 * KT // CMP_STRIDE, NSA_QPG * KT), lambda bi, i: (0, 0, 0, 0)),
                  pl.BlockSpec((NSA_KV, 2, KT, NSA_QPG * KT), lambda bi, i: (0, 0, 0, 0)),
                  pl.BlockSpec((nsp, m), c2),
                  pl.BlockSpec((t_len, nsp), c2)],
        out_specs=pl.BlockSpec((1, 512, KT), lambda bi, i: (bi, 0, i)),
        out_shape=jax.ShapeDtypeStruct((b, 512, t_len), BF16),
        scratch_shapes=[pltpu.VMEM((NSA_KV, t_len, KT), F32),
                        pltpu.VMEM((NSA_KV, m, NSA_QPG * KT), F32)],
        compiler_params=_cp("arbitrary", "arbitrary"),
    )(qn.reshape(b, t_len, 512), z_nsa, cc, kv.reshape(b, t_len, 768), bct, tbt, ovt, ext)


def _nsa_sample_kernel(pt_ref, cmp_pg, slc_pg, new_ref, win_ref, q_ref, gt_ref, pe_ref, w1_ref, w2_ref,
                       kn_ref, g128_ref, bc_ref, bs_ref, bw_ref, ov_ref, ex_ref, o_ref,
                       cmpk_s, cmpv_s, slc_s, win_s, *, npages, past, wb, tq, m, ns):
    p = pl.program_id(1)
    r0 = pl.multiple_of(p * PAGE, PAGE)
    cmpk_s[pl.ds(r0, PAGE), :] = cmp_pg[0, 0, :, 0:128]
    cmpv_s[pl.ds(r0, PAGE), :] = cmp_pg[0, 0, :, 128:256]
    slc_s[pl.ds(r0, PAGE), :] = slc_pg[0, 0]

    @pl.when(p == npages - 1)
    def _():
        kp = slc_s.shape[0]
        wp = win_s.shape[0]
        new = new_ref[0]
        pad = jnp.zeros((kp - past - tq, KV_LANES), F32)
        cmpk_s[past:past + tq, :] = new[:, 0:128]
        cmpk_s[past + tq:kp, :] = pad[:, 0:128]
        cmpv_s[past:past + tq, :] = new[:, 128:256]
        cmpv_s[past + tq:kp, :] = pad[:, 0:128]
        slc_s[past:past + tq, :] = new[:, 256:512]
        slc_s[past + tq:kp, :] = pad
        win_s[0:wb, :] = win_ref[0, 0]
        win_s[wb:wb + tq, :] = new[:, 512:768]
        win_s[wb + tq:wp, :] = jnp.zeros((wp - wb - tq, KV_LANES), F32)

        cc = _compress_rows((cmpk_s, cmpv_s), m, pe_ref, w1_ref, w2_ref, kn_ref, g128_ref)
        q = q_ref[0]
        gs = jax.nn.sigmoid(gt_ref[0])
        qpos = past + lax.broadcasted_iota(jnp.int32, (tq, 1), 0)
        n_io = lax.broadcasted_iota(jnp.int32, (tq, m), 1)
        mask_c = ((n_io * CMP_STRIDE + (CMP_LEN - 1)) <= qpos) & (n_io < m - 1)
        k_io = lax.broadcasted_iota(jnp.int32, (tq, kp), 1)
        mask_s0 = (k_io <= qpos) & (k_io < past + tq)
        j_io = lax.broadcasted_iota(jnp.int32, (tq, wp), 1)
        dist_w = qpos - (past - wb + j_io)
        mask_w = (dist_w >= 0) & (dist_w < WINDOW) & (j_io < wb + tq)
        outs = []
        for g in range(NSA_KV):
            qg = _heads_to_rows(q, g)
            kc = cc[:, HEAD_DIM * g:HEAD_DIM * (g + 1)].astype(BF16)
            vc = cc[:, 128 + HEAD_DIM * g:128 + HEAD_DIM * (g + 1)].astype(BF16)
            s = _nt(qg, kc).reshape(NSA_QPG, tq, m) + bc_ref[NSA_QPG * g:NSA_QPG * (g + 1)]
            p_c = _msoftmax(s, mask_c[None])
            o_c = jnp.dot(p_c.reshape(NSA_QPG * tq, m).astype(BF16), vc,
                          preferred_element_type=F32).reshape(NSA_QPG, tq, HEAD_DIM)
            psum = p_c[0] + p_c[1] + p_c[2] + p_c[3]
            imp = jnp.dot(psum, ov_ref[...], precision=HI, preferred_element_type=F32)
            sel = _select_blocks(imp, qpos, ns)
            selk = jnp.dot(sel.astype(BF16), ex_ref[...], preferred_element_type=F32)

            ks = slc_s[:, HEAD_DIM * g:HEAD_DIM * (g + 1)].astype(BF16)
            vs = slc_s[:, 128 + HEAD_DIM * g:128 + HEAD_DIM * (g + 1)].astype(BF16)
            s = _nt(qg, ks).reshape(NSA_QPG, tq, kp) + bs_ref[NSA_QPG * g:NSA_QPG * (g + 1)]
            p_s = _msoftmax(s, ((selk > 0.5) & mask_s0)[None])
            o_s = jnp.dot(p_s.reshape(NSA_QPG * tq, kp).astype(BF16), vs,
                          preferred_element_type=F32).reshape(NSA_QPG, tq, HEAD_DIM)

            kw = win_s[:, HEAD_DIM * g:HEAD_DIM * (g + 1)].astype(BF16)
            vw = win_s[:, 128 + HEAD_DIM * g:128 + HEAD_DIM * (g + 1)].astype(BF16)
            s = _nt(qg, kw).reshape(NSA_QPG, tq, wp) + bw_ref[NSA_QPG * g:NSA_QPG * (g + 1)]
            p_w = _msoftmax(s, mask_w[None])
            o_w = jnp.dot(p_w.reshape(NSA_QPG * tq, wp).astype(BF16), vw,
                          preferred_element_type=F32).reshape(NSA_QPG, tq, HEAD_DIM)
            for h in range(NSA_QPG):
                hh = NSA_QPG * g + h
                o = (gs[:, 3 * hh:3 * hh + 1] * o_c[h] + gs[:, 3 * hh + 1:3 * hh + 2] * o_s[h]
                     + gs[:, 3 * hh + 2:3 * hh + 3] * o_w[h])
                outs.append(o)
        o_ref[0] = jnp.concatenate(outs, axis=1).astype(BF16)


def _nsa_sample(pt, pool_cmp, pool_slc, li, kv_new, win, qn, z_nsa, cw, bc, bs, bw, ov, ex, b, tq):
    npages = pt.shape[1]
    past = npages * PAGE
    wb = win.shape[2]
    kp = _round_up(past + tq, LANE)
    wp = _round_up(wb + tq, LANE)
    m = (past + tq) // CMP_STRIDE
    ns = -(-(past + tq) // SEL_LEN)
    pe, w1, w2, kn, g128 = cw
    c2 = lambda bi, p, pt_: (0, 0)
    c3 = lambda bi, p, pt_: (0, 0, 0)
    per_b3 = lambda bi, p, pt_: (bi, 0, 0)
    grid_spec = pltpu.PrefetchScalarGridSpec(
        num_scalar_prefetch=1,
        grid=(b, npages),
        in_specs=[pl.BlockSpec((1, 1, PAGE, KV_LANES), lambda bi, p, pt_: (pt_[bi, p], li, 0, 0)),
                  pl.BlockSpec((1, 1, PAGE, KV_LANES), lambda bi, p, pt_: (pt_[bi, p], li, 0, 0)),
                  pl.BlockSpec((1, tq, 768), per_b3),
                  pl.BlockSpec((1, 1, wb, KV_LANES), lambda bi, p, pt_: (bi, li, 0, 0)),
                  pl.BlockSpec((1, tq, 512), per_b3),
                  pl.BlockSpec((1, tq, LANE), lambda bi, p, pt_: (bi, 0, GATE_COL_BLOCK)),
                  pl.BlockSpec((2, CMP_LEN, LANE), c3),
                  pl.BlockSpec((2, CMP_LEN, LANE, NSA_KV * CMP_HIDDEN), lambda bi, p, pt_: (0, 0, 0, 0)),
                  pl.BlockSpec((2, NSA_KV * CMP_HIDDEN, LANE), c3),
                  pl.BlockSpec((1, 128), c2), pl.BlockSpec((128, 128), c2),
                  pl.BlockSpec((NSA_HEADS, tq, m), c3),
                  pl.BlockSpec((NSA_HEADS, tq, kp), c3),
                  pl.BlockSpec((NSA_HEADS, tq, wp), c3),
                  pl.BlockSpec((m, LANE), c2),
                  pl.BlockSpec((LANE, kp), c2)],
        out_specs=pl.BlockSpec((1, tq, 512), per_b3),
        scratch_shapes=[pltpu.VMEM((kp, LANE), F32), pltpu.VMEM((kp, LANE), F32),
                        pltpu.VMEM((kp, KV_LANES), F32), pltpu.VMEM((wp, KV_LANES), F32)])
    kern = functools.partial(_nsa_sample_kernel, npages=npages, past=past, wb=wb, tq=tq, m=m, ns=ns)
    return pl.pallas_call(
        kern, grid_spec=grid_spec, name="nsa_sample",
        out_shape=jax.ShapeDtypeStruct((b, tq, 512), BF16),
        compiler_params=_cp("arbitrary", "arbitrary"),
    )(pt, pool_cmp, pool_slc, kv_new.reshape(b, tq, 768), win, qn.reshape(b, tq, 512),
      z_nsa.reshape(b, tq, NSA_PROJ_PAD), pe, w1, w2, kn, g128, bc, bs, bw, ov, ex)


def _nsa_paged_kernel(pt_ref, cmp_hbm, slc_hbm, new_ref, win_ref, q_ref, gt_ref, pe_ref, w1_ref, w2_ref,
                      kn_ref, g128_ref, bc_ref, bs_ref, bw_ref, ov_ref, ex_ref, o_ref, wo_ref,
                      cmpt_s, slct_s, cmpk_s, cmpv_s, sems, *, li, nsteps, nbr, npages, past, wb, tq, m, ns):
    step = pl.program_id(0)
    slot = step % 2

    def page_copy(st_, bb, p, sl, which):
        hbm, buf = ((cmp_hbm, cmpt_s), (slc_hbm, slct_s))[which]
        return pltpu.make_async_copy(hbm.at[pt_ref[st_ * nbr + bb, p], li],
                                     buf.at[sl, bb, :, pl.ds(p * PAGE, PAGE)], sems.at[sl, which])

    def start_step(st_, sl):
        for bb in range(nbr):
            for p in range(npages):
                page_copy(st_, bb, p, sl, 0).start()
                page_copy(st_, bb, p, sl, 1).start()

    @pl.when(step == 0)
    def _():
        start_step(0, 0)

    @pl.when(step + 1 < nsteps)
    def _():
        start_step(step + 1, 1 - slot)

    for bb in range(nbr):
        for p in range(npages):
            page_copy(step, bb, p, slot, 0).wait()
            page_copy(step, bb, p, slot, 1).wait()

    hw = NSA_KV * HEAD_DIM
    for bb in range(nbr):
        for p in range(npages):
            blk = cmpt_s[slot, bb, :, p * PAGE:(p + 1) * PAGE]
            r0 = bb * past + p * PAGE
            cmpk_s[r0:r0 + PAGE, :] = blk[0:hw].T
            cmpv_s[r0:r0 + PAGE, :] = blk[hw:2 * hw].T
    cc_all = _compress_rows((cmpk_s, cmpv_s), nbr * m, pe_ref, w1_ref, w2_ref, kn_ref, g128_ref)

    pairs = [(bb, g) for bb in range(nbr) for g in range(NSA_KV)]
    groups = range(len(pairs))
    ccs = [cc_all[m * bb:m * (bb + 1)] for bb in range(nbr)]
    newt32 = [jnp.concatenate([new_ref[bb], jnp.zeros((LANE - tq, 768), F32)], axis=0).T
              for bb in range(nbr)]
    newts = [n_.astype(BF16) for n_ in newt32]
    for bb in range(nbr):
        wo_ref[bb] = jnp.concatenate([win_ref[bb, 0, :, wb + tq - WINDOW:wb],
                                      newt32[bb][2 * KV_LANES:3 * KV_LANES, 0:tq]], axis=1)
    gss = [jax.nn.sigmoid(gt_ref[bb]) for bb in range(nbr)]
    kp = past + LANE
    wp = wb + LANE
    qpos = past + lax.broadcasted_iota(jnp.int32, (tq, 1), 0)
    n_io = lax.broadcasted_iota(jnp.int32, (tq, m), 1)
    mask_c = ((n_io * CMP_STRIDE + (CMP_LEN - 1)) <= qpos) & (n_io < m - 1)
    k_io = lax.broadcasted_iota(jnp.int32, (tq, kp), 1)
    mask_s0 = (k_io <= qpos) & (k_io < past + tq)
    j_io = lax.broadcasted_iota(jnp.int32, (tq, wp), 1)
    dist_w = qpos - (past - wb + j_io)
    mask_w = (dist_w >= 0) & (dist_w < WINDOW) & (j_io < wb + tq)
    rep = lambda a_: jnp.concatenate([a_] * NSA_QPG, axis=0)
    mask_c4, mask_w4 = rep(mask_c), rep(mask_w)
    gsl = lambda g, off: slice(off + HEAD_DIM * g, off + HEAD_DIM * (g + 1))
    qgs = [_heads_to_rows(q_ref[bb], g) for bb, g in pairs]
    bof = [bb for bb, _ in pairs]
    gof = [g for _, g in pairs]

    s_c = [_nt(qgs[i], ccs[bof[i]][:, gsl(gof[i], 0)].astype(BF16)) + bc_ref[gof[i]] for i in groups]
    p_c = [_msoftmax(s_c[i], mask_c4) for i in groups]
    o_c = [jnp.dot(p_c[i].astype(BF16), ccs[bof[i]][:, gsl(gof[i], hw)].astype(BF16),
                   preferred_element_type=F32) for i in groups]
    imps = [jnp.dot(p_c[i][0:tq] + p_c[i][tq:2 * tq] + p_c[i][2 * tq:3 * tq] + p_c[i][3 * tq:4 * tq],
                    ov_ref[...], precision=HI, preferred_element_type=F32) for i in groups]
    sels = [_select_blocks(imps[i], qpos, ns).astype(BF16) for i in groups]
    selk = [jnp.dot(sels[i], ex_ref[...], preferred_element_type=F32) for i in groups]

    s_s = [jnp.concatenate(
        [jnp.dot(qgs[i], slct_s[slot, bof[i], gsl(gof[i], 0), :].astype(BF16), preferred_element_type=F32),
         jnp.dot(qgs[i], newts[bof[i]][gsl(gof[i], 256)], preferred_element_type=F32)], axis=1)
        + bs_ref[gof[i]] for i in groups]
    p_s = [_msoftmax(s_s[i], rep((selk[i] > 0.5) & mask_s0)) for i in groups]
    o_s = [_nt(p_s[i][:, 0:past].astype(BF16), slct_s[slot, bof[i], gsl(gof[i], hw), :].astype(BF16))
           + _nt(p_s[i][:, past:kp].astype(BF16), newts[bof[i]][gsl(gof[i], 256 + hw)]) for i in groups]

    s_w = [jnp.concatenate(
        [jnp.dot(qgs[i], win_ref[bof[i], 0, gsl(gof[i], 0), :].astype(BF16), preferred_element_type=F32),
         jnp.dot(qgs[i], newts[bof[i]][gsl(gof[i], 512)], preferred_element_type=F32)], axis=1)
        + bw_ref[gof[i]] for i in groups]
    p_w = [_msoftmax(s_w[i], mask_w4) for i in groups]
    o_w = [_nt(p_w[i][:, 0:wb].astype(BF16), win_ref[bof[i], 0, gsl(gof[i], hw), :].astype(BF16))
           + _nt(p_w[i][:, wb:wp].astype(BF16), newts[bof[i]][gsl(gof[i], 512 + hw)]) for i in groups]

    for bb in range(nbr):
        outs = []
        gs = gss[bb]
        for g in range(NSA_KV):
            i = bb * NSA_KV + g
            for h in range(NSA_QPG):
                hh = NSA_QPG * g + h
                rs = slice(tq * h, tq * (h + 1))
                outs.append(gs[:, 3 * hh:3 * hh + 1] * o_c[i][rs] + gs[:, 3 * hh + 1:3 * hh + 2] * o_s[i][rs]
                            + gs[:, 3 * hh + 2:3 * hh + 3] * o_w[i][rs])
        o_ref[bb] = jnp.concatenate(outs, axis=1).astype(BF16)


def _nsa_paged(pt, pool_cmp_t, pool_slc_t, li, kv_new, win_t, qn, z_nsa, cw, bc, bs, bw, ov, ex, b, tq):
    npages = pt.shape[1]
    past = npages * PAGE
    wb = win_t.shape[3]
    assert past % CMP_STRIDE == 0 and tq < CMP_STRIDE
    assert wb + tq >= WINDOW
    kp = past + LANE
    wp = wb + LANE
    m = (past + tq) // CMP_STRIDE
    ns = -(-(past + tq) // SEL_LEN)
    rows = NSA_QPG * tq
    pe, w1, w2, kn, g128 = cw
    hw = NSA_KV * HEAD_DIM
    c2 = lambda bi, pt_: (0, 0)
    c3 = lambda bi, pt_: (0, 0, 0)
    per_b3 = lambda bi, pt_: (bi, 0, 0)
    nbr = 2 if b % 2 == 0 else 1
    grid_spec = pltpu.PrefetchScalarGridSpec(
        num_scalar_prefetch=1,
        grid=(b // nbr,),
        in_specs=[pl.BlockSpec(memory_space=pl.ANY),
                  pl.BlockSpec(memory_space=pl.ANY),
                  pl.BlockSpec((nbr, tq, 768), per_b3),
                  pl.BlockSpec((nbr, 1, KV_LANES, wb), lambda bi, pt_: (bi, li, 0, 0)),
                  pl.BlockSpec((nbr, tq, 512), per_b3),
                  pl.BlockSpec((nbr, tq, LANE), lambda bi, pt_: (bi, 0, 0)),
                  pl.BlockSpec((2, CMP_LEN, hw), c3),
                  pl.BlockSpec((2, CMP_LEN, hw, NSA_KV * CMP_HIDDEN), lambda bi, pt_: (0, 0, 0, 0)),
                  pl.BlockSpec((2, NSA_KV * CMP_HIDDEN, hw), c3),
                  pl.BlockSpec((1, 128), c2), pl.BlockSpec((128, 128), c2),
                  pl.BlockSpec((NSA_KV, rows, m), c3),
                  pl.BlockSpec((NSA_KV, rows, kp), c3),
                  pl.BlockSpec((NSA_KV, rows, wp), c3),
                  pl.BlockSpec((m, LANE), c2),
                  pl.BlockSpec((LANE, kp), c2)],
        out_specs=[pl.BlockSpec((nbr, tq, 512), per_b3),
                   pl.BlockSpec((nbr, KV_LANES, WINDOW), per_b3)],
        scratch_shapes=[pltpu.VMEM((2, nbr, KV_LANES, past), F32), pltpu.VMEM((2, nbr, KV_LANES, past), F32),
                        pltpu.VMEM((nbr * past, hw), F32), pltpu.VMEM((nbr * past, hw), F32),
                        pltpu.SemaphoreType.DMA((2, 2))])
    kern = functools.partial(_nsa_paged_kernel, li=li, nsteps=b // nbr, nbr=nbr, npages=npages, past=past,
                             wb=wb, tq=tq, m=m, ns=ns)
    return pl.pallas_call(
        kern, grid_spec=grid_spec, name="nsa_paged",
        out_shape=[jax.ShapeDtypeStruct((b, tq, 512), BF16),
                   jax.ShapeDtypeStruct((b, KV_LANES, WINDOW), F32)],
        compiler_params=_cp("arbitrary"),
    )(pt, pool_cmp_t, pool_slc_t, kv_new.reshape(b, tq, 768), win_t, qn.reshape(b, tq, 512),
      z_nsa.reshape(b, tq, LANE), pe, w1, w2, kn, g128, bc, bs, bw, ov, ex)


def _rwkv_kernel(z_ref, zp_ref, s0_ref, mu_ref, w0_ref, w2_ref, a0_ref, a2_ref, g2_ref, kkp_ref, ka_ref,
                 rk_ref, lnw_ref, lnb_ref, y_ref, st_ref, s_scr, carry_scr, *, c_len, nb):
    c = pl.program_id(1)
    cl = c_len
    nr = nb * cl

    @pl.when(c == 0)
    def _():
        s_scr[...] = s0_ref[...]
        carry_scr[...] = zp_ref[...]

    z3 = z_ref[...]
    row = lax.broadcasted_iota(jnp.int32, z3.shape, 1)
    shifted = jnp.where(row == 0, carry_scr[...], pltpu.roll(z3, 1, 1))
    carry_scr[...] = z3[:, cl - 1:cl, :]
    zz = (z3 + (shifted - z3) * mu_ref[...]).reshape(nr, RWKV_PROJ)
    w = RWKV_WIDTH
    r = zz[:, 0:w]
    k = zz[:, w:2 * w]
    v = zz[:, 2 * w:3 * w]
    zw = zz[:, 3 * w:3 * w + DECAY_LORA]
    za = zz[:, 3 * w + DECAY_LORA:3 * w + DECAY_LORA + AAA_LORA]
    zg = zz[:, 3 * w + DECAY_LORA + AAA_LORA:]
    xw = -(w0_ref[...] + _mmp(jnp.tanh(zw), w2_ref[...], _NN, RWKV_LORA_MM))
    softplus = jnp.maximum(xw, 0.0) + jnp.log(1.0 + jnp.exp(-jnp.abs(xw)))
    logdec = -jnp.exp(-softplus - 0.5)
    a = jax.nn.sigmoid(a0_ref[...] + _mmp(za, a2_ref[...], _NN, RWKV_LORA_MM))
    gate = _mmp(jax.nn.sigmoid(zg), g2_ref[...], _NN, RWKV_LORA_MM)
    k_mod = k * (1.0 + (a - 1.0) * ka_ref[...])
    kku = k * kkp_ref[...]

    ti = lax.broadcasted_iota(jnp.int32, (nr, nr), 0)
    si = lax.broadcasted_iota(jnp.int32, (nr, nr), 1)
    tri = ((si <= ti) & (si >= (ti // cl) * cl)).astype(BF16)
    ld_hi = logdec.astype(BF16)
    ld_r = logdec - ld_hi.astype(F32)
    ld_mid = ld_r.astype(BF16)
    ld_lo = (ld_r - ld_mid.astype(F32)).astype(BF16)
    gcum = (jnp.dot(tri, ld_hi, preferred_element_type=F32)
            + (jnp.dot(tri, ld_mid, preferred_element_type=F32)
               + jnp.dot(tri, ld_lo, preferred_element_type=F32)))
    e_g_all = jnp.exp(gcum)
    e_gm_all = jnp.exp(gcum - logdec)
    e_ng_all = jnp.exp(-gcum)
    nsteps = max(1, int(math.ceil(math.log2(cl))))
    row2 = lax.broadcasted_iota(jnp.int32, (cl, 2 * cl), 0)
    col2 = lax.broadcasted_iota(jnp.int32, (cl, 2 * cl), 1)
    col2 = jnp.where(col2 >= cl, col2 - cl, col2)
    strict2 = col2 < row2
    lower2 = col2 <= row2
    keep_z = lax.broadcasted_iota(jnp.int32, (cl, cl + HEAD_DIM), 1) >= cl

    pairs = [(bb, hd) for bb in range(nb) for hd in range(RWKV_HEADS)]
    heads = range(len(pairs))
    rws = [slice(cl * bb, cl * (bb + 1)) for bb, _ in pairs]
    sls = [slice(HEAD_DIM * hd, HEAD_DIM * (hd + 1)) for _, hd in pairs]
    kks = [kku[rws[h], sls[h]] for h in heads]
    kks = [kk * lax.rsqrt(jnp.maximum(jnp.sum(kk * kk, axis=-1, keepdims=True), 1e-24)) for kk in kks]
    kkas = [kks[h] * a[rws[h], sls[h]] for h in heads]
    kms = [k_mod[rws[h], sls[h]] for h in heads]
    vhs = [v[rws[h], sls[h]] for h in heads]
    rhs_ = [r[rws[h], sls[h]] for h in heads]
    s_hs = [s_scr[bb, hd] for bb, hd in pairs]
    e_g = [e_g_all[rws[h], sls[h]] for h in heads]
    e_gm = [e_gm_all[rws[h], sls[h]] for h in heads]
    e_ng = [e_ng_all[rws[h], sls[h]] for h in heads]
    glast = [gcum[cl * (bb + 1) - 1:cl * (bb + 1), sls[h]] for h, (bb, _) in enumerate(pairs)]
    e_lg = [jnp.exp(glast[h] - gcum[rws[h], sls[h]]) for h in heads]
    e_l = [jnp.exp(glast[h]) for h in heads]
    lrs = [jnp.concatenate([-kks[h] * e_gm[h], rhs_[h] * e_g[h]], axis=0) for h in heads]
    rrs = [jnp.concatenate([kkas[h] * e_ng[h], kms[h] * e_ng[h]], axis=0) for h in heads]
    m1s = [_mmp(lrs[h], rrs[h], _NT) for h in heads]
    m2s = [_mmp(lrs[h], s_hs[h], _NT) for h in heads]
    tops = [jnp.where(strict2, m1[0:cl], 0.0) for m1 in m1s]
    bots = [jnp.where(lower2, m1[cl:2 * cl], 0.0) for m1 in m1s]
    zeros_v = jnp.zeros((cl, HEAD_DIM), F32)
    rhss = [m2s[h][0:cl] + _mmp(tops[h], jnp.concatenate([zeros_v, vhs[h]], axis=0), _NN) for h in heads]
    wms = [jnp.concatenate([tops[h][:, 0:cl], rhss[h]], axis=1) for h in heads]
    for _ in range(nsteps):
        wms = [_mmp(w_[:, 0:cl], w_, _NN) + jnp.where(keep_z, w_, 0.0) for w_ in wms]
    zvs = [jnp.concatenate([wms[h][:, cl:cl + HEAD_DIM], vhs[h]], axis=0) for h in heads]
    ys = [m2s[h][cl:2 * cl] + _mmp(bots[h], zvs[h], _NN) for h in heads]
    bkhs = [jnp.concatenate([kkas[h] * e_lg[h], kms[h] * e_lg[h]], axis=0) for h in heads]
    s_new = [s_hs[h] * e_l[h] + _mmp(zvs[h], bkhs[h], _TN) for h in heads]
    for h, (bb, hd) in enumerate(pairs):
        s_scr[bb, hd] = s_new[h]
    outs = []
    for h in heads:
        y = ys[h]
        sl = sls[h]
        mean = jnp.mean(y, axis=-1, keepdims=True)
        yc = y - mean
        var = jnp.mean(yc * yc, axis=-1, keepdims=True)
        yn = yc * lax.rsqrt(var + GN_EPS) * lnw_ref[:, sl] + lnb_ref[:, sl]
        bonus = jnp.sum(rhs_[h] * kms[h] * rk_ref[:, sl], axis=-1, keepdims=True) * vhs[h]
        outs.append(yn + bonus)
    for bb in range(nb):
        yb = jnp.concatenate(outs[RWKV_HEADS * bb:RWKV_HEADS * (bb + 1)], axis=1)
        y_ref[bb] = (yb * gate[cl * bb:cl * (bb + 1)]).astype(BF16)
    st_ref[...] = s_scr[...]


def _rwkv(z_rw, z_prev, s0, pr, b, t_len, c_len, nb):
    nchunk = t_len // c_len
    c2 = lambda bi, c: (0, 0)
    vec = lambda n: pl.BlockSpec((1, n), c2)
    return pl.pallas_call(
        functools.partial(_rwkv_kernel, c_len=c_len, nb=nb), name="rwkv",
        grid=(b // nb, nchunk),
        in_specs=[pl.BlockSpec((nb, c_len, RWKV_PROJ), lambda bi, c: (bi, c, 0)),
                  pl.BlockSpec((nb, 1, RWKV_PROJ), lambda bi, c: (bi, 0, 0)),
                  pl.BlockSpec((nb, RWKV_HEADS, HEAD_DIM, HEAD_DIM), lambda bi, c: (bi, 0, 0, 0)),
                  vec(RWKV_PROJ), vec(512), pl.BlockSpec((DECAY_LORA, 512), c2),
                  vec(512), pl.BlockSpec((AAA_LORA, 512), c2), pl.BlockSpec((GATE_LORA, 512), c2),
                  vec(512), vec(512), vec(512), vec(512), vec(512)],
        out_specs=[pl.BlockSpec((nb, c_len, 512), lambda bi, c: (bi, c, 0)),
                   pl.BlockSpec((nb, RWKV_HEADS, HEAD_DIM, HEAD_DIM), lambda bi, c: (bi, 0, 0, 0))],
        out_shape=[jax.ShapeDtypeStruct((b, t_len, 512), BF16),
                   jax.ShapeDtypeStruct((b, RWKV_HEADS, HEAD_DIM, HEAD_DIM), F32)],
        scratch_shapes=[pltpu.VMEM((nb, RWKV_HEADS, HEAD_DIM, HEAD_DIM), F32),
                        pltpu.VMEM((nb, 1, RWKV_PROJ), F32)],
        compiler_params=_cp("arbitrary", "arbitrary"),
    )(z_rw.reshape(b, t_len, RWKV_PROJ), z_prev.reshape(b, 1, RWKV_PROJ), s0, *pr)


def _rel_bucket(dist):
    n = jnp.maximum(dist, 0)
    max_exact = NUM_BUCKETS // 2
    nf = jnp.maximum(n, 1).astype(F32)
    large = max_exact + (jnp.log(nf / max_exact) / math.log(MAX_DISTANCE / max_exact)
                         * (NUM_BUCKETS - max_exact)).astype(jnp.int32)
    large = jnp.minimum(large, NUM_BUCKETS - 1)
    return jnp.where(n < max_exact, n, large)


def _bias_table(rel_bias, dist, delta):
    tb = rel_bias.astype(F32)
    out = jnp.moveaxis(tb[_rel_bucket(dist)], -1, 0)
    if delta:
        out = out - tb[NUM_BUCKETS - 1].reshape((NSA_HEADS,) + (1,) * dist.ndim)
    return out


def _block_diag_ones(n, grp):
    idx = jnp.arange(n) // grp
    return (idx[:, None] == idx[None, :]).astype(F32) / grp


def _compress_weights(phi_pe, phi_w1, phi_w2, kn_cmp):
    pe = jnp.concatenate([phi_pe, phi_pe], axis=-1)
    w1 = jnp.zeros((2, CMP_LEN, NSA_KV * HEAD_DIM, NSA_KV * CMP_HIDDEN), F32)
    w2 = jnp.zeros((2, NSA_KV * CMP_HIDDEN, NSA_KV * HEAD_DIM), F32)
    for g in range(NSA_KV):
        w1 = w1.at[:, :, HEAD_DIM * g:HEAD_DIM * (g + 1), CMP_HIDDEN * g:CMP_HIDDEN * (g + 1)].set(phi_w1)
        w2 = w2.at[:, CMP_HIDDEN * g:CMP_HIDDEN * (g + 1), HEAD_DIM * g:HEAD_DIM * (g + 1)].set(phi_w2)
    kn = jnp.tile(kn_cmp, NSA_KV).reshape(1, 128)
    return pe, w1.astype(BF16), w2.astype(BF16), kn


def _overlap_expand(t_all):
    nc = t_all // CMP_STRIDE - 1
    m = nc + 1
    ns = -(-t_all // SEL_LEN)
    c0 = jnp.arange(m)[:, None] * CMP_STRIDE
    s0 = jnp.arange(LANE)[None, :] * SEL_LEN
    ov = ((c0 < s0 + SEL_LEN) & (c0 + CMP_LEN > s0) & (jnp.arange(m)[:, None] < nc)
          & (jnp.arange(LANE)[None, :] < ns)).astype(F32)
    kp = _round_up(t_all, LANE)
    ex = (jnp.arange(kp)[None, :] // SEL_LEN == jnp.arange(LANE)[:, None]).astype(BF16)
    return ov, ex


def _pick_tm(t_len):
    return 512 if t_len % 512 == 0 else 128


def _trunk(x3, prm, st):
    b, t_len, d = x3.shape
    m_rows = b * t_len
    x = x3.reshape(m_rows, d)
    prompt = st is None
    if prompt:
        tm = _pick_tm(t_len)
        c_len = 64
    else:
        tm = 256 if m_rows % 256 == 0 else m_rows
        c_len = t_len
    tm_f = tm
    g512 = _block_diag_ones(512, HEAD_DIM)
    g128 = _block_diag_ones(128, HEAD_DIM)
    rel_bias = prm["rel_bias"]

    if prompt:
        ov, ex = _overlap_expand(t_len)
        nsp = _round_up(-(-t_len // SEL_LEN), 8)
        ovt = ov[:, :nsp].T
        ext = ex[:nsp, :t_len].T
        kk_ = jnp.arange(KT)[:, None]
        qq_ = jnp.arange(KT)[None, :]

        def band(dist):
            rows = dist.shape[0]
            t_ = _bias_table(rel_bias, dist, True).reshape(NSA_KV, NSA_QPG, rows, KT)
            return jnp.transpose(t_, (0, 2, 1, 3)).reshape(NSA_KV, rows, NSA_QPG * KT)

        tbt = jnp.stack([band(KT + qq_ - kk_), band(qq_ - kk_)], axis=1)
        cj = jnp.arange(2 * KT // CMP_STRIDE)[:, None] * CMP_STRIDE + (CMP_LEN - 1)
        bct = jnp.stack([band(KT + qq_ - cj), band(qq_ - cj)], axis=1)
    else:
        pt = st["page_table"]
        past = pt.shape[1] * PAGE
        wb = st["win"].shape[2]
        t_all = past + t_len
        ov, ex = _overlap_expand(t_all)
        mcmp = t_all // CMP_STRIDE
        kp = past + LANE
        wp = wb + LANE
        qp = past + jnp.arange(t_len)[:, None]
        cend = jnp.arange(mcmp)[None, :] * CMP_STRIDE + (CMP_LEN - 1)
        rows_ = NSA_QPG * t_len
        bc = _bias_table(rel_bias, qp - cend, False).reshape(NSA_KV, rows_, mcmp)
        bs = _bias_table(rel_bias, qp - jnp.arange(kp)[None, :], True).reshape(NSA_KV, rows_, kp)
        bw = _bias_table(rel_bias, qp - (past - wb + jnp.arange(wp)[None, :]), True).reshape(NSA_KV, rows_, wp)
        pool_cmp = jnp.swapaxes(st["cmp"].reshape(st["cmp"].shape[0], st["cmp"].shape[1], PAGE, KV_LANES), 2, 3)
        pool_slc = jnp.swapaxes(st["slc"].reshape(st["slc"].shape[0], st["slc"].shape[1], PAGE, KV_LANES), 2, 3)
        win_all = st["win"].reshape(b, st["win"].shape[1], wb, KV_LANES)
        win_all_t = jnp.swapaxes(win_all, 2, 3)

    cmp_r, slc_r, win_r, wkv_r, sh_r, conv_r, ffn_r = [], [], [], [], [], [], []
    for l in range(4):
        li = l // 2
        if l % 2 == 0:
            w_in = prm["w_in_even"][li].T
            w_nsa = jnp.pad(w_in[:NSA_PROJ], ((0, NSA_PROJ_PAD - NSA_PROJ), (0, 0))).astype(BF16)
            w_rw = w_in[NSA_PROJ:].astype(BF16)
            qg = jnp.tile(prm["q_norm"][li], NSA_HEADS).reshape(1, 512)
            kg = jnp.stack([jnp.tile(prm["k_norm"][li, 1], NSA_KV), jnp.tile(prm["k_norm"][li, 2], NSA_KV)])
            qn, kv3, z_nsa, z_rw = _even_in(x, prm["norm_mix"][l], w_nsa, w_rw, qg, kg,
                                            g512.astype(BF16), g128.astype(BF16), tm)
            cw = _compress_weights(prm["phi_pe"][li], prm["phi_w1"][li], prm["phi_w2"][li],
                                   prm["k_norm"][li, 0])
            kv3b = kv3.reshape(b, t_len, 768)
            if prompt:
                cc = _compress(kv3b, *cw, g128)
                nsa = _nsa_prompt(qn, z_nsa, cc, kv3, bct, tbt, ovt, ext, b, t_len)
                z_prev = jnp.zeros((b, RWKV_PROJ), F32)
                s0 = jnp.zeros((b, RWKV_HEADS, HEAD_DIM, HEAD_DIM), F32)
                n_keep = min(WINDOW, t_len)
                win_rows = kv3b[:, t_len - n_keep:, 512:768]
            else:
                nsa, win_new_t = _nsa_paged(pt, pool_cmp, pool_slc, li, kv3, win_all_t, qn, z_nsa,
                                            cw + (g128,), bc, bs, bw, ov, ex, b, t_len)
                z_prev = st["shift"][:, li]
                s0 = st["wkv"][:, li]
                n_keep = WINDOW
                win_rows = jnp.swapaxes(win_new_t, 1, 2)
            pr = (prm["rwkv_mu"][li].reshape(1, -1), prm["rwkv_w0"][li].reshape(1, -1), prm["rwkv_w2"][li],
                  prm["rwkv_a0"][li].reshape(1, -1), prm["rwkv_a2"][li], prm["rwkv_g2"][li],
                  prm["rwkv_kk"][li].reshape(1, -1), prm["rwkv_ka"][li].reshape(1, -1),
                  prm["rwkv_rk"][li].reshape(1, -1), prm["rwkv_ln_w"][li].reshape(1, -1),
                  prm["rwkv_ln_b"][li].reshape(1, -1))
            nb_rw = max(n_ for n_ in ((2, 1) if prompt else (8, 4, 2, 1)) if b % n_ == 0)
            rw, s_t = _rwkv(z_rw, z_prev, s0, pr, b, t_len, c_len, nb_rw)
            w_out = prm["w_out_even"][li].astype(BF16)
            if prompt:
                x = _matmul_res_t(nsa, rw.reshape(m_rows, 512), w_out, x, tm, 512)
            else:
                mix = jnp.concatenate([nsa.reshape(m_rows, 512), rw.reshape(m_rows, 512)], axis=1)
                x = _matmul_res(mix, w_out, x, tm, 512)
            cmp_r.append(kv3b[:, :, 0:256].reshape(b, t_len, 2, NSA_KV, HEAD_DIM))
            slc_r.append(kv3b[:, :, 256:512].reshape(b, t_len, 2, NSA_KV, HEAD_DIM))
            win_r.append(win_rows.reshape(b, n_keep, 2, NSA_KV, HEAD_DIM))
            wkv_r.append(s_t)
            sh_r.append(z_rw.reshape(b, t_len, RWKV_PROJ)[:, -1])
        else:
            prev = jnp.zeros((b, 2, D_MODEL), F32) if prompt else st["conv"][:, li]
            x, cs = _fused_block(x, prm["norm_mix"][l], prm["w_in_odd"][li].astype(BF16),
                                 prm["w_out_odd"][li].astype(BF16), prm["conv_w"][li], prev, "odd", t_len, tm_f)
            conv_r.append(cs)
        prev = jnp.zeros((b, 2, D_FF), F32) if prompt else st["ffn"][:, l]
        x, fs = _fused_block(x, prm["norm_ffn"][l], prm["ffn_up"][l].astype(BF16),
                             prm["ffn_down"][l].astype(BF16), prm["ffn_conv"][l], prev, "ffn", t_len, tm_f)
        ffn_r.append(fs)
    stk = lambda a_: jnp.stack(a_, axis=1)
    return x.reshape(b, t_len, d), (stk(cmp_r), stk(slc_r), stk(win_r), stk(wkv_r), stk(sh_r),
                                    stk(conv_r), stk(ffn_r))


def kernel(x_prompt, x_sample, cache_cmp_kv, cache_slc_kv, cache_win_kv, state_rwkv_wkv, state_rwkv_shift, state_conv, state_ffn_conv, page_table, norm_mix, norm_ffn, rel_bias, w_in_even, w_out_even, q_norm, k_norm, phi_pe, phi_w1, phi_w2, rwkv_mu, rwkv_w0, rwkv_w2, rwkv_a0, rwkv_a2, rwkv_g2, rwkv_kk, rwkv_ka, rwkv_rk, rwkv_ln_w, rwkv_ln_b, w_in_odd, conv_w, w_out_odd, ffn_up, ffn_conv, ffn_down):
    prm = dict(norm_mix=norm_mix, norm_ffn=norm_ffn, rel_bias=rel_bias, w_in_even=w_in_even,
               w_out_even=w_out_even, q_norm=q_norm, k_norm=k_norm, phi_pe=phi_pe, phi_w1=phi_w1,
               phi_w2=phi_w2, rwkv_mu=rwkv_mu, rwkv_w0=rwkv_w0, rwkv_w2=rwkv_w2, rwkv_a0=rwkv_a0,
               rwkv_a2=rwkv_a2, rwkv_g2=rwkv_g2, rwkv_kk=rwkv_kk, rwkv_ka=rwkv_ka, rwkv_rk=rwkv_rk,
               rwkv_ln_w=rwkv_ln_w, rwkv_ln_b=rwkv_ln_b, w_in_odd=w_in_odd, conv_w=conv_w,
               w_out_odd=w_out_odd, ffn_up=ffn_up, ffn_conv=ffn_conv, ffn_down=ffn_down)
    st = dict(cmp=cache_cmp_kv, slc=cache_slc_kv, win=cache_win_kv, wkv=state_rwkv_wkv,
              shift=state_rwkv_shift, conv=state_conv, ffn=state_ffn_conv, page_table=page_table)
    y_p, (cmp_p, slc_p, win_p, wkv_p, sh_p, conv_p, ffn_p) = _trunk(x_prompt, prm, None)
    y_s, (cmp_s, slc_s, win_s, wkv_s, sh_s, conv_s, ffn_s) = _trunk(x_sample, prm, st)
    return (y_p, y_s, cmp_p, cmp_s, slc_p, slc_s, win_p, win_s, wkv_p, wkv_s,
            sh_p, sh_s, conv_p, conv_s, ffn_p, ffn_s)
```

```python
import functools
import math

import jax
import jax.numpy as jnp
from jax import lax
from jax.experimental import pallas as pl
from jax.experimental.pallas import tpu as pltpu

F32 = jnp.float32
BF16 = jnp.bfloat16
HI = lax.Precision.HIGHEST

D_MODEL = 1024
HEAD_DIM = 64
NSA_HEADS = 8
NSA_KV = 2
NSA_QPG = 4
NSA_WIDTH = 512
KV_LANES = 2 * NSA_KV * HEAD_DIM
CMP_LEN = 32
CMP_STRIDE = 16
CMP_HIDDEN = 128
SEL_LEN = 64
N_SELECT = 16
WINDOW = 512
FORCE_SCORE = 1e9
RWKV_HEADS = 8
RWKV_WIDTH = 512
DECAY_LORA = 64
AAA_LORA = 64
GATE_LORA = 128
RWKV_PROJ = 3 * RWKV_WIDTH + DECAY_LORA + AAA_LORA + GATE_LORA
NSA_PROJ = NSA_WIDTH + 6 * NSA_KV * HEAD_DIM + 3 * NSA_HEADS
NSA_PROJ_PAD = 1408
D_FF = 2816
NUM_BUCKETS = 32
MAX_DISTANCE = 128
RMS_EPS = 1e-6
GN_EPS = 64e-5
PAGE = 128
LANE = 128
KT = 128
NEG = -1e30
VMEM_LIMIT = 56 * 1024 * 1024


def _cp(*sem):
    return pltpu.CompilerParams(dimension_semantics=sem, vmem_limit_bytes=VMEM_LIMIT)


def _round_up(a, b):
    return (a + b - 1) // b * b


def _nt(a, b, precision=None):
    return lax.dot_general(a, b, (((1,), (1,)), ((), ())), precision=precision,
                           preferred_element_type=F32)


def _tn(a, b, precision=None):
    return lax.dot_general(a, b, (((0,), (0,)), ((), ())), precision=precision,
                           preferred_element_type=F32)


_NN = ((1,), (0,))
_NT = ((1,), (1,))
_TN = ((0,), (0,))
RWKV_MM = "bf16"
RWKV_LORA_MM = "bf16"


def _mmp(a, b, dims, mode=None):
    mode = RWKV_MM if mode is None else mode
    dn = (dims, ((), ()))
    if mode == "f32":
        return lax.dot_general(a, b, dn, precision=HI, preferred_element_type=F32)
    if mode == "bf16":
        return lax.dot_general(a.astype(BF16), b.astype(BF16), dn, preferred_element_type=F32)
    ah = a.astype(BF16)
    al = (a - ah.astype(F32)).astype(BF16)
    bh = b.astype(BF16)
    bl = (b - bh.astype(F32)).astype(BF16)
    dot = lambda x, y: lax.dot_general(x, y, dn, preferred_element_type=F32)
    return dot(ah, bh) + (dot(ah, bl) + dot(al, bh))


def _mm_res_kernel(a_ref, w_ref, r_ref, o_ref):
    o_ref[...] = r_ref[...] + jnp.dot(a_ref[...], w_ref[...], preferred_element_type=F32)


def _matmul_res(a, w, res, tm, tn):
    m, k = a.shape
    n = w.shape[1]
    return pl.pallas_call(
        _mm_res_kernel, name="mm_res",
        grid=(m // tm, n // tn),
        in_specs=[pl.BlockSpec((tm, k), lambda i, j: (i, 0)),
                  pl.BlockSpec((k, tn), lambda i, j: (0, j)),
                  pl.BlockSpec((tm, tn), lambda i, j: (i, j))],
        out_specs=pl.BlockSpec((tm, tn), lambda i, j: (i, j)),
        out_shape=jax.ShapeDtypeStruct((m, n), F32),
        compiler_params=_cp("arbitrary", "arbitrary"),
    )(a, w, res)


def _mm_res_t_kernel(at_ref, b_ref, wa_ref, wb_ref, r_ref, o_ref):
    o_ref[...] = (r_ref[...] + _tn(at_ref[0], wa_ref[...])
                  + jnp.dot(b_ref[...], wb_ref[...], preferred_element_type=F32))


def _matmul_res_t(a_t, bmat, w, res, tm, tn):
    nb, ka, t_len = a_t.shape
    m, kb = bmat.shape
    n = w.shape[1]
    tpb = t_len // tm
    return pl.pallas_call(
        _mm_res_t_kernel, name="mm_res_t",
        grid=(m // tm, n // tn),
        in_specs=[pl.BlockSpec((1, ka, tm), lambda i, j: (i // tpb, 0, i % tpb)),
                  pl.BlockSpec((tm, kb), lambda i, j: (i, 0)),
                  pl.BlockSpec((ka, tn), lambda i, j: (0, j)),
                  pl.BlockSpec((kb, tn), lambda i, j: (ka // kb, j)),
                  pl.BlockSpec((tm, tn), lambda i, j: (i, j))],
        out_specs=pl.BlockSpec((tm, tn), lambda i, j: (i, j)),
        out_shape=jax.ShapeDtypeStruct((m, n), F32),
        compiler_params=_cp("arbitrary", "arbitrary"),
    )(a_t, bmat, w, w, res)


FUSED_CHUNK = 256


def _fused_kernel(x_ref, g_ref, win_ref, wout_ref, cw_ref, prev_ref, o_ref, st_ref, carry_ref,
                  *, mode, nbr, seq_tiles, nseq, lt):
    i = pl.program_id(0)
    x = x_ref[...]
    ms = jnp.mean(x * x, axis=-1, keepdims=True)
    xn = (x * lax.rsqrt(ms + RMS_EPS) * g_ref[...]).astype(BF16)
    f = cw_ref.shape[-1]
    ch = FUSED_CHUNK
    nchunk = f // ch
    if seq_tiles == 1:
        p = prev_ref[...]
    else:
        p = jnp.where(i % seq_tiles == 0, prev_ref[...], carry_ref[...])
    t = lax.broadcasted_iota(jnp.int32, (nseq, lt, ch), 1)

    def project(c):
        return [jnp.dot(xn, win_ref[:, r * f + c * ch:r * f + (c + 1) * ch], preferred_element_type=F32)
                for r in range(nbr)]

    def gate(c, br):
        cs = slice(c * ch, (c + 1) * ch)
        if mode == "ffn":
            cin, other = br
        else:
            other, cg, xi = br
            cin = cg * xi
        c3 = cin.reshape(nseq, lt, ch)
        p0 = p[:, 0:1, cs]
        p1 = p[:, 1:2, cs]
        c1 = jnp.where(t == 0, p1, pltpu.roll(c3, 1, 1))
        c2 = jnp.where(t == 0, p0, jnp.where(t == 1, p1, pltpu.roll(c3, 2, 1)))
        y = c2 * cw_ref[0:1, cs] + c1 * cw_ref[1:2, cs] + c3 * cw_ref[2:3, cs]
        last2 = c3[:, lt - 2:lt, :]
        st_ref[:, :, cs] = last2
        if seq_tiles > 1:
            carry_ref[:, :, cs] = last2
        o3 = other.reshape(c3.shape)
        out = (y * jax.nn.sigmoid(y)) * o3 if mode == "ffn" else o3 * y
        return out.reshape(nseq * lt, ch).astype(BF16)

    acc = jnp.zeros(x.shape, F32)
    br = project(0)
    for c in range(nchunk):
        br_next = project(c + 1) if c + 1 < nchunk else None
        gch = gate(c, br)
        acc = acc + jnp.dot(gch, wout_ref[c * ch:(c + 1) * ch, :], preferred_element_type=F32)
        br = br_next
    o_ref[...] = x + acc


def _fused_block(x, g, w_in, w_out, cw, prev, mode, seq_len, tm):
    m, d = x.shape
    nbr = 2 if mode == "ffn" else 3
    f = w_in.shape[1] // nbr
    if seq_len >= tm:
        seq_tiles, nseq, lt = seq_len // tm, 1, tm
    else:
        seq_tiles, nseq, lt = 1, tm // seq_len, seq_len
    b = prev.shape[0]
    kern = functools.partial(_fused_kernel, mode=mode, nbr=nbr, seq_tiles=seq_tiles, nseq=nseq, lt=lt)
    return pl.pallas_call(
        kern, name="fused_" + mode,
        grid=(m // tm,),
        in_specs=[pl.BlockSpec((tm, d), lambda i: (i, 0)),
                  pl.BlockSpec((1, d), lambda i: (0, 0)),
                  pl.BlockSpec((d, nbr * f), lambda i: (0, 0), pipeline_mode=pl.Buffered(1)),
                  pl.BlockSpec((f, d), lambda i: (0, 0), pipeline_mode=pl.Buffered(1)),
                  pl.BlockSpec((3, f), lambda i: (0, 0)),
                  pl.BlockSpec((nseq, 2, f), lambda i: (i // seq_tiles, 0, 0))],
        out_specs=[pl.BlockSpec((tm, d), lambda i: (i, 0)),
                   pl.BlockSpec((nseq, 2, f), lambda i: (i // seq_tiles, 0, 0))],
        out_shape=[jax.ShapeDtypeStruct((m, d), F32),
                   jax.ShapeDtypeStruct((b, 2, f), F32)],
        scratch_shapes=[pltpu.VMEM((nseq, 2, f), F32)],
        compiler_params=_cp("arbitrary"),
    )(x, g.reshape(1, d), w_in, w_out, cw, prev)


def _group_mean(x2, g_ref):
    hi = x2.astype(BF16)
    r1 = x2 - hi.astype(F32)
    mid = r1.astype(BF16)
    lo = (r1 - mid.astype(F32)).astype(BF16)
    g = g_ref[...]
    return (jnp.dot(hi, g, preferred_element_type=F32)
            + (jnp.dot(mid, g, preferred_element_type=F32) + jnp.dot(lo, g, preferred_element_type=F32)))


def _even_in_kernel(x_ref, g_ref, wn_ref, wr_ref, qg_ref, kg_ref, g512_ref, g128_ref,
                    q_ref, kv_ref, gt_ref, zrw_ref):
    x = x_ref[...]
    ms = jnp.mean(x * x, axis=-1, keepdims=True)
    xn = (x * lax.rsqrt(ms + RMS_EPS) * g_ref[...]).astype(BF16)
    zrw_ref[...] = _nt(xn, wr_ref[...])
    z = _nt(xn, wn_ref[...])
    q = z[:, 0:NSA_WIDTH]
    q_ref[...] = (q * lax.rsqrt(_group_mean(q * q, g512_ref) + RMS_EPS) * qg_ref[...]
                  * (HEAD_DIM ** -0.5)).astype(BF16)
    kv_ref[:, 0:256] = z[:, 512:768]
    for r, off in ((0, 768), (1, 1024)):
        k = z[:, off:off + 128]
        kv_ref[:, off - 512:off - 384] = (k * lax.rsqrt(_group_mean(k * k, g128_ref) + RMS_EPS)
                                          * kg_ref[r:r + 1, :])
        kv_ref[:, off - 384:off - 256] = z[:, off + 128:off + 256]
    gt_ref[...] = z[:, 1280:NSA_PROJ_PAD]


def _even_in(x, g, wn_t, wr_t, qg, kg, g512, g128, tm):
    m, d = x.shape
    c2 = lambda i: (0, 0)
    row = lambda n: pl.BlockSpec((tm, n), lambda i: (i, 0))
    return pl.pallas_call(
        _even_in_kernel, name="even_in",
        grid=(m // tm,),
        in_specs=[row(d), pl.BlockSpec((1, d), c2),
                  pl.BlockSpec((NSA_PROJ_PAD, d), c2, pipeline_mode=pl.Buffered(1)),
                  pl.BlockSpec((RWKV_PROJ, d), c2, pipeline_mode=pl.Buffered(1)),
                  pl.BlockSpec((1, 512), c2), pl.BlockSpec((2, 128), c2),
                  pl.BlockSpec((512, 512), c2), pl.BlockSpec((128, 128), c2)],
        out_specs=[row(512), row(768), row(LANE), row(RWKV_PROJ)],
        out_shape=[jax.ShapeDtypeStruct((m, 512), BF16), jax.ShapeDtypeStruct((m, 768), F32),
                   jax.ShapeDtypeStruct((m, LANE), F32), jax.ShapeDtypeStruct((m, RWKV_PROJ), F32)],
        compiler_params=_cp("arbitrary"),
    )(x, g.reshape(1, d), wn_t, wr_t, qg, kg, g512, g128)


def _compress_rows(rows_refs, m, pe_ref, w1_ref, w2_ref, kn_ref, g128_ref):
    outs = []
    for kv, rows_ref in enumerate(rows_refs):
        acc0 = jnp.zeros((m, NSA_KV * CMP_HIDDEN), F32)
        acc1 = jnp.zeros((m, NSA_KV * CMP_HIDDEN), F32)
        for j in range(CMP_STRIDE):
            xj = rows_ref[pl.ds(j, m, stride=CMP_STRIDE), :]
            acc0 = acc0 + jnp.dot((xj + pe_ref[kv, j:j + 1, :]).astype(BF16), w1_ref[kv, j],
                                  preferred_element_type=F32)
            acc1 = acc1 + jnp.dot((xj + pe_ref[kv, CMP_STRIDE + j:CMP_STRIDE + j + 1, :]).astype(BF16),
                                  w1_ref[kv, CMP_STRIDE + j], preferred_element_type=F32)
        hid = acc0 + pltpu.roll(acc1, m - 1, 0)
        outs.append(jnp.dot(jax.nn.gelu(hid).astype(BF16), w2_ref[kv], preferred_element_type=F32))
    kc, vc = outs
    ms = jnp.dot(kc * kc, g128_ref[...], precision=HI, preferred_element_type=F32)
    kc = kc * lax.rsqrt(ms + RMS_EPS) * kn_ref[...]
    return jnp.concatenate([kc, vc], axis=1)


def _compress_kernel(k_ref, v_ref, pe_ref, w1_ref, w2_ref, kn_ref, g128_ref, o_ref, *, m):
    o_ref[0] = _compress_rows((k_ref.at[0], v_ref.at[0]), m, pe_ref, w1_ref, w2_ref, kn_ref, g128_ref)


def _compress(kv, pe, w1, w2, kn, g128):
    b, t, _ = kv.shape
    m = t // CMP_STRIDE
    c2 = lambda i: (0, 0)
    hw = NSA_KV * HEAD_DIM
    return pl.pallas_call(
        functools.partial(_compress_kernel, m=m), name="compress",
        grid=(b,),
        in_specs=[pl.BlockSpec((1, t, hw), lambda i: (i, 0, 0)),
                  pl.BlockSpec((1, t, hw), lambda i: (i, 0, 1)),
                  pl.BlockSpec((2, CMP_LEN, hw), lambda i: (0, 0, 0)),
                  pl.BlockSpec((2, CMP_LEN, hw, NSA_KV * CMP_HIDDEN), lambda i: (0, 0, 0, 0)),
                  pl.BlockSpec((2, NSA_KV * CMP_HIDDEN, hw), lambda i: (0, 0, 0)),
                  pl.BlockSpec((1, 128), c2), pl.BlockSpec((128, 128), c2)],
        out_specs=pl.BlockSpec((1, m, KV_LANES), lambda i: (i, 0, 0)),
        out_shape=jax.ShapeDtypeStruct((b, m, KV_LANES), F32),
        compiler_params=_cp("arbitrary"),
    )(kv, kv, pe, w1, w2, kn, g128)


def _msoftmax(s, mask):
    s = jnp.where(mask, s, NEG)
    mx = jnp.max(s, axis=-1, keepdims=True)
    e = jnp.where(mask, jnp.exp(s - mx), 0.0)
    den = jnp.sum(e, axis=-1, keepdims=True)
    return e * (1.0 / jnp.where(den > 0, den, 1.0))


def _select_blocks(imp, qpos, ns):
    s_io = lax.broadcasted_iota(jnp.int32, imp.shape, 1)
    cur = qpos // SEL_LEN
    forced = (s_io == 0) | (s_io == cur) | (s_io == cur - 1)
    future = s_io * SEL_LEN > qpos
    imp = jnp.where(forced, FORCE_SCORE, imp)
    imp = jnp.where(future, -FORCE_SCORE, imp)
    imp = jnp.where(s_io >= ns, -3e38, imp)
    rank = jnp.zeros(imp.shape, jnp.int32)
    for sp in range(ns):
        col = imp[:, sp:sp + 1]
        beats = (col > imp) | ((col == imp) & (s_io > sp))
        rank = rank + beats.astype(jnp.int32)
    sel = (rank < min(N_SELECT, ns)) & (s_io < ns)
    return sel.astype(F32)


def _heads_to_rows(q, g):
    return jnp.concatenate(
        [q[:, HEAD_DIM * (NSA_QPG * g + h):HEAD_DIM * (NSA_QPG * g + h + 1)] for h in range(NSA_QPG)],
        axis=0)


def _msoftmax0(s, mask):
    s = jnp.where(mask, s, NEG)
    mx = jnp.max(s, axis=0, keepdims=True)
    e = jnp.where(mask, jnp.exp(s - mx), 0.0)
    den = jnp.sum(e, axis=0, keepdims=True)
    return e * (1.0 / jnp.where(den > 0, den, 1.0))


def _select_blocks_t(imp, qpos, ns):
    s_io = lax.broadcasted_iota(jnp.int32, imp.shape, 0)
    cur = qpos // SEL_LEN
    forced = (s_io == 0) | (s_io == cur) | (s_io == cur - 1)
    future = s_io * SEL_LEN > qpos
    imp = jnp.where(forced, FORCE_SCORE, imp)
    imp = jnp.where(future, -FORCE_SCORE, imp)
    imp = jnp.where(s_io >= ns, -3e38, imp)
    rank = jnp.zeros(imp.shape, jnp.int32)
    for sp in range(ns):
        row = imp[sp:sp + 1, :]
        beats = (row > imp) | ((row == imp) & (s_io > sp))
        rank = rank + beats.astype(jnp.int32)
    sel = (rank < min(N_SELECT, ns)) & (s_io < ns)
    return sel.astype(F32)


def _flash_steps_t(chains):
    sts = [_nt(k, qg) for (qg, k, _, _, _, _) in chains]
    mids = []
    for st, (_, _, _, bias, mask, (m_, l_, _)) in zip(sts, chains):
        ps, ms, ls, als = [], [], [], []
        for h in range(NSA_QPG):
            hs = slice(KT * h, KT * (h + 1))
            s = st[:, hs]
            if bias is not None:
                s = s + bias[:, hs]
            s = jnp.where(mask, s, NEG)
            m_new = jnp.maximum(m_[:, hs], jnp.max(s, axis=0, keepdims=True))
            alpha = jnp.exp(m_[:, hs] - m_new)
            e = jnp.where(mask, jnp.exp(s - m_new), 0.0)
            ls.append(alpha * l_[:, hs] + jnp.sum(e, axis=0, keepdims=True))
            ms.append(m_new)
            als.append(alpha)
            ps.append(e.astype(BF16))
        mids.append((jnp.concatenate(ps, axis=1), jnp.concatenate(ms, axis=1),
                     jnp.concatenate(ls, axis=1), jnp.concatenate(als, axis=1)))
    out = []
    for (p, m_new, l_new, alpha), (_, _, v, _, _, (_, _, acc)) in zip(mids, chains):
        out.append((m_new, l_new, alpha * acc + _tn(v, p)))
    return out


def _flash_init_t():
    return (jnp.full((1, NSA_QPG * KT), NEG, F32), jnp.zeros((1, NSA_QPG * KT), F32),
            jnp.zeros((HEAD_DIM, NSA_QPG * KT), F32))


def _flash_out_t(carry):
    _, l_, acc = carry
    return acc * (1.0 / jnp.where(l_ > 0, l_, 1.0))


def _nsa_prompt_kernel(q_ref, gt_ref, cc_ref, kv_ref, bc_ref, tb_ref, ovt_ref, ext_ref, o_ref, selk_s, sc_s,
                       *, m, ns):
    i = pl.program_id(1)
    q = q_ref[0]
    gst = jax.nn.sigmoid(gt_ref[...]).T
    k_io = lax.broadcasted_iota(jnp.int32, (KT, KT), 0)
    q_io = lax.broadcasted_iota(jnp.int32, (KT, KT), 1)
    qpos = i * KT + lax.broadcasted_iota(jnp.int32, (1, KT), 1)
    n_io = lax.broadcasted_iota(jnp.int32, (m, KT), 0)
    mask_c = ((n_io * CMP_STRIDE + (CMP_LEN - 1)) <= qpos) & (n_io < m - 1)
    cc = cc_ref[0]
    groups = range(NSA_KV)
    qgs = [_heads_to_rows(q, g) for g in groups]

    kcs = [cc[:, HEAD_DIM * g:HEAD_DIM * (g + 1)].astype(BF16) for g in groups]
    vcs = [cc[:, 128 + HEAD_DIM * g:128 + HEAD_DIM * (g + 1)].astype(BF16) for g in groups]
    c_off = pl.multiple_of(jnp.maximum(i - 1, 0) * (KT // CMP_STRIDE), KT // CMP_STRIDE)
    sts = []
    for g in groups:
        sc_s[g] = _nt(kcs[g], qgs[g])
        band = jnp.where(i == 0, bc_ref[g, 1], bc_ref[g, 0])
        sc_s[g, pl.ds(c_off, 2 * KT // CMP_STRIDE), :] += band
        sts.append(sc_s[g])
    pcs = [[_msoftmax0(sts[g][:, KT * h:KT * (h + 1)], mask_c) for h in range(NSA_QPG)]
           for g in groups]
    o_cs = [_tn(vcs[g], jnp.concatenate(pcs[g], axis=1).astype(BF16)) for g in groups]
    imps = [jnp.dot(ovt_ref[...], pcs[g][0] + pcs[g][1] + pcs[g][2] + pcs[g][3], precision=HI,
                    preferred_element_type=F32) for g in groups]
    sels = [_select_blocks_t(imps[g], qpos, ns).astype(BF16) for g in groups]
    for g in groups:
        selk_s[g] = jnp.dot(ext_ref[...], sels[g], preferred_element_type=F32)

    def load_kv(g, kt, off, ntile=1):
        r0 = pl.multiple_of(kt * KT, KT)
        k_lo = 256 + off + HEAD_DIM * g
        v_lo = 384 + off + HEAD_DIM * g
        k = kv_ref[0, pl.ds(r0, ntile * KT), k_lo:k_lo + HEAD_DIM].astype(BF16)
        v = kv_ref[0, pl.ds(r0, ntile * KT), v_lo:v_lo + HEAD_DIM].astype(BF16)
        return k, v

    def sel_mask(g, kt, ntile=1):
        r0 = pl.multiple_of(kt * KT, KT)
        return selk_s[g, pl.ds(r0, ntile * KT), :] > 0.5

    n_main = jnp.maximum(i - 1, 0)

    def body(pr, carry):
        chains = []
        for g in groups:
            k, v = load_kv(g, 2 * pr, 0, 2)
            chains.append((qgs[g], k, v, None, sel_mask(g, 2 * pr, 2), carry[g]))
        return tuple(_flash_steps_t(chains))

    c_sel = list(lax.fori_loop(0, n_main // 2, body, tuple(_flash_init_t() for _ in groups)))
    kt_odd = (n_main // 2) * 2
    has_odd = n_main % 2 == 1

    nwt = WINDOW // KT
    c_win = [_flash_init_t() for _ in groups]
    for d in range(nwt, -1, -1):
        kt = i - d
        ktc = jnp.maximum(kt, 0)
        if d == nwt:
            mk = (k_io > q_io) & (kt >= 0)
        elif d == 0:
            mk = k_io <= q_io
        else:
            mk = kt >= 0
        chains = []
        for g in groups:
            k, v = load_kv(g, ktc, 256)
            bias = tb_ref[g, 0] if d == 1 else (tb_ref[g, 1] if d == 0 else None)
            chains.append((qgs[g], k, v, bias, mk, c_win[g]))
        if d == nwt:
            for g in groups:
                k, v = load_kv(g, kt_odd, 0)
                chains.append((qgs[g], k, v, None, sel_mask(g, kt_odd) & has_odd, c_sel[g]))
        elif d <= 1:
            for g in groups:
                k, v = load_kv(g, ktc, 0)
                smk = sel_mask(g, ktc) & ((kt >= 0) if d == 1 else (k_io <= q_io))
                chains.append((qgs[g], k, v, tb_ref[g, 1 - d], smk, c_sel[g]))
        res = _flash_steps_t(chains)
        c_win = res[:NSA_KV]
        if len(res) > NSA_KV:
            c_sel = res[NSA_KV:]
    o_ss = [_flash_out_t(c_sel[g]) for g in groups]
    o_ws = [_flash_out_t(c_win[g]) for g in groups]

    for g in groups:
        for h in range(NSA_QPG):
            hh = NSA_QPG * g + h
            hs = slice(KT * h, KT * (h + 1))
            o = (gst[3 * hh:3 * hh + 1, :] * o_cs[g][:, hs] + gst[3 * hh + 1:3 * hh + 2, :] * o_ss[g][:, hs]
                 + gst[3 * hh + 2:3 * hh + 3, :] * o_ws[g][:, hs])
            o_ref[0, HEAD_DIM * hh:HEAD_DIM * (hh + 1), :] = o.astype(BF16)


def _nsa_prompt(qn, z_nsa, cc, kv, bct, tbt, ovt, ext, b, t_len):
    m = cc.shape[1]
    ns = -(-t_len // SEL_LEN)
    nq = t_len // KT
    nsp = ovt.shape[0]
    c2 = lambda bi, i: (0, 0)
    return pl.pallas_call(
        functools.partial(_nsa_prompt_kernel, m=m, ns=ns), name="nsa_prompt",
        grid=(b, nq),
        in_specs=[pl.BlockSpec((1, KT, 512), lambda bi, i: (bi, i, 0)),
                  pl.BlockSpec((KT, LANE), lambda bi, i: (bi * nq + i, 0)),
                  pl.BlockSpec((1, m, KV_LANES), lambda bi, i: (bi, 0, 0)),
                  pl.BlockSpec((1, t_len, 768), lambda bi, i: (bi, 0, 0)),
                  pl.BlockSpec((NSA_KV, 2, 2 * KT // CMP_STRIDE, NSA_QPG * KT), lambda bi, i: (0, 0, 0, 0)),
                  pl.BlockSpec((NSA_KV, 2, KT, NSA_QPG * KT), lambda bi, i: (0, 0, 0, 0)),
                  pl.BlockSpec((nsp, m), c2),
                  pl.BlockSpec((t_len, nsp), c2)],
        out_specs=pl.BlockSpec((1, 512, KT), lambda bi, i: (bi, 0, i)),
        out_shape=jax.ShapeDtypeStruct((b, 512, t_len), BF16),
        scratch_shapes=[pltpu.VMEM((NSA_KV, t_len, KT), F32),
                        pltpu.VMEM((NSA_KV, m, NSA_QPG * KT), F32)],
        compiler_params=_cp("arbitrary", "arbitrary"),
    )(qn.reshape(b, t_len, 512), z_nsa, cc, kv.reshape(b, t_len, 768), bct, tbt, ovt, ext)


def _nsa_paged_kernel(pt_ref, cmp_hbm, slc_hbm, new_ref, win_ref, q_ref, gt_ref, pe_ref, w1_ref, w2_ref,
                      kn_ref, g128_ref, bc_ref, bs_ref, bw_ref, ov_ref, ex_ref, o_ref, wo_ref,
                      cmpt_s, slct_s, cmpk_s, cmpv_s, sems, *, li, nsteps, nbr, npages, past, wb, tq, m, ns):
    step = pl.program_id(0)
    slot = step % 2

    def page_copy(st_, bb, p, sl, which):
        hbm, buf = ((cmp_hbm, cmpt_s), (slc_hbm, slct_s))[which]
        return pltpu.make_async_copy(hbm.at[pt_ref[st_ * nbr + bb, p], li],
                                     buf.at[sl, bb, :, pl.ds(p * PAGE, PAGE)], sems.at[sl, which])

    def start_step(st_, sl):
        for bb in range(nbr):
            for p in range(npages):
                page_copy(st_, bb, p, sl, 0).start()
                page_copy(st_, bb, p, sl, 1).start()

    @pl.when(step == 0)
    def _():
        start_step(0, 0)

    @pl.when(step + 1 < nsteps)
    def _():
        start_step(step + 1, 1 - slot)

    for bb in range(nbr):
        for p in range(npages):
            page_copy(step, bb, p, slot, 0).wait()
            page_copy(step, bb, p, slot, 1).wait()

    hw = NSA_KV * HEAD_DIM
    for bb in range(nbr):
        for p in range(npages):
            blk = cmpt_s[slot, bb, :, p * PAGE:(p + 1) * PAGE]
            r0 = bb * past + p * PAGE
            cmpk_s[r0:r0 + PAGE, :] = blk[0:hw].T
            cmpv_s[r0:r0 + PAGE, :] = blk[hw:2 * hw].T
    cc_all = _compress_rows((cmpk_s, cmpv_s), nbr * m, pe_ref, w1_ref, w2_ref, kn_ref, g128_ref)

    pairs = [(bb, g) for bb in range(nbr) for g in range(NSA_KV)]
    groups = range(len(pairs))
    ccs = [cc_all[m * bb:m * (bb + 1)] for bb in range(nbr)]
    newt32 = [jnp.concatenate([new_ref[bb], jnp.zeros((LANE - tq, 768), F32)], axis=0).T
              for bb in range(nbr)]
    newts = [n_.astype(BF16) for n_ in newt32]
    for bb in range(nbr):
        wo_ref[bb] = jnp.concatenate([win_ref[bb, 0, :, wb + tq - WINDOW:wb],
                                      newt32[bb][2 * KV_LANES:3 * KV_LANES, 0:tq]], axis=1)
    gss = [jax.nn.sigmoid(gt_ref[bb]) for bb in range(nbr)]
    kp = past + LANE
    wp = wb + LANE
    qpos = past + lax.broadcasted_iota(jnp.int32, (tq, 1), 0)
    n_io = lax.broadcasted_iota(jnp.int32, (tq, m), 1)
    mask_c = ((n_io * CMP_STRIDE + (CMP_LEN - 1)) <= qpos) & (n_io < m - 1)
    k_io = lax.broadcasted_iota(jnp.int32, (tq, kp), 1)
    mask_s0 = (k_io <= qpos) & (k_io < past + tq)
    j_io = lax.broadcasted_iota(jnp.int32, (tq, wp), 1)
    dist_w = qpos - (past - wb + j_io)
    mask_w = (dist_w >= 0) & (dist_w < WINDOW) & (j_io < wb + tq)
    rep = lambda a_: jnp.concatenate([a_] * NSA_QPG, axis=0)
    mask_c4, mask_w4 = rep(mask_c), rep(mask_w)
    gsl = lambda g, off: slice(off + HEAD_DIM * g, off + HEAD_DIM * (g + 1))
    qgs = [_heads_to_rows(q_ref[bb], g) for bb, g in pairs]
    bof = [bb for bb, _ in pairs]
    gof = [g for _, g in pairs]

    s_c = [_nt(qgs[i], ccs[bof[i]][:, gsl(gof[i], 0)].astype(BF16)) + bc_ref[gof[i]] for i in groups]
    p_c = [_msoftmax(s_c[i], mask_c4) for i in groups]
    o_c = [jnp.dot(p_c[i].astype(BF16), ccs[bof[i]][:, gsl(gof[i], hw)].astype(BF16),
                   preferred_element_type=F32) for i in groups]
    imps = [jnp.dot(p_c[i][0:tq] + p_c[i][tq:2 * tq] + p_c[i][2 * tq:3 * tq] + p_c[i][3 * tq:4 * tq],
                    ov_ref[...], precision=HI, preferred_element_type=F32) for i in groups]
    sels = [_select_blocks(imps[i], qpos, ns).astype(BF16) for i in groups]
    selk = [jnp.dot(sels[i], ex_ref[...], preferred_element_type=F32) for i in groups]

    s_s = [jnp.concatenate(
        [jnp.dot(qgs[i], slct_s[slot, bof[i], gsl(gof[i], 0), :].astype(BF16), preferred_element_type=F32),
         jnp.dot(qgs[i], newts[bof[i]][gsl(gof[i], 256)], preferred_element_type=F32)], axis=1)
        + bs_ref[gof[i]] for i in groups]
    p_s = [_msoftmax(s_s[i], rep((selk[i] > 0.5) & mask_s0)) for i in groups]
    o_s = [_nt(p_s[i][:, 0:past].astype(BF16), slct_s[slot, bof[i], gsl(gof[i], hw), :].astype(BF16))
           + _nt(p_s[i][:, past:kp].astype(BF16), newts[bof[i]][gsl(gof[i], 256 + hw)]) for i in groups]

    s_w = [jnp.concatenate(
        [jnp.dot(qgs[i], win_ref[bof[i], 0, gsl(gof[i], 0), :].astype(BF16), preferred_element_type=F32),
         jnp.dot(qgs[i], newts[bof[i]][gsl(gof[i], 512)], preferred_element_type=F32)], axis=1)
        + bw_ref[gof[i]] for i in groups]
    p_w = [_msoftmax(s_w[i], mask_w4) for i in groups]
    o_w = [_nt(p_w[i][:, 0:wb].astype(BF16), win_ref[bof[i], 0, gsl(gof[i], hw), :].astype(BF16))
           + _nt(p_w[i][:, wb:wp].astype(BF16), newts[bof[i]][gsl(gof[i], 512 + hw)]) for i in groups]

    for bb in range(nbr):
        outs = []
        gs = gss[bb]
        for g in range(NSA_KV):
            i = bb * NSA_KV + g
            for h in range(NSA_QPG):
                hh = NSA_QPG * g + h
                rs = slice(tq * h, tq * (h + 1))
                outs.append(gs[:, 3 * hh:3 * hh + 1] * o_c[i][rs] + gs[:, 3 * hh + 1:3 * hh + 2] * o_s[i][rs]
                            + gs[:, 3 * hh + 2:3 * hh + 3] * o_w[i][rs])
        o_ref[bb] = jnp.concatenate(outs, axis=1).astype(BF16)


def _nsa_paged(pt, pool_cmp_t, pool_slc_t, li, kv_new, win_t, qn, z_nsa, cw, bc, bs, bw, ov, ex, b, tq):
    npages = pt.shape[1]
    past = npages * PAGE
    wb = win_t.shape[3]
    assert past % CMP_STRIDE == 0 and tq < CMP_STRIDE
    assert wb + tq >= WINDOW
    kp = past + LANE
    wp = wb + LANE
    m = (past + tq) // CMP_STRIDE
    ns = -(-(past + tq) // SEL_LEN)
    rows = NSA_QPG * tq
    pe, w1, w2, kn, g128 = cw
    hw = NSA_KV * HEAD_DIM
    c2 = lambda bi, pt_: (0, 0)
    c3 = lambda bi, pt_: (0, 0, 0)
    per_b3 = lambda bi, pt_: (bi, 0, 0)
    nbr = 2 if b % 2 == 0 else 1
    grid_spec = pltpu.PrefetchScalarGridSpec(
        num_scalar_prefetch=1,
        grid=(b // nbr,),
        in_specs=[pl.BlockSpec(memory_space=pl.ANY),
                  pl.BlockSpec(memory_space=pl.ANY),
                  pl.BlockSpec((nbr, tq, 768), per_b3),
                  pl.BlockSpec((nbr, 1, KV_LANES, wb), lambda bi, pt_: (bi, li, 0, 0)),
                  pl.BlockSpec((nbr, tq, 512), per_b3),
                  pl.BlockSpec((nbr, tq, LANE), lambda bi, pt_: (bi, 0, 0)),
                  pl.BlockSpec((2, CMP_LEN, hw), c3),
                  pl.BlockSpec((2, CMP_LEN, hw, NSA_KV * CMP_HIDDEN), lambda bi, pt_: (0, 0, 0, 0)),
                  pl.BlockSpec((2, NSA_KV * CMP_HIDDEN, hw), c3),
                  pl.BlockSpec((1, 128), c2), pl.BlockSpec((128, 128), c2),
                  pl.BlockSpec((NSA_KV, rows, m), c3),
                  pl.BlockSpec((NSA_KV, rows, kp), c3),
                  pl.BlockSpec((NSA_KV, rows, wp), c3),
                  pl.BlockSpec((m, LANE), c2),
                  pl.BlockSpec((LANE, kp), c2)],
        out_specs=[pl.BlockSpec((nbr, tq, 512), per_b3),
                   pl.BlockSpec((nbr, KV_LANES, WINDOW), per_b3)],
        scratch_shapes=[pltpu.VMEM((2, nbr, KV_LANES, past), F32), pltpu.VMEM((2, nbr, KV_LANES, past), F32),
                        pltpu.VMEM((nbr * past, hw), F32), pltpu.VMEM((nbr * past, hw), F32),
                        pltpu.SemaphoreType.DMA((2, 2))])
    kern = functools.partial(_nsa_paged_kernel, li=li, nsteps=b // nbr, nbr=nbr, npages=npages, past=past,
                             wb=wb, tq=tq, m=m, ns=ns)
    return pl.pallas_call(
        kern, grid_spec=grid_spec, name="nsa_paged",
        out_shape=[jax.ShapeDtypeStruct((b, tq, 512), BF16),
                   jax.ShapeDtypeStruct((b, KV_LANES, WINDOW), F32)],
        compiler_params=_cp("arbitrary"),
    )(pt, pool_cmp_t, pool_slc_t, kv_new.reshape(b, tq, 768), win_t, qn.reshape(b, tq, 512),
      z_nsa.reshape(b, tq, LANE), pe, w1, w2, kn, g128, bc, bs, bw, ov, ex)


def _rwkv_kernel(z_ref, zp_ref, s0_ref, mu_ref, w0_ref, w2_ref, a0_ref, a2_ref, g2_ref, kkp_ref, ka_ref,
                 rk_ref, lnw_ref, lnb_ref, y_ref, st_ref, s_scr, carry_scr, *, c_len, nb):
    c = pl.program_id(1)
    cl = c_len
    nr = nb * cl

    @pl.when(c == 0)
    def _():
        s_scr[...] = s0_ref[...]
        carry_scr[...] = zp_ref[...]

    z3 = z_ref[...]
    row = lax.broadcasted_iota(jnp.int32, z3.shape, 1)
    shifted = jnp.where(row == 0, carry_scr[...], pltpu.roll(z3, 1, 1))
    carry_scr[...] = z3[:, cl - 1:cl, :]
    zz = (z3 + (shifted - z3) * mu_ref[...]).reshape(nr, RWKV_PROJ)
    w = RWKV_WIDTH
    r = zz[:, 0:w]
    k = zz[:, w:2 * w]
    v = zz[:, 2 * w:3 * w]
    zw = zz[:, 3 * w:3 * w + DECAY_LORA]
    za = zz[:, 3 * w + DECAY_LORA:3 * w + DECAY_LORA + AAA_LORA]
    zg = zz[:, 3 * w + DECAY_LORA + AAA_LORA:]
    xw = -(w0_ref[...] + _mmp(jnp.tanh(zw), w2_ref[...], _NN, RWKV_LORA_MM))
    softplus = jnp.maximum(xw, 0.0) + jnp.log(1.0 + jnp.exp(-jnp.abs(xw)))
    logdec = -jnp.exp(-softplus - 0.5)
    a = jax.nn.sigmoid(a0_ref[...] + _mmp(za, a2_ref[...], _NN, RWKV_LORA_MM))
    gate = _mmp(jax.nn.sigmoid(zg), g2_ref[...], _NN, RWKV_LORA_MM)
    k_mod = k * (1.0 + (a - 1.0) * ka_ref[...])
    kku = k * kkp_ref[...]

    ti = lax.broadcasted_iota(jnp.int32, (nr, nr), 0)
    si = lax.broadcasted_iota(jnp.int32, (nr, nr), 1)
    tri = ((si <= ti) & (si >= (ti // cl) * cl)).astype(BF16)
    ld_hi = logdec.astype(BF16)
    ld_r = logdec - ld_hi.astype(F32)
    ld_mid = ld_r.astype(BF16)
    ld_lo = (ld_r - ld_mid.astype(F32)).astype(BF16)
    gcum = (jnp.dot(tri, ld_hi, preferred_element_type=F32)
            + (jnp.dot(tri, ld_mid, preferred_element_type=F32)
               + jnp.dot(tri, ld_lo, preferred_element_type=F32)))
    e_g_all = jnp.exp(gcum)
    e_gm_all = jnp.exp(gcum - logdec)
    e_ng_all = jnp.exp(-gcum)
    nsteps = max(1, int(math.ceil(math.log2(cl))))
    row2 = lax.broadcasted_iota(jnp.int32, (cl, 2 * cl), 0)
    col2 = lax.broadcasted_iota(jnp.int32, (cl, 2 * cl), 1)
    col2 = jnp.where(col2 >= cl, col2 - cl, col2)
    strict2 = col2 < row2
    lower2 = col2 <= row2
    keep_z = lax.broadcasted_iota(jnp.int32, (cl, cl + HEAD_DIM), 1) >= cl

    pairs = [(bb, hd) for bb in range(nb) for hd in range(RWKV_HEADS)]
    heads = range(len(pairs))
    rws = [slice(cl * bb, cl * (bb + 1)) for bb, _ in pairs]
    sls = [slice(HEAD_DIM * hd, HEAD_DIM * (hd + 1)) for _, hd in pairs]
    kks = [kku[rws[h], sls[h]] for h in heads]
    kks = [kk * lax.rsqrt(jnp.maximum(jnp.sum(kk * kk, axis=-1, keepdims=True), 1e-24)) for kk in kks]
    kkas = [kks[h] * a[rws[h], sls[h]] for h in heads]
    kms = [k_mod[rws[h], sls[h]] for h in heads]
    vhs = [v[rws[h], sls[h]] for h in heads]
    rhs_ = [r[rws[h], sls[h]] for h in heads]
    s_hs = [s_scr[bb, hd] for bb, hd in pairs]
    e_g = [e_g_all[rws[h], sls[h]] for h in heads]
    e_gm = [e_gm_all[rws[h], sls[h]] for h in heads]
    e_ng = [e_ng_all[rws[h], sls[h]] for h in heads]
    glast = [gcum[cl * (bb + 1) - 1:cl * (bb + 1), sls[h]] for h, (bb, _) in enumerate(pairs)]
    e_lg = [jnp.exp(glast[h] - gcum[rws[h], sls[h]]) for h in heads]
    e_l = [jnp.exp(glast[h]) for h in heads]
    lrs = [jnp.concatenate([-kks[h] * e_gm[h], rhs_[h] * e_g[h]], axis=0) for h in heads]
    rrs = [jnp.concatenate([kkas[h] * e_ng[h], kms[h] * e_ng[h]], axis=0) for h in heads]
    m1s = [_mmp(lrs[h], rrs[h], _NT) for h in heads]
    m2s = [_mmp(lrs[h], s_hs[h], _NT) for h in heads]
    tops = [jnp.where(strict2, m1[0:cl], 0.0) for m1 in m1s]
    bots = [jnp.where(lower2, m1[cl:2 * cl], 0.0) for m1 in m1s]
    zeros_v = jnp.zeros((cl, HEAD_DIM), F32)
    rhss = [m2s[h][0:cl] + _mmp(tops[h], jnp.concatenate([zeros_v, vhs[h]], axis=0), _NN) for h in heads]
    wms = [jnp.concatenate([tops[h][:, 0:cl], rhss[h]], axis=1) for h in heads]
    for _ in range(nsteps):
        wms = [_mmp(w_[:, 0:cl], w_, _NN) + jnp.where(keep_z, w_, 0.0) for w_ in wms]
    zvs = [jnp.concatenate([wms[h][:, cl:cl + HEAD_DIM], vhs[h]], axis=0) for h in heads]
    ys = [m2s[h][cl:2 * cl] + _mmp(bots[h], zvs[h], _NN) for h in heads]
    bkhs = [jnp.concatenate([kkas[h] * e_lg[h], kms[h] * e_lg[h]], axis=0) for h in heads]
    s_new = [s_hs[h] * e_l[h] + _mmp(zvs[h], bkhs[h], _TN) for h in heads]
    for h, (bb, hd) in enumerate(pairs):
        s_scr[bb, hd] = s_new[h]
    outs = []
    for h in heads:
        y = ys[h]
        sl = sls[h]
        mean = jnp.mean(y, axis=-1, keepdims=True)
        yc = y - mean
        var = jnp.mean(yc * yc, axis=-1, keepdims=True)
        yn = yc * lax.rsqrt(var + GN_EPS) * lnw_ref[:, sl] + lnb_ref[:, sl]
        bonus = jnp.sum(rhs_[h] * kms[h] * rk_ref[:, sl], axis=-1, keepdims=True) * vhs[h]
        outs.append(yn + bonus)
    for bb in range(nb):
        yb = jnp.concatenate(outs[RWKV_HEADS * bb:RWKV_HEADS * (bb + 1)], axis=1)
        y_ref[bb] = (yb * gate[cl * bb:cl * (bb + 1)]).astype(BF16)
    st_ref[...] = s_scr[...]


def _rwkv(z_rw, z_prev, s0, pr, b, t_len, c_len, nb):
    nchunk = t_len // c_len
    c2 = lambda bi, c: (0, 0)
    vec = lambda n: pl.BlockSpec((1, n), c2)
    return pl.pallas_call(
        functools.partial(_rwkv_kernel, c_len=c_len, nb=nb), name="rwkv",
        grid=(b // nb, nchunk),
        in_specs=[pl.BlockSpec((nb, c_len, RWKV_PROJ), lambda bi, c: (bi, c, 0)),
                  pl.BlockSpec((nb, 1, RWKV_PROJ), lambda bi, c: (bi, 0, 0)),
                  pl.BlockSpec((nb, RWKV_HEADS, HEAD_DIM, HEAD_DIM), lambda bi, c: (bi, 0, 0, 0)),
                  vec(RWKV_PROJ), vec(512), pl.BlockSpec((DECAY_LORA, 512), c2),
                  vec(512), pl.BlockSpec((AAA_LORA, 512), c2), pl.BlockSpec((GATE_LORA, 512), c2),
                  vec(512), vec(512), vec(512), vec(512), vec(512)],
        out_specs=[pl.BlockSpec((nb, c_len, 512), lambda bi, c: (bi, c, 0)),
                   pl.BlockSpec((nb, RWKV_HEADS, HEAD_DIM, HEAD_DIM), lambda bi, c: (bi, 0, 0, 0))],
        out_shape=[jax.ShapeDtypeStruct((b, t_len, 512), BF16),
                   jax.ShapeDtypeStruct((b, RWKV_HEADS, HEAD_DIM, HEAD_DIM), F32)],
        scratch_shapes=[pltpu.VMEM((nb, RWKV_HEADS, HEAD_DIM, HEAD_DIM), F32),
                        pltpu.VMEM((nb, 1, RWKV_PROJ), F32)],
        compiler_params=_cp("arbitrary", "arbitrary"),
    )(z_rw.reshape(b, t_len, RWKV_PROJ), z_prev.reshape(b, 1, RWKV_PROJ), s0, *pr)


def _rel_bucket(dist):
    n = jnp.maximum(dist, 0)
    max_exact = NUM_BUCKETS // 2
    nf = jnp.maximum(n, 1).astype(F32)
    large = max_exact + (jnp.log(nf / max_exact) / math.log(MAX_DISTANCE / max_exact)
                         * (NUM_BUCKETS - max_exact)).astype(jnp.int32)
    large = jnp.minimum(large, NUM_BUCKETS - 1)
    return jnp.where(n < max_exact, n, large)


def _bias_table(rel_bias, dist, delta):
    tb = rel_bias.astype(F32)
    out = jnp.moveaxis(tb[_rel_bucket(dist)], -1, 0)
    if delta:
        out = out - tb[NUM_BUCKETS - 1].reshape((NSA_HEADS,) + (1,) * dist.ndim)
    return out


def _toeplitz(u, nrows, ncols, base, step):
    lo = base - step * (nrows - 1)
    ln = ncols + step * (nrows - 1)
    lead = u.shape[:-1]
    w = jnp.broadcast_to(u[..., None, lo:lo + ln], lead + (nrows, ln)).reshape(lead + (nrows * ln,))
    w = jnp.pad(w, [(0, 0)] * len(lead) + [(0, nrows * step)]).reshape(lead + (nrows, ln + step))
    return w[..., ::-1, :ncols]


def _block_diag_ones(n, grp):
    idx = jnp.arange(n) // grp
    return (idx[:, None] == idx[None, :]).astype(F32) / grp


def _compress_weights(phi_pe, phi_w1, phi_w2, kn_cmp):
    pe = jnp.concatenate([phi_pe, phi_pe], axis=-1)
    w1 = jnp.zeros((2, CMP_LEN, NSA_KV * HEAD_DIM, NSA_KV * CMP_HIDDEN), F32)
    w2 = jnp.zeros((2, NSA_KV * CMP_HIDDEN, NSA_KV * HEAD_DIM), F32)
    for g in range(NSA_KV):
        w1 = w1.at[:, :, HEAD_DIM * g:HEAD_DIM * (g + 1), CMP_HIDDEN * g:CMP_HIDDEN * (g + 1)].set(phi_w1)
        w2 = w2.at[:, CMP_HIDDEN * g:CMP_HIDDEN * (g + 1), HEAD_DIM * g:HEAD_DIM * (g + 1)].set(phi_w2)
    kn = jnp.tile(kn_cmp, NSA_KV).reshape(1, 128)
    return pe, w1.astype(BF16), w2.astype(BF16), kn


def _overlap_expand(t_all):
    nc = t_all // CMP_STRIDE - 1
    m = nc + 1
    ns = -(-t_all // SEL_LEN)
    c0 = jnp.arange(m)[:, None] * CMP_STRIDE
    s0 = jnp.arange(LANE)[None, :] * SEL_LEN
    ov = ((c0 < s0 + SEL_LEN) & (c0 + CMP_LEN > s0) & (jnp.arange(m)[:, None] < nc)
          & (jnp.arange(LANE)[None, :] < ns)).astype(F32)
    kp = _round_up(t_all, LANE)
    ex = (jnp.arange(kp)[None, :] // SEL_LEN == jnp.arange(LANE)[:, None]).astype(BF16)
    return ov, ex


def _pick_tm(t_len):
    return 512 if t_len % 512 == 0 else 128


def _trunk(x3, prm, st):
    b, t_len, d = x3.shape
    m_rows = b * t_len
    x = x3.reshape(m_rows, d)
    prompt = st is None
    if prompt:
        tm = _pick_tm(t_len)
        c_len = 64
    else:
        tm = 256 if m_rows % 256 == 0 else m_rows
        c_len = t_len
    tm_f = tm
    g512 = _block_diag_ones(512, HEAD_DIM)
    g128 = _block_diag_ones(128, HEAD_DIM)
    rel_bias = prm["rel_bias"]

    if prompt:
        ov, ex = _overlap_expand(t_len)
        nsp = _round_up(-(-t_len // SEL_LEN), 8)
        ovt = ov[:, :nsp].T
        ext = ex[:nsp, :t_len].T
        off = 3 * KT
        u = _bias_table(rel_bias, jnp.arange(-off, 2 * KT), True)

        def band(rows, base, step):
            t_ = _toeplitz(u, rows, KT, off + base, step).reshape(NSA_KV, NSA_QPG, rows, KT)
            return jnp.transpose(t_, (0, 2, 1, 3)).reshape(NSA_KV, rows, NSA_QPG * KT)

        tbt = jnp.stack([band(KT, KT, 1), band(KT, 0, 1)], axis=1)
        ncb = 2 * KT // CMP_STRIDE
        bct = jnp.stack([band(ncb, KT - (CMP_LEN - 1), CMP_STRIDE),
                         band(ncb, -(CMP_LEN - 1), CMP_STRIDE)], axis=1)
    else:
        pt = st["page_table"]
        past = pt.shape[1] * PAGE
        wb = st["win"].shape[2]
        t_all = past + t_len
        ov, ex = _overlap_expand(t_all)
        mcmp = t_all // CMP_STRIDE
        kp = past + LANE
        wp = wb + LANE
        qp = past + jnp.arange(t_len)[:, None]
        cend = jnp.arange(mcmp)[None, :] * CMP_STRIDE + (CMP_LEN - 1)
        rows_ = NSA_QPG * t_len
        bc = _bias_table(rel_bias, qp - cend, False).reshape(NSA_KV, rows_, mcmp)
        near = _bias_table(rel_bias, jnp.arange(-LANE, 2 * LANE), True)[:, ::-1]

        def by_distance(d_top, ncols):
            u_r = jnp.concatenate([jnp.zeros((NSA_HEADS, d_top - (2 * LANE - 1)), F32), near], axis=1)
            return _toeplitz(u_r, t_len, ncols, t_len - 1, 1).reshape(NSA_KV, rows_, ncols)

        bs = by_distance(past + t_len - 1, kp)
        bw = by_distance(wb + t_len - 1, wp)
        pool_cmp = jnp.swapaxes(st["cmp"].reshape(st["cmp"].shape[0], st["cmp"].shape[1], PAGE, KV_LANES), 2, 3)
        pool_slc = jnp.swapaxes(st["slc"].reshape(st["slc"].shape[0], st["slc"].shape[1], PAGE, KV_LANES), 2, 3)
        win_all = st["win"].reshape(b, st["win"].shape[1], wb, KV_LANES)
        win_all_t = jnp.swapaxes(win_all, 2, 3)

    cmp_r, slc_r, win_r, wkv_r, sh_r, conv_r, ffn_r = [], [], [], [], [], [], []
    for l in range(4):
        li = l // 2
        if l % 2 == 0:
            w_in = prm["w_in_even"][li].T
            w_nsa = jnp.pad(w_in[:NSA_PROJ], ((0, NSA_PROJ_PAD - NSA_PROJ), (0, 0))).astype(BF16)
            w_rw = w_in[NSA_PROJ:].astype(BF16)
            qg = jnp.tile(prm["q_norm"][li], NSA_HEADS).reshape(1, 512)
            kg = jnp.stack([jnp.tile(prm["k_norm"][li, 1], NSA_KV), jnp.tile(prm["k_norm"][li, 2], NSA_KV)])
            qn, kv3, z_nsa, z_rw = _even_in(x, prm["norm_mix"][l], w_nsa, w_rw, qg, kg,
                                            g512.astype(BF16), g128.astype(BF16), tm)
            cw = _compress_weights(prm["phi_pe"][li], prm["phi_w1"][li], prm["phi_w2"][li],
                                   prm["k_norm"][li, 0])
            kv3b = kv3.reshape(b, t_len, 768)
            if prompt:
                cc = _compress(kv3b, *cw, g128)
                nsa = _nsa_prompt(qn, z_nsa, cc, kv3, bct, tbt, ovt, ext, b, t_len)
                z_prev = jnp.zeros((b, RWKV_PROJ), F32)
                s0 = jnp.zeros((b, RWKV_HEADS, HEAD_DIM, HEAD_DIM), F32)
                n_keep = min(WINDOW, t_len)
                win_rows = kv3b[:, t_len - n_keep:, 512:768]
            else:
                nsa, win_new_t = _nsa_paged(pt, pool_cmp, pool_slc, li, kv3, win_all_t, qn, z_nsa,
                                            cw + (g128,), bc, bs, bw, ov, ex, b, t_len)
                z_prev = st["shift"][:, li]
                s0 = st["wkv"][:, li]
                n_keep = WINDOW
                win_rows = jnp.swapaxes(win_new_t, 1, 2)
            pr = (prm["rwkv_mu"][li].reshape(1, -1), prm["rwkv_w0"][li].reshape(1, -1), prm["rwkv_w2"][li],
                  prm["rwkv_a0"][li].reshape(1, -1), prm["rwkv_a2"][li], prm["rwkv_g2"][li],
                  prm["rwkv_kk"][li].reshape(1, -1), prm["rwkv_ka"][li].reshape(1, -1),
                  prm["rwkv_rk"][li].reshape(1, -1), prm["rwkv_ln_w"][li].reshape(1, -1),
                  prm["rwkv_ln_b"][li].reshape(1, -1))
            nb_rw = max(n_ for n_ in ((2, 1) if prompt else (8, 4, 2, 1)) if b % n_ == 0)
            rw, s_t = _rwkv(z_rw, z_prev, s0, pr, b, t_len, c_len, nb_rw)
            w_out = prm["w_out_even"][li].astype(BF16)
            if prompt:
                x = _matmul_res_t(nsa, rw.reshape(m_rows, 512), w_out, x, tm, 512)
            else:
                mix = jnp.concatenate([nsa.reshape(m_rows, 512), rw.reshape(m_rows, 512)], axis=1)
                x = _matmul_res(mix, w_out, x, tm, 512)
            cmp_r.append(kv3b[:, :, 0:256].reshape(b, t_len, 2, NSA_KV, HEAD_DIM))
            slc_r.append(kv3b[:, :, 256:512].reshape(b, t_len, 2, NSA_KV, HEAD_DIM))
            win_r.append(win_rows.reshape(b, n_keep, 2, NSA_KV, HEAD_DIM))
            wkv_r.append(s_t)
            sh_r.append(z_rw.reshape(b, t_len, RWKV_PROJ)[:, -1])
        else:
            prev = jnp.zeros((b, 2, D_MODEL), F32) if prompt else st["conv"][:, li]
            x, cs = _fused_block(x, prm["norm_mix"][l], prm["w_in_odd"][li].astype(BF16),
                                 prm["w_out_odd"][li].astype(BF16), prm["conv_w"][li], prev, "odd", t_len, tm_f)
            conv_r.append(cs)
        prev = jnp.zeros((b, 2, D_FF), F32) if prompt else st["ffn"][:, l]
        x, fs = _fused_block(x, prm["norm_ffn"][l], prm["ffn_up"][l].astype(BF16),
                             prm["ffn_down"][l].astype(BF16), prm["ffn_conv"][l], prev, "ffn", t_len, tm_f)
        ffn_r.append(fs)
    stk = lambda a_: jnp.stack(a_, axis=1)
    return x.reshape(b, t_len, d), (stk(cmp_r), stk(slc_r), stk(win_r), stk(wkv_r), stk(sh_r),
                                    stk(conv_r), stk(ffn_r))


def kernel(x_prompt, x_sample, cache_cmp_kv, cache_slc_kv, cache_win_kv, state_rwkv_wkv, state_rwkv_shift, state_conv, state_ffn_conv, page_table, norm_mix, norm_ffn, rel_bias, w_in_even, w_out_even, q_norm, k_norm, phi_pe, phi_w1, phi_w2, rwkv_mu, rwkv_w0, rwkv_w2, rwkv_a0, rwkv_a2, rwkv_g2, rwkv_kk, rwkv_ka, rwkv_rk, rwkv_ln_w, rwkv_ln_b, w_in_odd, conv_w, w_out_odd, ffn_up, ffn_conv, ffn_down):
    prm = dict(norm_mix=norm_mix, norm_ffn=norm_ffn, rel_bias=rel_bias, w_in_even=w_in_even,
               w_out_even=w_out_even, q_norm=q_norm, k_norm=k_norm, phi_pe=phi_pe, phi_w1=phi_w1,
               phi_w2=phi_w2, rwkv_mu=rwkv_mu, rwkv_w0=rwkv_w0, rwkv_w2=rwkv_w2, rwkv_a0=rwkv_a0,
               rwkv_a2=rwkv_a2, rwkv_g2=rwkv_g2, rwkv_kk=rwkv_kk, rwkv_ka=rwkv_ka, rwkv_rk=rwkv_rk,
               rwkv_ln_w=rwkv_ln_w, rwkv_ln_b=rwkv_ln_b, w_in_odd=w_in_odd, conv_w=conv_w,
               w_out_odd=w_out_odd, ffn_up=ffn_up, ffn_conv=ffn_conv, ffn_down=ffn_down)
    st = dict(cmp=cache_cmp_kv, slc=cache_slc_kv, win=cache_win_kv, wkv=state_rwkv_wkv,
              shift=state_rwkv_shift, conv=state_conv, ffn=state_ffn_conv, page_table=page_table)
    y_p, (cmp_p, slc_p, win_p, wkv_p, sh_p, conv_p, ffn_p) = _trunk(x_prompt, prm, None)
    y_s, (cmp_s, slc_s, win_s, wkv_s, sh_s, conv_s, ffn_s) = _trunk(x_sample, prm, st)
    return (y_p, y_s, cmp_p, cmp_s, slc_p, slc_s, win_p, win_s, wkv_p, wkv_s,
            sh_p, sh_s, conv_p, conv_s, ffn_p, ffn_s)
```

```python
import functools
import math

import jax
import jax.numpy as jnp
from jax import lax
from jax.experimental import pallas as pl
from jax.experimental.pallas import tpu as pltpu

F32 = jnp.float32
BF16 = jnp.bfloat16
HI = lax.Precision.HIGHEST

D_MODEL = 1024
HEAD_DIM = 64
NSA_HEADS = 8
NSA_KV = 2
NSA_QPG = 4
NSA_WIDTH = 512
KV_LANES = 2 * NSA_KV * HEAD_DIM
CMP_LEN = 32
CMP_STRIDE = 16
CMP_HIDDEN = 128
SEL_LEN = 64
N_SELECT = 16
WINDOW = 512
FORCE_SCORE = 1e9
RWKV_HEADS = 8
RWKV_WIDTH = 512
DECAY_LORA = 64
AAA_LORA = 64
GATE_LORA = 128
RWKV_PROJ = 3 * RWKV_WIDTH + DECAY_LORA + AAA_LORA + GATE_LORA
NSA_PROJ = NSA_WIDTH + 6 * NSA_KV * HEAD_DIM + 3 * NSA_HEADS
NSA_PROJ_PAD = 1408
D_FF = 2816
NUM_BUCKETS = 32
MAX_DISTANCE = 128
RMS_EPS = 1e-6
GN_EPS = 64e-5
PAGE = 128
LANE = 128
KT = 128
NEG = -1e30
VMEM_LIMIT = 56 * 1024 * 1024


def _cp(*sem):
    return pltpu.CompilerParams(dimension_semantics=sem, vmem_limit_bytes=VMEM_LIMIT)


def _round_up(a, b):
    return (a + b - 1) // b * b


def _nt(a, b, precision=None):
    return lax.dot_general(a, b, (((1,), (1,)), ((), ())), precision=precision,
                           preferred_element_type=F32)


def _tn(a, b, precision=None):
    return lax.dot_general(a, b, (((0,), (0,)), ((), ())), precision=precision,
                           preferred_element_type=F32)


_NN = ((1,), (0,))
_NT = ((1,), (1,))
_TN = ((0,), (0,))
RWKV_MM = "bf16"
RWKV_LORA_MM = "bf16"


def _mmp(a, b, dims, mode=None):
    mode = RWKV_MM if mode is None else mode
    dn = (dims, ((), ()))
    if mode == "f32":
        return lax.dot_general(a, b, dn, precision=HI, preferred_element_type=F32)
    if mode == "bf16":
        return lax.dot_general(a.astype(BF16), b.astype(BF16), dn, preferred_element_type=F32)
    ah = a.astype(BF16)
    al = (a - ah.astype(F32)).astype(BF16)
    bh = b.astype(BF16)
    bl = (b - bh.astype(F32)).astype(BF16)
    dot = lambda x, y: lax.dot_general(x, y, dn, preferred_element_type=F32)
    return dot(ah, bh) + (dot(ah, bl) + dot(al, bh))


def _mm_res_kernel(a_ref, w_ref, r_ref, o_ref):
    o_ref[...] = r_ref[...] + jnp.dot(a_ref[...], w_ref[...], preferred_element_type=F32)


def _matmul_res(a, w, res, tm, tn):
    m, k = a.shape
    n = w.shape[1]
    return pl.pallas_call(
        _mm_res_kernel, name="mm_res",
        grid=(m // tm, n // tn),
        in_specs=[pl.BlockSpec((tm, k), lambda i, j: (i, 0)),
                  pl.BlockSpec((k, tn), lambda i, j: (0, j)),
                  pl.BlockSpec((tm, tn), lambda i, j: (i, j))],
        out_specs=pl.BlockSpec((tm, tn), lambda i, j: (i, j)),
        out_shape=jax.ShapeDtypeStruct((m, n), F32),
        compiler_params=_cp("arbitrary", "arbitrary"),
    )(a, w, res)


def _mm_res_t_kernel(at_ref, b_ref, wa_ref, wb_ref, r_ref, o_ref):
    o_ref[...] = (r_ref[...] + _tn(at_ref[0], wa_ref[...])
                  + jnp.dot(b_ref[...], wb_ref[...], preferred_element_type=F32))


def _matmul_res_t(a_t, bmat, w, res, tm, tn):
    nb, ka, t_len = a_t.shape
    m, kb = bmat.shape
    n = w.shape[1]
    tpb = t_len // tm
    return pl.pallas_call(
        _mm_res_t_kernel, name="mm_res_t",
        grid=(m // tm, n // tn),
        in_specs=[pl.BlockSpec((1, ka, tm), lambda i, j: (i // tpb, 0, i % tpb)),
                  pl.BlockSpec((tm, kb), lambda i, j: (i, 0)),
                  pl.BlockSpec((ka, tn), lambda i, j: (0, j)),
                  pl.BlockSpec((kb, tn), lambda i, j: (ka // kb, j)),
                  pl.BlockSpec((tm, tn), lambda i, j: (i, j))],
        out_specs=pl.BlockSpec((tm, tn), lambda i, j: (i, j)),
        out_shape=jax.ShapeDtypeStruct((m, n), F32),
        compiler_params=_cp("arbitrary", "arbitrary"),
    )(a_t, bmat, w, w, res)


FUSED_CHUNK = 256


def _fused_kernel(x_ref, g_ref, win_ref, wout_ref, cw_ref, prev_ref, o_ref, st_ref, carry_ref,
                  *, mode, nbr, seq_tiles, nseq, lt):
    i = pl.program_id(0)
    x = x_ref[...]
    ms = jnp.mean(x * x, axis=-1, keepdims=True)
    xn = (x * lax.rsqrt(ms + RMS_EPS) * g_ref[...]).astype(BF16)
    f = cw_ref.shape[-1]
    ch = FUSED_CHUNK
    nchunk = f // ch
    if seq_tiles == 1:
        p = prev_ref[...]
    else:
        p = jnp.where(i % seq_tiles == 0, prev_ref[...], carry_ref[...])
    t = lax.broadcasted_iota(jnp.int32, (nseq, lt, ch), 1)

    def project(c):
        return [jnp.dot(xn, win_ref[:, r * f + c * ch:r * f + (c + 1) * ch], preferred_element_type=F32)
                for r in range(nbr)]

    def gate(c, br):
        cs = slice(c * ch, (c + 1) * ch)
        if mode == "ffn":
            cin, other = br
        else:
            other, cg, xi = br
            cin = cg * xi
        c3 = cin.reshape(nseq, lt, ch)
        p0 = p[:, 0:1, cs]
        p1 = p[:, 1:2, cs]
        c1 = jnp.where(t == 0, p1, pltpu.roll(c3, 1, 1))
        c2 = jnp.where(t == 0, p0, jnp.where(t == 1, p1, pltpu.roll(c3, 2, 1)))
        y = c2 * cw_ref[0:1, cs] + c1 * cw_ref[1:2, cs] + c3 * cw_ref[2:3, cs]
        last2 = c3[:, lt - 2:lt, :]
        st_ref[:, :, cs] = last2
        if seq_tiles > 1:
            carry_ref[:, :, cs] = last2
        o3 = other.reshape(c3.shape)
        out = (y * jax.nn.sigmoid(y)) * o3 if mode == "ffn" else o3 * y
        return out.reshape(nseq * lt, ch).astype(BF16)

    acc = jnp.zeros(x.shape, F32)
    br = project(0)
    for c in range(nchunk):
        br_next = project(c + 1) if c + 1 < nchunk else None
        gch = gate(c, br)
        acc = acc + jnp.dot(gch, wout_ref[c * ch:(c + 1) * ch, :], preferred_element_type=F32)
        br = br_next
    o_ref[...] = x + acc


def _fused_block(x, g, w_in, w_out, cw, prev, mode, seq_len, tm):
    m, d = x.shape
    nbr = 2 if mode == "ffn" else 3
    f = w_in.shape[1] // nbr
    if seq_len >= tm:
        seq_tiles, nseq, lt = seq_len // tm, 1, tm
    else:
        seq_tiles, nseq, lt = 1, tm // seq_len, seq_len
    b = prev.shape[0]
    kern = functools.partial(_fused_kernel, mode=mode, nbr=nbr, seq_tiles=seq_tiles, nseq=nseq, lt=lt)
    return pl.pallas_call(
        kern, name="fused_" + mode,
        grid=(m // tm,),
        in_specs=[pl.BlockSpec((tm, d), lambda i: (i, 0)),
                  pl.BlockSpec((1, d), lambda i: (0, 0)),
                  pl.BlockSpec((d, nbr * f), lambda i: (0, 0), pipeline_mode=pl.Buffered(1)),
                  pl.BlockSpec((f, d), lambda i: (0, 0), pipeline_mode=pl.Buffered(1)),
                  pl.BlockSpec((3, f), lambda i: (0, 0)),
                  pl.BlockSpec((nseq, 2, f), lambda i: (i // seq_tiles, 0, 0))],
        out_specs=[pl.BlockSpec((tm, d), lambda i: (i, 0)),
                   pl.BlockSpec((nseq, 2, f), lambda i: (i // seq_tiles, 0, 0))],
        out_shape=[jax.ShapeDtypeStruct((m, d), F32),
                   jax.ShapeDtypeStruct((b, 2, f), F32)],
        scratch_shapes=[pltpu.VMEM((nseq, 2, f), F32)],
        compiler_params=_cp("arbitrary"),
    )(x, g.reshape(1, d), w_in, w_out, cw, prev)


def _group_mean(x2, g_ref):
    hi = x2.astype(BF16)
    r1 = x2 - hi.astype(F32)
    mid = r1.astype(BF16)
    lo = (r1 - mid.astype(F32)).astype(BF16)
    g = g_ref[...]
    return (jnp.dot(hi, g, preferred_element_type=F32)
            + (jnp.dot(mid, g, preferred_element_type=F32) + jnp.dot(lo, g, preferred_element_type=F32)))


def _even_in_kernel(x_ref, g_ref, wn_ref, wr_ref, qg_ref, kg_ref, g512_ref, g128_ref,
                    q_ref, kv_ref, gt_ref, zrw_ref):
    x = x_ref[...]
    ms = jnp.mean(x * x, axis=-1, keepdims=True)
    xn = (x * lax.rsqrt(ms + RMS_EPS) * g_ref[...]).astype(BF16)
    zrw_ref[...] = _nt(xn, wr_ref[...])
    z = _nt(xn, wn_ref[...])
    q = z[:, 0:NSA_WIDTH]
    q_ref[...] = (q * lax.rsqrt(_group_mean(q * q, g512_ref) + RMS_EPS) * qg_ref[...]
                  * (HEAD_DIM ** -0.5)).astype(BF16)
    kv_ref[:, 0:256] = z[:, 512:768]
    for r, off in ((0, 768), (1, 1024)):
        k = z[:, off:off + 128]
        kv_ref[:, off - 512:off - 384] = (k * lax.rsqrt(_group_mean(k * k, g128_ref) + RMS_EPS)
                                          * kg_ref[r:r + 1, :])
        kv_ref[:, off - 384:off - 256] = z[:, off + 128:off + 256]
    gt_ref[...] = z[:, 1280:NSA_PROJ_PAD]


def _even_in(x, g, wn_t, wr_t, qg, kg, g512, g128, tm):
    m, d = x.shape
    c2 = lambda i: (0, 0)
    row = lambda n: pl.BlockSpec((tm, n), lambda i: (i, 0))
    return pl.pallas_call(
        _even_in_kernel, name="even_in",
        grid=(m // tm,),
        in_specs=[row(d), pl.BlockSpec((1, d), c2),
                  pl.BlockSpec((NSA_PROJ_PAD, d), c2, pipeline_mode=pl.Buffered(1)),
                  pl.BlockSpec((RWKV_PROJ, d), c2, pipeline_mode=pl.Buffered(1)),
                  pl.BlockSpec((1, 512), c2), pl.BlockSpec((2, 128), c2),
                  pl.BlockSpec((512, 512), c2), pl.BlockSpec((128, 128), c2)],
        out_specs=[row(512), row(768), row(LANE), row(RWKV_PROJ)],
        out_shape=[jax.ShapeDtypeStruct((m, 512), BF16), jax.ShapeDtypeStruct((m, 768), F32),
                   jax.ShapeDtypeStruct((m, LANE), F32), jax.ShapeDtypeStruct((m, RWKV_PROJ), F32)],
        compiler_params=_cp("arbitrary"),
    )(x, g.reshape(1, d), wn_t, wr_t, qg, kg, g512, g128)


def _compress_rows(rows_refs, m, pe_ref, w1_ref, w2_ref, kn_ref, g128_ref):
    outs = []
    for kv, rows_ref in enumerate(rows_refs):
        acc0 = jnp.zeros((m, NSA_KV * CMP_HIDDEN), F32)
        acc1 = jnp.zeros((m, NSA_KV * CMP_HIDDEN), F32)
        for j in range(CMP_STRIDE):
            xj = rows_ref[pl.ds(j, m, stride=CMP_STRIDE), :]
            acc0 = acc0 + jnp.dot((xj + pe_ref[kv, j:j + 1, :]).astype(BF16), w1_ref[kv, j],
                                  preferred_element_type=F32)
            acc1 = acc1 + jnp.dot((xj + pe_ref[kv, CMP_STRIDE + j:CMP_STRIDE + j + 1, :]).astype(BF16),
                                  w1_ref[kv, CMP_STRIDE + j], preferred_element_type=F32)
        hid = acc0 + pltpu.roll(acc1, m - 1, 0)
        outs.append(jnp.dot(jax.nn.gelu(hid).astype(BF16), w2_ref[kv], preferred_element_type=F32))
    kc, vc = outs
    ms = jnp.dot(kc * kc, g128_ref[...], precision=HI, preferred_element_type=F32)
    kc = kc * lax.rsqrt(ms + RMS_EPS) * kn_ref[...]
    return jnp.concatenate([kc, vc], axis=1)


def _compress_kernel(k_ref, v_ref, pe_ref, w1_ref, w2_ref, kn_ref, g128_ref, o_ref, *, m):
    o_ref[0] = _compress_rows((k_ref.at[0], v_ref.at[0]), m, pe_ref, w1_ref, w2_ref, kn_ref, g128_ref)


def _compress(kv, pe, w1, w2, kn, g128):
    b, t, _ = kv.shape
    m = t // CMP_STRIDE
    c2 = lambda i: (0, 0)
    hw = NSA_KV * HEAD_DIM
    return pl.pallas_call(
        functools.partial(_compress_kernel, m=m), name="compress",
        grid=(b,),
        in_specs=[pl.BlockSpec((1, t, hw), lambda i: (i, 0, 0)),
                  pl.BlockSpec((1, t, hw), lambda i: (i, 0, 1)),
                  pl.BlockSpec((2, CMP_LEN, hw), lambda i: (0, 0, 0)),
                  pl.BlockSpec((2, CMP_LEN, hw, NSA_KV * CMP_HIDDEN), lambda i: (0, 0, 0, 0)),
                  pl.BlockSpec((2, NSA_KV * CMP_HIDDEN, hw), lambda i: (0, 0, 0)),
                  pl.BlockSpec((1, 128), c2), pl.BlockSpec((128, 128), c2)],
        out_specs=pl.BlockSpec((1, m, KV_LANES), lambda i: (i, 0, 0)),
        out_shape=jax.ShapeDtypeStruct((b, m, KV_LANES), F32),
        compiler_params=_cp("arbitrary"),
    )(kv, kv, pe, w1, w2, kn, g128)


def _msoftmax(s, mask):
    s = jnp.where(mask, s, NEG)
    mx = jnp.max(s, axis=-1, keepdims=True)
    e = jnp.where(mask, jnp.exp(s - mx), 0.0)
    den = jnp.sum(e, axis=-1, keepdims=True)
    return e * (1.0 / jnp.where(den > 0, den, 1.0))


def _select_blocks(imp, qpos, ns):
    s_io = lax.broadcasted_iota(jnp.int32, imp.shape, 1)
    cur = qpos // SEL_LEN
    forced = (s_io == 0) | (s_io == cur) | (s_io == cur - 1)
    future = s_io * SEL_LEN > qpos
    imp = jnp.where(forced, FORCE_SCORE, imp)
    imp = jnp.where(future, -FORCE_SCORE, imp)
    imp = jnp.where(s_io >= ns, -3e38, imp)
    rank = jnp.zeros(imp.shape, jnp.int32)
    for sp in range(ns):
        col = imp[:, sp:sp + 1]
        beats = (col > imp) | ((col == imp) & (s_io > sp))
        rank = rank + beats.astype(jnp.int32)
    sel = (rank < min(N_SELECT, ns)) & (s_io < ns)
    return sel.astype(F32)


def _heads_to_rows(q, g):
    return jnp.concatenate(
        [q[:, HEAD_DIM * (NSA_QPG * g + h):HEAD_DIM * (NSA_QPG * g + h + 1)] for h in range(NSA_QPG)],
        axis=0)


def _msoftmax0(s, mask):
    s = jnp.where(mask, s, NEG)
    mx = jnp.max(s, axis=0, keepdims=True)
    e = jnp.where(mask, jnp.exp(s - mx), 0.0)
    den = jnp.sum(e, axis=0, keepdims=True)
    return e * (1.0 / jnp.where(den > 0, den, 1.0))


def _select_blocks_t(imp, qpos, ns):
    s_io = lax.broadcasted_iota(jnp.int32, imp.shape, 0)
    cur = qpos // SEL_LEN
    forced = (s_io == 0) | (s_io == cur) | (s_io == cur - 1)
    future = s_io * SEL_LEN > qpos
    imp = jnp.where(forced, FORCE_SCORE, imp)
    imp = jnp.where(future, -FORCE_SCORE, imp)
    imp = jnp.where(s_io >= ns, -3e38, imp)
    rank = jnp.zeros(imp.shape, jnp.int32)
    for sp in range(ns):
        row = imp[sp:sp + 1, :]
        beats = (row > imp) | ((row == imp) & (s_io > sp))
        rank = rank + beats.astype(jnp.int32)
    sel = (rank < min(N_SELECT, ns)) & (s_io < ns)
    return sel.astype(F32)


def _flash_steps_t(chains):
    sts = [_nt(k, qg) for (qg, k, _, _, _, _) in chains]
    mids = []
    for st, (_, _, _, bias, madd, (m_, l_, _)) in zip(sts, chains):
        ps, ms, ls, als = [], [], [], []
        for h in range(NSA_QPG):
            hs = slice(KT * h, KT * (h + 1))
            s = st[:, hs]
            if bias is not None:
                s = s + bias[:, hs]
            if madd is not None:
                s = s + madd
            m_new = jnp.maximum(m_[:, hs], jnp.max(s, axis=0, keepdims=True))
            alpha = jnp.exp(m_[:, hs] - m_new)
            e = jnp.exp(s - m_new)
            ls.append(alpha * l_[:, hs] + jnp.sum(e, axis=0, keepdims=True))
            ms.append(m_new)
            als.append(alpha)
            ps.append(e.astype(BF16))
        mids.append((jnp.concatenate(ps, axis=1), jnp.concatenate(ms, axis=1),
                     jnp.concatenate(ls, axis=1), jnp.concatenate(als, axis=1)))
    out = []
    for (p, m_new, l_new, alpha), (_, _, v, _, _, (_, _, acc)) in zip(mids, chains):
        out.append((m_new, l_new, alpha * acc + _tn(v, p)))
    return out


def _flash_init_t():
    return (jnp.full((1, NSA_QPG * KT), NEG, F32), jnp.zeros((1, NSA_QPG * KT), F32),
            jnp.zeros((HEAD_DIM, NSA_QPG * KT), F32))


def _flash_out_t(carry):
    _, l_, acc = carry
    return acc * (1.0 / jnp.where(l_ > 0, l_, 1.0))


def _nsa_prompt_kernel(q_ref, gt_ref, cc_ref, kv_ref, bc_ref, tb_ref, ovt_ref, ext_ref, o_ref, selk_s, sc_s,
                       *, m, ns):
    i = pl.program_id(1)
    q = q_ref[0]
    gst = jax.nn.sigmoid(gt_ref[...]).T
    qpos = i * KT + lax.broadcasted_iota(jnp.int32, (1, KT), 1)
    n_io = lax.broadcasted_iota(jnp.int32, (m, KT), 0)
    mask_c = ((n_io * CMP_STRIDE + (CMP_LEN - 1)) <= qpos) & (n_io < m - 1)
    cc = cc_ref[0]
    groups = range(NSA_KV)
    qgs = [_heads_to_rows(q, g) for g in groups]

    kcs = [cc[:, HEAD_DIM * g:HEAD_DIM * (g + 1)].astype(BF16) for g in groups]
    vcs = [cc[:, 128 + HEAD_DIM * g:128 + HEAD_DIM * (g + 1)].astype(BF16) for g in groups]
    c_off = pl.multiple_of(jnp.maximum(i - 1, 0) * (KT // CMP_STRIDE), KT // CMP_STRIDE)
    sts = []
    for g in groups:
        sc_s[g] = _nt(kcs[g], qgs[g])
        band = jnp.where(i == 0, bc_ref[g, 1], bc_ref[g, 0])
        sc_s[g, pl.ds(c_off, 2 * KT // CMP_STRIDE), :] += band
        sts.append(sc_s[g])
    pcs = [[_msoftmax0(sts[g][:, KT * h:KT * (h + 1)], mask_c) for h in range(NSA_QPG)]
           for g in groups]
    o_cs = [_tn(vcs[g], jnp.concatenate(pcs[g], axis=1).astype(BF16)) for g in groups]
    imps = [jnp.dot(ovt_ref[...], pcs[g][0] + pcs[g][1] + pcs[g][2] + pcs[g][3], precision=HI,
                    preferred_element_type=F32) for g in groups]
    sels = [_select_blocks_t(imps[g], qpos, ns).astype(BF16) for g in groups]
    for g in groups:
        selk_s[g] = (jnp.dot(ext_ref[...], sels[g], preferred_element_type=F32) - 1.0) * (-NEG)

    def load_kv(g, kt, off, ntile=1):
        r0 = pl.multiple_of(kt * KT, KT)
        k_lo = 256 + off + HEAD_DIM * g
        v_lo = 384 + off + HEAD_DIM * g
        k = kv_ref[0, pl.ds(r0, ntile * KT), k_lo:k_lo + HEAD_DIM].astype(BF16)
        v = kv_ref[0, pl.ds(r0, ntile * KT), v_lo:v_lo + HEAD_DIM].astype(BF16)
        return k, v

    def sel_mask(g, kt, ntile=1):
        r0 = pl.multiple_of(kt * KT, KT)
        return selk_s[g, pl.ds(r0, ntile * KT), :]

    n_main = jnp.maximum(i - 1, 0)

    def body(pr, carry):
        chains = []
        for g in groups:
            k, v = load_kv(g, 2 * pr, 0, 2)
            chains.append((qgs[g], k, v, None, sel_mask(g, 2 * pr, 2), carry[g]))
        return tuple(_flash_steps_t(chains))

    c_sel = list(lax.fori_loop(0, n_main // 2, body, tuple(_flash_init_t() for _ in groups)))
    kt_odd = (n_main // 2) * 2
    has_odd = n_main % 2 == 1

    nwt = WINDOW // KT
    c_win = [_flash_init_t() for _ in groups]
    for d in range(nwt, -1, -1):
        kt = i - d
        ktc = jnp.maximum(kt, 0)
        valid = None if d == 0 else jnp.where(kt >= 0, 0.0, NEG)
        chains = []
        for g in groups:
            k, v = load_kv(g, ktc, 256)
            bias = tb_ref[g, {nwt: 2, 1: 0, 0: 1}[d]] if d in (nwt, 1, 0) else None
            chains.append((qgs[g], k, v, bias, valid, c_win[g]))
        if d == nwt:
            for g in groups:
                k, v = load_kv(g, kt_odd, 0)
                madd = sel_mask(g, kt_odd) + jnp.where(has_odd, 0.0, NEG)
                chains.append((qgs[g], k, v, None, madd, c_sel[g]))
        elif d <= 1:
            for g in groups:
                k, v = load_kv(g, ktc, 0)
                madd = sel_mask(g, ktc) if d == 0 else sel_mask(g, ktc) + valid
                chains.append((qgs[g], k, v, tb_ref[g, 1 - d], madd, c_sel[g]))
        res = _flash_steps_t(chains)
        c_win = res[:NSA_KV]
        if len(res) > NSA_KV:
            c_sel = res[NSA_KV:]
    o_ss = [_flash_out_t(c_sel[g]) for g in groups]
    o_ws = [_flash_out_t(c_win[g]) for g in groups]

    for g in groups:
        for h in range(NSA_QPG):
            hh = NSA_QPG * g + h
            hs = slice(KT * h, KT * (h + 1))
            o = (gst[3 * hh:3 * hh + 1, :] * o_cs[g][:, hs] + gst[3 * hh + 1:3 * hh + 2, :] * o_ss[g][:, hs]
                 + gst[3 * hh + 2:3 * hh + 3, :] * o_ws[g][:, hs])
            o_ref[0, HEAD_DIM * hh:HEAD_DIM * (hh + 1), :] = o.astype(BF16)


def _nsa_prompt(qn, z_nsa, cc, kv, bct, tbt, ovt, ext, b, t_len):
    m = cc.shape[1]
    ns = -(-t_len // SEL_LEN)
    nq = t_len // KT
    nsp = ovt.shape[0]
    c2 = lambda bi, i: (0, 0)
    return pl.pallas_call(
        functools.partial(_nsa_prompt_kernel, m=m, ns=ns), name="nsa_prompt",
        grid=(b, nq),
        in_specs=[pl.BlockSpec((1, KT, 512), lambda bi, i: (bi, i, 0)),
                  pl.BlockSpec((KT, LANE), lambda bi, i: (bi * nq + i, 0)),
                  pl.BlockSpec((1, m, KV_LANES), lambda bi, i: (bi, 0, 0)),
                  pl.BlockSpec((1, t_len, 768), lambda bi, i: (bi, 0, 0)),
                  pl.BlockSpec((NSA_KV, 2, 2 * KT // CMP_STRIDE, NSA_QPG * KT), lambda bi, i: (0, 0, 0, 0)),
                  pl.BlockSpec((NSA_KV, 3, KT, NSA_QPG * KT), lambda bi, i: (0, 0, 0, 0)),
                  pl.BlockSpec((nsp, m), c2),
                  pl.BlockSpec((t_len, nsp), c2)],
        out_specs=pl.BlockSpec((1, 512, KT), lambda bi, i: (bi, 0, i)),
        out_shape=jax.ShapeDtypeStruct((b, 512, t_len), BF16),
        scratch_shapes=[pltpu.VMEM((NSA_KV, t_len, KT), F32),
                        pltpu.VMEM((NSA_KV, m, NSA_QPG * KT), F32)],
        compiler_params=_cp("arbitrary", "arbitrary"),
    )(qn.reshape(b, t_len, 512), z_nsa, cc, kv.reshape(b, t_len, 768), bct, tbt, ovt, ext)


def _nsa_paged_kernel(pt_ref, cmp_hbm, slc_hbm, new_ref, win_ref, q_ref, gt_ref, pe_ref, w1_ref, w2_ref,
                      kn_ref, g128_ref, bc_ref, bs_ref, bw_ref, ov_ref, ex_ref, o_ref, wo_ref,
                      cmpt_s, slct_s, cmpk_s, cmpv_s, sems, *, li, nsteps, nbr, npages, past, wb, tq, m, ns):
    step = pl.program_id(0)
    slot = step % 2

    def page_copy(st_, bb, p, sl, which):
        hbm, buf = ((cmp_hbm, cmpt_s), (slc_hbm, slct_s))[which]
        return pltpu.make_async_copy(hbm.at[pt_ref[st_ * nbr + bb, p], li],
                                     buf.at[sl, bb, :, pl.ds(p * PAGE, PAGE)], sems.at[sl, which])

    def start_step(st_, sl):
        for bb in range(nbr):
            for p in range(npages):
                page_copy(st_, bb, p, sl, 0).start()
                page_copy(st_, bb, p, sl, 1).start()

    @pl.when(step == 0)
    def _():
        start_step(0, 0)

    @pl.when(step + 1 < nsteps)
    def _():
        start_step(step + 1, 1 - slot)

    for bb in range(nbr):
        for p in range(npages):
            page_copy(step, bb, p, slot, 0).wait()
            page_copy(step, bb, p, slot, 1).wait()

    hw = NSA_KV * HEAD_DIM
    for bb in range(nbr):
        for p in range(npages):
            blk = cmpt_s[slot, bb, :, p * PAGE:(p + 1) * PAGE]
            r0 = bb * past + p * PAGE
            cmpk_s[r0:r0 + PAGE, :] = blk[0:hw].T
            cmpv_s[r0:r0 + PAGE, :] = blk[hw:2 * hw].T
    cc_all = _compress_rows((cmpk_s, cmpv_s), nbr * m, pe_ref, w1_ref, w2_ref, kn_ref, g128_ref)

    pairs = [(bb, g) for bb in range(nbr) for g in range(NSA_KV)]
    groups = range(len(pairs))
    ccs = [cc_all[m * bb:m * (bb + 1)] for bb in range(nbr)]
    newt32 = [jnp.concatenate([new_ref[bb], jnp.zeros((LANE - tq, 768), F32)], axis=0).T
              for bb in range(nbr)]
    newts = [n_.astype(BF16) for n_ in newt32]
    for bb in range(nbr):
        wo_ref[bb] = jnp.concatenate([win_ref[bb, 0, :, wb + tq - WINDOW:wb],
                                      newt32[bb][2 * KV_LANES:3 * KV_LANES, 0:tq]], axis=1)
    gss = [jax.nn.sigmoid(gt_ref[bb]) for bb in range(nbr)]
    kp = past + LANE
    wp = wb + LANE
    qpos = past + lax.broadcasted_iota(jnp.int32, (tq, 1), 0)
    n_io = lax.broadcasted_iota(jnp.int32, (tq, m), 1)
    mask_c = ((n_io * CMP_STRIDE + (CMP_LEN - 1)) <= qpos) & (n_io < m - 1)
    k_io = lax.broadcasted_iota(jnp.int32, (tq, kp), 1)
    mask_s0 = (k_io <= qpos) & (k_io < past + tq)
    j_io = lax.broadcasted_iota(jnp.int32, (tq, wp), 1)
    dist_w = qpos - (past - wb + j_io)
    mask_w = (dist_w >= 0) & (dist_w < WINDOW) & (j_io < wb + tq)
    rep = lambda a_: jnp.concatenate([a_] * NSA_QPG, axis=0)
    mask_c4, mask_w4 = rep(mask_c), rep(mask_w)
    gsl = lambda g, off: slice(off + HEAD_DIM * g, off + HEAD_DIM * (g + 1))
    qgs = [_heads_to_rows(q_ref[bb], g) for bb, g in pairs]
    bof = [bb for bb, _ in pairs]
    gof = [g for _, g in pairs]

    s_c = [_nt(qgs[i], ccs[bof[i]][:, gsl(gof[i], 0)].astype(BF16)) + bc_ref[gof[i]] for i in groups]
    p_c = [_msoftmax(s_c[i], mask_c4) for i in groups]
    o_c = [jnp.dot(p_c[i].astype(BF16), ccs[bof[i]][:, gsl(gof[i], hw)].astype(BF16),
                   preferred_element_type=F32) for i in groups]
    imps = [jnp.dot(p_c[i][0:tq] + p_c[i][tq:2 * tq] + p_c[i][2 * tq:3 * tq] + p_c[i][3 * tq:4 * tq],
                    ov_ref[...], precision=HI, preferred_element_type=F32) for i in groups]
    sels = [_select_blocks(imps[i], qpos, ns).astype(BF16) for i in groups]
    selk = [jnp.dot(sels[i], ex_ref[...], preferred_element_type=F32) for i in groups]

    s_s = [jnp.concatenate(
        [jnp.dot(qgs[i], slct_s[slot, bof[i], gsl(gof[i], 0), :].astype(BF16), preferred_element_type=F32),
         jnp.dot(qgs[i], newts[bof[i]][gsl(gof[i], 256)], preferred_element_type=F32)], axis=1)
        + bs_ref[gof[i]] for i in groups]
    p_s = [_msoftmax(s_s[i], rep((selk[i] > 0.5) & mask_s0)) for i in groups]
    o_s = [_nt(p_s[i][:, 0:past].astype(BF16), slct_s[slot, bof[i], gsl(gof[i], hw), :].astype(BF16))
           + _nt(p_s[i][:, past:kp].astype(BF16), newts[bof[i]][gsl(gof[i], 256 + hw)]) for i in groups]

    s_w = [jnp.concatenate(
        [jnp.dot(qgs[i], win_ref[bof[i], 0, gsl(gof[i], 0), :].astype(BF16), preferred_element_type=F32),
         jnp.dot(qgs[i], newts[bof[i]][gsl(gof[i], 512)], preferred_element_type=F32)], axis=1)
        + bw_ref[gof[i]] for i in groups]
    p_w = [_msoftmax(s_w[i], mask_w4) for i in groups]
    o_w = [_nt(p_w[i][:, 0:wb].astype(BF16), win_ref[bof[i], 0, gsl(gof[i], hw), :].astype(BF16))
           + _nt(p_w[i][:, wb:wp].astype(BF16), newts[bof[i]][gsl(gof[i], 512 + hw)]) for i in groups]

    for bb in range(nbr):
        outs = []
        gs = gss[bb]
        for g in range(NSA_KV):
            i = bb * NSA_KV + g
            for h in range(NSA_QPG):
                hh = NSA_QPG * g + h
                rs = slice(tq * h, tq * (h + 1))
                outs.append(gs[:, 3 * hh:3 * hh + 1] * o_c[i][rs] + gs[:, 3 * hh + 1:3 * hh + 2] * o_s[i][rs]
                            + gs[:, 3 * hh + 2:3 * hh + 3] * o_w[i][rs])
        o_ref[bb] = jnp.concatenate(outs, axis=1).astype(BF16)


def _nsa_paged(pt, pool_cmp_t, pool_slc_t, li, kv_new, win_t, qn, z_nsa, cw, bc, bs, bw, ov, ex, b, tq):
    npages = pt.shape[1]
    past = npages * PAGE
    wb = win_t.shape[3]
    assert past % CMP_STRIDE == 0 and tq < CMP_STRIDE
    assert wb + tq >= WINDOW
    kp = past + LANE
    wp = wb + LANE
    m = (past + tq) // CMP_STRIDE
    ns = -(-(past + tq) // SEL_LEN)
    rows = NSA_QPG * tq
    pe, w1, w2, kn, g128 = cw
    hw = NSA_KV * HEAD_DIM
    c2 = lambda bi, pt_: (0, 0)
    c3 = lambda bi, pt_: (0, 0, 0)
    per_b3 = lambda bi, pt_: (bi, 0, 0)
    nbr = 2 if b % 2 == 0 else 1
    grid_spec = pltpu.PrefetchScalarGridSpec(
        num_scalar_prefetch=1,
        grid=(b // nbr,),
        in_specs=[pl.BlockSpec(memory_space=pl.ANY),
                  pl.BlockSpec(memory_space=pl.ANY),
                  pl.BlockSpec((nbr, tq, 768), per_b3),
                  pl.BlockSpec((nbr, 1, KV_LANES, wb), lambda bi, pt_: (bi, li, 0, 0)),
                  pl.BlockSpec((nbr, tq, 512), per_b3),
                  pl.BlockSpec((nbr, tq, LANE), lambda bi, pt_: (bi, 0, 0)),
                  pl.BlockSpec((2, CMP_LEN, hw), c3),
                  pl.BlockSpec((2, CMP_LEN, hw, NSA_KV * CMP_HIDDEN), lambda bi, pt_: (0, 0, 0, 0)),
                  pl.BlockSpec((2, NSA_KV * CMP_HIDDEN, hw), c3),
                  pl.BlockSpec((1, 128), c2), pl.BlockSpec((128, 128), c2),
                  pl.BlockSpec((NSA_KV, rows, m), c3),
                  pl.BlockSpec((NSA_KV, rows, kp), c3),
                  pl.BlockSpec((NSA_KV, rows, wp), c3),
                  pl.BlockSpec((m, LANE), c2),
                  pl.BlockSpec((LANE, kp), c2)],
        out_specs=[pl.BlockSpec((nbr, tq, 512), per_b3),
                   pl.BlockSpec((nbr, KV_LANES, WINDOW), per_b3)],
        scratch_shapes=[pltpu.VMEM((2, nbr, KV_LANES, past), F32), pltpu.VMEM((2, nbr, KV_LANES, past), F32),
                        pltpu.VMEM((nbr * past, hw), F32), pltpu.VMEM((nbr * past, hw), F32),
                        pltpu.SemaphoreType.DMA((2, 2))])
    kern = functools.partial(_nsa_paged_kernel, li=li, nsteps=b // nbr, nbr=nbr, npages=npages, past=past,
                             wb=wb, tq=tq, m=m, ns=ns)
    return pl.pallas_call(
        kern, grid_spec=grid_spec, name="nsa_paged",
        out_shape=[jax.ShapeDtypeStruct((b, tq, 512), BF16),
                   jax.ShapeDtypeStruct((b, KV_LANES, WINDOW), F32)],
        compiler_params=_cp("arbitrary"),
    )(pt, pool_cmp_t, pool_slc_t, kv_new.reshape(b, tq, 768), win_t, qn.reshape(b, tq, 512),
      z_nsa.reshape(b, tq, LANE), pe, w1, w2, kn, g128, bc, bs, bw, ov, ex)


def _rwkv_kernel(z_ref, zp_ref, s0_ref, mu_ref, w0_ref, w2_ref, a0_ref, a2_ref, g2_ref, kkp_ref, ka_ref,
                 rk_ref, lnw_ref, lnb_ref, y_ref, st_ref, s_scr, carry_scr, *, c_len, nb):
    c = pl.program_id(1)
    cl = c_len
    nr = nb * cl

    @pl.when(c == 0)
    def _():
        s_scr[...] = s0_ref[...]
        carry_scr[...] = zp_ref[...]

    z3 = z_ref[...]
    row = lax.broadcasted_iota(jnp.int32, z3.shape, 1)
    shifted = jnp.where(row == 0, carry_scr[...], pltpu.roll(z3, 1, 1))
    carry_scr[...] = z3[:, cl - 1:cl, :]
    zz = (z3 + (shifted - z3) * mu_ref[...]).reshape(nr, RWKV_PROJ)
    w = RWKV_WIDTH
    r = zz[:, 0:w]
    k = zz[:, w:2 * w]
    v = zz[:, 2 * w:3 * w]
    zw = zz[:, 3 * w:3 * w + DECAY_LORA]
    za = zz[:, 3 * w + DECAY_LORA:3 * w + DECAY_LORA + AAA_LORA]
    zg = zz[:, 3 * w + DECAY_LORA + AAA_LORA:]
    xw = -(w0_ref[...] + _mmp(jnp.tanh(zw), w2_ref[...], _NN, RWKV_LORA_MM))
    softplus = jnp.maximum(xw, 0.0) + jnp.log(1.0 + jnp.exp(-jnp.abs(xw)))
    logdec = -jnp.exp(-softplus - 0.5)
    a = jax.nn.sigmoid(a0_ref[...] + _mmp(za, a2_ref[...], _NN, RWKV_LORA_MM))
    gate = _mmp(jax.nn.sigmoid(zg), g2_ref[...], _NN, RWKV_LORA_MM)
    k_mod = k * (1.0 + (a - 1.0) * ka_ref[...])
    kku = k * kkp_ref[...]

    ti = lax.broadcasted_iota(jnp.int32, (nr, nr), 0)
    si = lax.broadcasted_iota(jnp.int32, (nr, nr), 1)
    tri = ((si <= ti) & (si >= (ti // cl) * cl)).astype(BF16)
    ld_hi = logdec.astype(BF16)
    ld_r = logdec - ld_hi.astype(F32)
    ld_mid = ld_r.astype(BF16)
    ld_lo = (ld_r - ld_mid.astype(F32)).astype(BF16)
    gcum = (jnp.dot(tri, ld_hi, preferred_element_type=F32)
            + (jnp.dot(tri, ld_mid, preferred_element_type=F32)
               + jnp.dot(tri, ld_lo, preferred_element_type=F32)))
    e_g_all = jnp.exp(gcum)
    e_gm_all = jnp.exp(gcum - logdec)
    e_ng_all = jnp.exp(-gcum)
    nsteps = max(1, int(math.ceil(math.log2(cl))))
    row2 = lax.broadcasted_iota(jnp.int32, (cl, 2 * cl), 0)
    col2 = lax.broadcasted_iota(jnp.int32, (cl, 2 * cl), 1)
    col2 = jnp.where(col2 >= cl, col2 - cl, col2)
    strict2 = col2 < row2
    lower2 = col2 <= row2
    keep_z = lax.broadcasted_iota(jnp.int32, (cl, cl + HEAD_DIM), 1) >= cl

    pairs = [(bb, hd) for bb in range(nb) for hd in range(RWKV_HEADS)]
    heads = range(len(pairs))
    rws = [slice(cl * bb, cl * (bb + 1)) for bb, _ in pairs]
    sls = [slice(HEAD_DIM * hd, HEAD_DIM * (hd + 1)) for _, hd in pairs]
    kks = [kku[rws[h], sls[h]] for h in heads]
    kks = [kk * lax.rsqrt(jnp.maximum(jnp.sum(kk * kk, axis=-1, keepdims=True), 1e-24)) for kk in kks]
    kkas = [kks[h] * a[rws[h], sls[h]] for h in heads]
    kms = [k_mod[rws[h], sls[h]] for h in heads]
    vhs = [v[rws[h], sls[h]] for h in heads]
    rhs_ = [r[rws[h], sls[h]] for h in heads]
    s_hs = [s_scr[bb, hd] for bb, hd in pairs]
    e_g = [e_g_all[rws[h], sls[h]] for h in heads]
    e_gm = [e_gm_all[rws[h], sls[h]] for h in heads]
    e_ng = [e_ng_all[rws[h], sls[h]] for h in heads]
    glast = [gcum[cl * (bb + 1) - 1:cl * (bb + 1), sls[h]] for h, (bb, _) in enumerate(pairs)]
    e_lg = [jnp.exp(glast[h] - gcum[rws[h], sls[h]]) for h in heads]
    e_l = [jnp.exp(glast[h]) for h in heads]
    lrs = [jnp.concatenate([-kks[h] * e_gm[h], rhs_[h] * e_g[h]], axis=0) for h in heads]
    rrs = [jnp.concatenate([kkas[h] * e_ng[h], kms[h] * e_ng[h]], axis=0) for h in heads]
    m1s = [_mmp(lrs[h], rrs[h], _NT) for h in heads]
    m2s = [_mmp(lrs[h], s_hs[h], _NT) for h in heads]
    tops = [jnp.where(strict2, m1[0:cl], 0.0) for m1 in m1s]
    bots = [jnp.where(lower2, m1[cl:2 * cl], 0.0) for m1 in m1s]
    zeros_v = jnp.zeros((cl, HEAD_DIM), F32)
    rhss = [m2s[h][0:cl] + _mmp(tops[h], jnp.concatenate([zeros_v, vhs[h]], axis=0), _NN) for h in heads]
    wms = [jnp.concatenate([tops[h][:, 0:cl], rhss[h]], axis=1) for h in heads]
    for _ in range(nsteps):
        wms = [_mmp(w_[:, 0:cl], w_, _NN) + jnp.where(keep_z, w_, 0.0) for w_ in wms]
    zvs = [jnp.concatenate([wms[h][:, cl:cl + HEAD_DIM], vhs[h]], axis=0) for h in heads]
    ys = [m2s[h][cl:2 * cl] + _mmp(bots[h], zvs[h], _NN) for h in heads]
    bkhs = [jnp.concatenate([kkas[h] * e_lg[h], kms[h] * e_lg[h]], axis=0) for h in heads]
    s_new = [s_hs[h] * e_l[h] + _mmp(zvs[h], bkhs[h], _TN) for h in heads]
    for h, (bb, hd) in enumerate(pairs):
        s_scr[bb, hd] = s_new[h]
    outs = []
    for h in heads:
        y = ys[h]
        sl = sls[h]
        mean = jnp.mean(y, axis=-1, keepdims=True)
        yc = y - mean
        var = jnp.mean(yc * yc, axis=-1, keepdims=True)
        yn = yc * lax.rsqrt(var + GN_EPS) * lnw_ref[:, sl] + lnb_ref[:, sl]
        bonus = jnp.sum(rhs_[h] * kms[h] * rk_ref[:, sl], axis=-1, keepdims=True) * vhs[h]
        outs.append(yn + bonus)
    for bb in range(nb):
        yb = jnp.concatenate(outs[RWKV_HEADS * bb:RWKV_HEADS * (bb + 1)], axis=1)
        y_ref[bb] = (yb * gate[cl * bb:cl * (bb + 1)]).astype(BF16)
    st_ref[...] = s_scr[...]


def _rwkv(z_rw, z_prev, s0, pr, b, t_len, c_len, nb):
    nchunk = t_len // c_len
    c2 = lambda bi, c: (0, 0)
    vec = lambda n: pl.BlockSpec((1, n), c2)
    return pl.pallas_call(
        functools.partial(_rwkv_kernel, c_len=c_len, nb=nb), name="rwkv",
        grid=(b // nb, nchunk),
        in_specs=[pl.BlockSpec((nb, c_len, RWKV_PROJ), lambda bi, c: (bi, c, 0)),
                  pl.BlockSpec((nb, 1, RWKV_PROJ), lambda bi, c: (bi, 0, 0)),
                  pl.BlockSpec((nb, RWKV_HEADS, HEAD_DIM, HEAD_DIM), lambda bi, c: (bi, 0, 0, 0)),
                  vec(RWKV_PROJ), vec(512), pl.BlockSpec((DECAY_LORA, 512), c2),
                  vec(512), pl.BlockSpec((AAA_LORA, 512), c2), pl.BlockSpec((GATE_LORA, 512), c2),
                  vec(512), vec(512), vec(512), vec(512), vec(512)],
        out_specs=[pl.BlockSpec((nb, c_len, 512), lambda bi, c: (bi, c, 0)),
                   pl.BlockSpec((nb, RWKV_HEADS, HEAD_DIM, HEAD_DIM), lambda bi, c: (bi, 0, 0, 0))],
        out_shape=[jax.ShapeDtypeStruct((b, t_len, 512), BF16),
                   jax.ShapeDtypeStruct((b, RWKV_HEADS, HEAD_DIM, HEAD_DIM), F32)],
        scratch_shapes=[pltpu.VMEM((nb, RWKV_HEADS, HEAD_DIM, HEAD_DIM), F32),
                        pltpu.VMEM((nb, 1, RWKV_PROJ), F32)],
        compiler_params=_cp("arbitrary", "arbitrary"),
    )(z_rw.reshape(b, t_len, RWKV_PROJ), z_prev.reshape(b, 1, RWKV_PROJ), s0, *pr)


def _rel_bucket(dist):
    n = jnp.maximum(dist, 0)
    max_exact = NUM_BUCKETS // 2
    nf = jnp.maximum(n, 1).astype(F32)
    large = max_exact + (jnp.log(nf / max_exact) / math.log(MAX_DISTANCE / max_exact)
                         * (NUM_BUCKETS - max_exact)).astype(jnp.int32)
    large = jnp.minimum(large, NUM_BUCKETS - 1)
    return jnp.where(n < max_exact, n, large)


def _bias_table(rel_bias, dist, delta):
    tb = rel_bias.astype(F32)
    out = jnp.moveaxis(tb[_rel_bucket(dist)], -1, 0)
    if delta:
        out = out - tb[NUM_BUCKETS - 1].reshape((NSA_HEADS,) + (1,) * dist.ndim)
    return out


def _toeplitz(u, nrows, ncols, base, step):
    lo = base - step * (nrows - 1)
    ln = ncols + step * (nrows - 1)
    lead = u.shape[:-1]
    w = jnp.broadcast_to(u[..., None, lo:lo + ln], lead + (nrows, ln)).reshape(lead + (nrows * ln,))
    w = jnp.pad(w, [(0, 0)] * len(lead) + [(0, nrows * step)]).reshape(lead + (nrows, ln + step))
    return w[..., ::-1, :ncols]


def _block_diag_ones(n, grp):
    idx = jnp.arange(n) // grp
    return (idx[:, None] == idx[None, :]).astype(F32) / grp


def _compress_weights(phi_pe, phi_w1, phi_w2, kn_cmp):
    pe = jnp.concatenate([phi_pe, phi_pe], axis=-1)
    w1 = jnp.zeros((2, CMP_LEN, NSA_KV * HEAD_DIM, NSA_KV * CMP_HIDDEN), F32)
    w2 = jnp.zeros((2, NSA_KV * CMP_HIDDEN, NSA_KV * HEAD_DIM), F32)
    for g in range(NSA_KV):
        w1 = w1.at[:, :, HEAD_DIM * g:HEAD_DIM * (g + 1), CMP_HIDDEN * g:CMP_HIDDEN * (g + 1)].set(phi_w1)
        w2 = w2.at[:, CMP_HIDDEN * g:CMP_HIDDEN * (g + 1), HEAD_DIM * g:HEAD_DIM * (g + 1)].set(phi_w2)
    kn = jnp.tile(kn_cmp, NSA_KV).reshape(1, 128)
    return pe, w1.astype(BF16), w2.astype(BF16), kn


def _overlap_expand(t_all):
    nc = t_all // CMP_STRIDE - 1
    m = nc + 1
    ns = -(-t_all // SEL_LEN)
    c0 = jnp.arange(m)[:, None] * CMP_STRIDE
    s0 = jnp.arange(LANE)[None, :] * SEL_LEN
    ov = ((c0 < s0 + SEL_LEN) & (c0 + CMP_LEN > s0) & (jnp.arange(m)[:, None] < nc)
          & (jnp.arange(LANE)[None, :] < ns)).astype(F32)
    kp = _round_up(t_all, LANE)
    ex = (jnp.arange(kp)[None, :] // SEL_LEN == jnp.arange(LANE)[:, None]).astype(BF16)
    return ov, ex


def _pick_tm(t_len):
    return 512 if t_len % 512 == 0 else 128


def _trunk(x3, prm, st):
    b, t_len, d = x3.shape
    m_rows = b * t_len
    x = x3.reshape(m_rows, d)
    prompt = st is None
    if prompt:
        tm = _pick_tm(t_len)
        c_len = 64
    else:
        tm = 256 if m_rows % 256 == 0 else m_rows
        c_len = t_len
    tm_f = tm
    g512 = _block_diag_ones(512, HEAD_DIM)
    g128 = _block_diag_ones(128, HEAD_DIM)
    rel_bias = prm["rel_bias"]

    if prompt:
        ov, ex = _overlap_expand(t_len)
        nsp = _round_up(-(-t_len // SEL_LEN), 8)
        ovt = ov[:, :nsp].T
        ext = ex[:nsp, :t_len].T
        off = 3 * KT
        u = _bias_table(rel_bias, jnp.arange(-off, 2 * KT), True)

        def band(rows, base, step):
            t_ = _toeplitz(u, rows, KT, off + base, step).reshape(NSA_KV, NSA_QPG, rows, KT)
            return jnp.transpose(t_, (0, 2, 1, 3)).reshape(NSA_KV, rows, NSA_QPG * KT)

        after = jnp.tile(jnp.arange(KT)[:, None] > jnp.arange(KT)[None, :], (1, NSA_QPG))
        neg_after = jnp.broadcast_to(jnp.where(after, NEG, 0.0), (NSA_KV, KT, NSA_QPG * KT))
        neg_upto = jnp.broadcast_to(jnp.where(after, 0.0, NEG), (NSA_KV, KT, NSA_QPG * KT))
        tbt = jnp.stack([band(KT, KT, 1), band(KT, 0, 1) + neg_after, neg_upto], axis=1)
        ncb = 2 * KT // CMP_STRIDE
        bct = jnp.stack([band(ncb, KT - (CMP_LEN - 1), CMP_STRIDE),
                         band(ncb, -(CMP_LEN - 1), CMP_STRIDE)], axis=1)
    else:
        pt = st["page_table"]
        past = pt.shape[1] * PAGE
        wb = st["win"].shape[2]
        t_all = past + t_len
        ov, ex = _overlap_expand(t_all)
        mcmp = t_all // CMP_STRIDE
        kp = past + LANE
        wp = wb + LANE
        qp = past + jnp.arange(t_len)[:, None]
        cend = jnp.arange(mcmp)[None, :] * CMP_STRIDE + (CMP_LEN - 1)
        rows_ = NSA_QPG * t_len
        bc = _bias_table(rel_bias, qp - cend, False).reshape(NSA_KV, rows_, mcmp)
        near = _bias_table(rel_bias, jnp.arange(-LANE, 2 * LANE), True)[:, ::-1]

        def by_distance(d_top, ncols):
            u_r = jnp.concatenate([jnp.zeros((NSA_HEADS, d_top - (2 * LANE - 1)), F32), near], axis=1)
            return _toeplitz(u_r, t_len, ncols, t_len - 1, 1).reshape(NSA_KV, rows_, ncols)

        bs = by_distance(past + t_len - 1, kp)
        bw = by_distance(wb + t_len - 1, wp)
        pool_cmp = jnp.swapaxes(st["cmp"].reshape(st["cmp"].shape[0], st["cmp"].shape[1], PAGE, KV_LANES), 2, 3)
        pool_slc = jnp.swapaxes(st["slc"].reshape(st["slc"].shape[0], st["slc"].shape[1], PAGE, KV_LANES), 2, 3)
        win_all = st["win"].reshape(b, st["win"].shape[1], wb, KV_LANES)
        win_all_t = jnp.swapaxes(win_all, 2, 3)

    cmp_r, slc_r, win_r, wkv_r, sh_r, conv_r, ffn_r = [], [], [], [], [], [], []
    for l in range(4):
        li = l // 2
        if l % 2 == 0:
            w_in = prm["w_in_even"][li].T
            w_nsa = jnp.pad(w_in[:NSA_PROJ], ((0, NSA_PROJ_PAD - NSA_PROJ), (0, 0))).astype(BF16)
            w_rw = w_in[NSA_PROJ:].astype(BF16)
            qg = jnp.tile(prm["q_norm"][li], NSA_HEADS).reshape(1, 512)
            kg = jnp.stack([jnp.tile(prm["k_norm"][li, 1], NSA_KV), jnp.tile(prm["k_norm"][li, 2], NSA_KV)])
            qn, kv3, z_nsa, z_rw = _even_in(x, prm["norm_mix"][l], w_nsa, w_rw, qg, kg,
                                            g512.astype(BF16), g128.astype(BF16), tm)
            cw = _compress_weights(prm["phi_pe"][li], prm["phi_w1"][li], prm["phi_w2"][li],
                                   prm["k_norm"][li, 0])
            kv3b = kv3.reshape(b, t_len, 768)
            if prompt:
                cc = _compress(kv3b, *cw, g128)
                nsa = _nsa_prompt(qn, z_nsa, cc, kv3, bct, tbt, ovt, ext, b, t_len)
                z_prev = jnp.zeros((b, RWKV_PROJ), F32)
                s0 = jnp.zeros((b, RWKV_HEADS, HEAD_DIM, HEAD_DIM), F32)
                n_keep = min(WINDOW, t_len)
                win_rows = kv3b[:, t_len - n_keep:, 512:768]
            else:
                nsa, win_new_t = _nsa_paged(pt, pool_cmp, pool_slc, li, kv3, win_all_t, qn, z_nsa,
                                            cw + (g128,), bc, bs, bw, ov, ex, b, t_len)
                z_prev = st["shift"][:, li]
                s0 = st["wkv"][:, li]
                n_keep = WINDOW
                win_rows = jnp.swapaxes(win_new_t, 1, 2)
            pr = (prm["rwkv_mu"][li].reshape(1, -1), prm["rwkv_w0"][li].reshape(1, -1), prm["rwkv_w2"][li],
                  prm["rwkv_a0"][li].reshape(1, -1), prm["rwkv_a2"][li], prm["rwkv_g2"][li],
                  prm["rwkv_kk"][li].reshape(1, -1), prm["rwkv_ka"][li].reshape(1, -1),
                  prm["rwkv_rk"][li].reshape(1, -1), prm["rwkv_ln_w"][li].reshape(1, -1),
                  prm["rwkv_ln_b"][li].reshape(1, -1))
            nb_rw = max(n_ for n_ in ((2, 1) if prompt else (4, 2, 1)) if b % n_ == 0)
            rw, s_t = _rwkv(z_rw, z_prev, s0, pr, b, t_len, c_len, nb_rw)
            w_out = prm["w_out_even"][li].astype(BF16)
            if prompt:
                x = _matmul_res_t(nsa, rw.reshape(m_rows, 512), w_out, x, tm, 512)
            else:
                mix = jnp.concatenate([nsa.reshape(m_rows, 512), rw.reshape(m_rows, 512)], axis=1)
                x = _matmul_res(mix, w_out, x, tm, 512)
            cmp_r.append(kv3b[:, :, 0:256].reshape(b, t_len, 2, NSA_KV, HEAD_DIM))
            slc_r.append(kv3b[:, :, 256:512].reshape(b, t_len, 2, NSA_KV, HEAD_DIM))
            win_r.append(win_rows.reshape(b, n_keep, 2, NSA_KV, HEAD_DIM))
            wkv_r.append(s_t)
            sh_r.append(z_rw.reshape(b, t_len, RWKV_PROJ)[:, -1])
        else:
            prev = jnp.zeros((b, 2, D_MODEL), F32) if prompt else st["conv"][:, li]
            x, cs = _fused_block(x, prm["norm_mix"][l], prm["w_in_odd"][li].astype(BF16),
                                 prm["w_out_odd"][li].astype(BF16), prm["conv_w"][li], prev, "odd", t_len, tm_f)
            conv_r.append(cs)
        prev = jnp.zeros((b, 2, D_FF), F32) if prompt else st["ffn"][:, l]
        x, fs = _fused_block(x, prm["norm_ffn"][l], prm["ffn_up"][l].astype(BF16),
                             prm["ffn_down"][l].astype(BF16), prm["ffn_conv"][l], prev, "ffn", t_len, tm_f)
        ffn_r.append(fs)
    stk = lambda a_: jnp.stack(a_, axis=1)
    return x.reshape(b, t_len, d), (stk(cmp_r), stk(slc_r), stk(win_r), stk(wkv_r), stk(sh_r),
                                    stk(conv_r), stk(ffn_r))


def kernel(x_prompt, x_sample, cache_cmp_kv, cache_slc_kv, cache_win_kv, state_rwkv_wkv, state_rwkv_shift, state_conv, state_ffn_conv, page_table, norm_mix, norm_ffn, rel_bias, w_in_even, w_out_even, q_norm, k_norm, phi_pe, phi_w1, phi_w2, rwkv_mu, rwkv_w0, rwkv_w2, rwkv_a0, rwkv_a2, rwkv_g2, rwkv_kk, rwkv_ka, rwkv_rk, rwkv_ln_w, rwkv_ln_b, w_in_odd, conv_w, w_out_odd, ffn_up, ffn_conv, ffn_down):
    prm = dict(norm_mix=norm_mix, norm_ffn=norm_ffn, rel_bias=rel_bias, w_in_even=w_in_even,
               w_out_even=w_out_even, q_norm=q_norm, k_norm=k_norm, phi_pe=phi_pe, phi_w1=phi_w1,
               phi_w2=phi_w2, rwkv_mu=rwkv_mu, rwkv_w0=rwkv_w0, rwkv_w2=rwkv_w2, rwkv_a0=rwkv_a0,
               rwkv_a2=rwkv_a2, rwkv_g2=rwkv_g2, rwkv_kk=rwkv_kk, rwkv_ka=rwkv_ka, rwkv_rk=rwkv_rk,
               rwkv_ln_w=rwkv_ln_w, rwkv_ln_b=rwkv_ln_b, w_in_odd=w_in_odd, conv_w=conv_w,
               w_out_odd=w_out_odd, ffn_up=ffn_up, ffn_conv=ffn_conv, ffn_down=ffn_down)
    st = dict(cmp=cache_cmp_kv, slc=cache_slc_kv, win=cache_win_kv, wkv=state_rwkv_wkv,
              shift=state_rwkv_shift, conv=state_conv, ffn=state_ffn_conv, page_table=page_table)
    y_p, (cmp_p, slc_p, win_p, wkv_p, sh_p, conv_p, ffn_p) = _trunk(x_prompt, prm, None)
    y_s, (cmp_s, slc_s, win_s, wkv_s, sh_s, conv_s, ffn_s) = _trunk(x_sample, prm, st)
    return (y_p, y_s, cmp_p, cmp_s, slc_p, slc_s, win_p, win_s, wkv_p, wkv_s,
            sh_p, sh_s, conv_p, conv_s, ffn_p, ffn_s)
```

```python
import functools
import math

import jax
import jax.numpy as jnp
from jax import lax
from jax.experimental import pallas as pl
from jax.experimental.pallas import tpu as pltpu

F32 = jnp.float32
BF16 = jnp.bfloat16
HI = lax.Precision.HIGHEST

D_MODEL = 1024
HEAD_DIM = 64
NSA_HEADS = 8
NSA_KV = 2
NSA_QPG = 4
NSA_WIDTH = 512
KV_LANES = 2 * NSA_KV * HEAD_DIM
CMP_LEN = 32
CMP_STRIDE = 16
CMP_HIDDEN = 128
SEL_LEN = 64
N_SELECT = 16
WINDOW = 512
FORCE_SCORE = 1e9
RWKV_HEADS = 8
RWKV_WIDTH = 512
DECAY_LORA = 64
AAA_LORA = 64
GATE_LORA = 128
RWKV_PROJ = 3 * RWKV_WIDTH + DECAY_LORA + AAA_LORA + GATE_LORA
NSA_PROJ = NSA_WIDTH + 6 * NSA_KV * HEAD_DIM + 3 * NSA_HEADS
NSA_PROJ_PAD = 1408
D_FF = 2816
NUM_BUCKETS = 32
MAX_DISTANCE = 128
RMS_EPS = 1e-6
GN_EPS = 64e-5
PAGE = 128
LANE = 128
KT = 128
NEG = -1e30
VMEM_LIMIT = 56 * 1024 * 1024


def _cp(*sem):
    return pltpu.CompilerParams(dimension_semantics=sem, vmem_limit_bytes=VMEM_LIMIT)


def _round_up(a, b):
    return (a + b - 1) // b * b


def _nt(a, b, precision=None):
    return lax.dot_general(a, b, (((1,), (1,)), ((), ())), precision=precision,
                           preferred_element_type=F32)


def _tn(a, b, precision=None):
    return lax.dot_general(a, b, (((0,), (0,)), ((), ())), precision=precision,
                           preferred_element_type=F32)


_NN = ((1,), (0,))
_NT = ((1,), (1,))
_TN = ((0,), (0,))
RWKV_MM = "bf16"
RWKV_LORA_MM = "bf16"


def _mmp(a, b, dims, mode=None):
    mode = RWKV_MM if mode is None else mode
    dn = (dims, ((), ()))
    if mode == "f32":
        return lax.dot_general(a, b, dn, precision=HI, preferred_element_type=F32)
    if mode == "bf16":
        return lax.dot_general(a.astype(BF16), b.astype(BF16), dn, preferred_element_type=F32)
    ah = a.astype(BF16)
    al = (a - ah.astype(F32)).astype(BF16)
    bh = b.astype(BF16)
    bl = (b - bh.astype(F32)).astype(BF16)
    dot = lambda x, y: lax.dot_general(x, y, dn, preferred_element_type=F32)
    return dot(ah, bh) + (dot(ah, bl) + dot(al, bh))


def _mm_res_kernel(a_ref, w_ref, r_ref, o_ref):
    o_ref[...] = r_ref[...] + jnp.dot(a_ref[...], w_ref[...], preferred_element_type=F32)


def _matmul_res(a, w, res, tm, tn):
    m, k = a.shape
    n = w.shape[1]
    return pl.pallas_call(
        _mm_res_kernel, name="mm_res",
        grid=(m // tm, n // tn),
        in_specs=[pl.BlockSpec((tm, k), lambda i, j: (i, 0)),
                  pl.BlockSpec((k, tn), lambda i, j: (0, j)),
                  pl.BlockSpec((tm, tn), lambda i, j: (i, j))],
        out_specs=pl.BlockSpec((tm, tn), lambda i, j: (i, j)),
        out_shape=jax.ShapeDtypeStruct((m, n), F32),
        compiler_params=_cp("arbitrary", "arbitrary"),
    )(a, w, res)


FUSED_CHUNK = 256


def _fused_kernel(x_ref, *rest, mode, nbr, seq_tiles, nseq, lt, mixed):
    i = pl.program_id(0)
    x = x_ref[...]
    if mixed:
        at_ref, b_ref, wmix_ref = rest[:3]
        rest = rest[3:]
        ka = at_ref.shape[1]
        x = (x + _tn(at_ref[0], wmix_ref[0:ka, :])
             + jnp.dot(b_ref[...], wmix_ref[ka:, :], preferred_element_type=F32))
    g_ref, win_ref, wout_ref, cw_ref, prev_ref, o_ref, st_ref, carry_ref = rest
    ms = jnp.mean(x * x, axis=-1, keepdims=True)
    xn = (x * lax.rsqrt(ms + RMS_EPS) * g_ref[...]).astype(BF16)
    f = cw_ref.shape[-1]
    ch = FUSED_CHUNK
    nchunk = f // ch
    if seq_tiles == 1:
        p = prev_ref[...]
    else:
        p = jnp.where(i % seq_tiles == 0, prev_ref[...], carry_ref[...])
    t = lax.broadcasted_iota(jnp.int32, (nseq, lt, ch), 1)

    def project(c):
        return [jnp.dot(xn, win_ref[:, r * f + c * ch:r * f + (c + 1) * ch], preferred_element_type=F32)
                for r in range(nbr)]

    def gate(c, br):
        cs = slice(c * ch, (c + 1) * ch)
        if mode == "ffn":
            cin, other = br
        else:
            other, cg, xi = br
            cin = cg * xi
        c3 = cin.reshape(nseq, lt, ch)
        p0 = p[:, 0:1, cs]
        p1 = p[:, 1:2, cs]
        c1 = jnp.where(t == 0, p1, pltpu.roll(c3, 1, 1))
        c2 = jnp.where(t == 0, p0, jnp.where(t == 1, p1, pltpu.roll(c3, 2, 1)))
        y = c2 * cw_ref[0:1, cs] + c1 * cw_ref[1:2, cs] + c3 * cw_ref[2:3, cs]
        last2 = c3[:, lt - 2:lt, :]
        st_ref[:, :, cs] = last2
        if seq_tiles > 1:
            carry_ref[:, :, cs] = last2
        o3 = other.reshape(c3.shape)
        out = (y * jax.nn.sigmoid(y)) * o3 if mode == "ffn" else o3 * y
        return out.reshape(nseq * lt, ch).astype(BF16)

    acc = jnp.zeros(x.shape, F32)
    br = project(0)
    for c in range(nchunk):
        br_next = project(c + 1) if c + 1 < nchunk else None
        gch = gate(c, br)
        acc = acc + jnp.dot(gch, wout_ref[c * ch:(c + 1) * ch, :], preferred_element_type=F32)
        br = br_next
    o_ref[...] = x + acc


def _fused_block(x, g, w_in, w_out, cw, prev, mode, seq_len, tm, mix=None):
    m, d = x.shape
    nbr = 2 if mode == "ffn" else 3
    f = w_in.shape[1] // nbr
    if seq_len >= tm:
        seq_tiles, nseq, lt = seq_len // tm, 1, tm
    else:
        seq_tiles, nseq, lt = 1, tm // seq_len, seq_len
    b = prev.shape[0]
    mix_specs, mix_args = [], ()
    if mix is not None:
        a_t, bmat, w_mix = mix
        ka, kb = a_t.shape[1], bmat.shape[1]
        tpb = seq_len // tm
        mix_specs = [pl.BlockSpec((1, ka, tm), lambda i: (i // tpb, 0, i % tpb)),
                     pl.BlockSpec((tm, kb), lambda i: (i, 0)),
                     pl.BlockSpec((ka + kb, d), lambda i: (0, 0), pipeline_mode=pl.Buffered(1))]
        mix_args = (a_t, bmat, w_mix)
    kern = functools.partial(_fused_kernel, mode=mode, nbr=nbr, seq_tiles=seq_tiles, nseq=nseq, lt=lt,
                             mixed=mix is not None)
    return pl.pallas_call(
        kern, name="fused_" + mode,
        grid=(m // tm,),
        in_specs=[pl.BlockSpec((tm, d), lambda i: (i, 0))] + mix_specs + [
                  pl.BlockSpec((1, d), lambda i: (0, 0)),
                  pl.BlockSpec((d, nbr * f), lambda i: (0, 0), pipeline_mode=pl.Buffered(1)),
                  pl.BlockSpec((f, d), lambda i: (0, 0), pipeline_mode=pl.Buffered(1)),
                  pl.BlockSpec((3, f), lambda i: (0, 0)),
                  pl.BlockSpec((nseq, 2, f), lambda i: (i // seq_tiles, 0, 0))],
        out_specs=[pl.BlockSpec((tm, d), lambda i: (i, 0)),
                   pl.BlockSpec((nseq, 2, f), lambda i: (i // seq_tiles, 0, 0))],
        out_shape=[jax.ShapeDtypeStruct((m, d), F32),
                   jax.ShapeDtypeStruct((b, 2, f), F32)],
        scratch_shapes=[pltpu.VMEM((nseq, 2, f), F32)],
        compiler_params=_cp("arbitrary"),
    )(x, *mix_args, g.reshape(1, d), w_in, w_out, cw, prev)


def _group_mean(x2, g_ref):
    hi = x2.astype(BF16)
    r1 = x2 - hi.astype(F32)
    mid = r1.astype(BF16)
    lo = (r1 - mid.astype(F32)).astype(BF16)
    g = g_ref[...]
    return (jnp.dot(hi, g, preferred_element_type=F32)
            + (jnp.dot(mid, g, preferred_element_type=F32) + jnp.dot(lo, g, preferred_element_type=F32)))


def _even_in_kernel(x_ref, g_ref, wn_ref, wr_ref, qg_ref, kg_ref, g512_ref, g128_ref,
                    q_ref, kv_ref, gt_ref, zrw_ref):
    x = x_ref[...]
    ms = jnp.mean(x * x, axis=-1, keepdims=True)
    xn = (x * lax.rsqrt(ms + RMS_EPS) * g_ref[...]).astype(BF16)
    zrw_ref[...] = _nt(xn, wr_ref[...])
    z = _nt(xn, wn_ref[...])
    q = z[:, 0:NSA_WIDTH]
    q_ref[...] = (q * lax.rsqrt(_group_mean(q * q, g512_ref) + RMS_EPS) * qg_ref[...]
                  * (HEAD_DIM ** -0.5)).astype(BF16)
    kv_ref[:, 0:256] = z[:, 512:768]
    for r, off in ((0, 768), (1, 1024)):
        k = z[:, off:off + 128]
        kv_ref[:, off - 512:off - 384] = (k * lax.rsqrt(_group_mean(k * k, g128_ref) + RMS_EPS)
                                          * kg_ref[r:r + 1, :])
        kv_ref[:, off - 384:off - 256] = z[:, off + 128:off + 256]
    gt_ref[...] = z[:, 1280:NSA_PROJ_PAD]


def _even_in(x, g, wn_t, wr_t, qg, kg, g512, g128, tm):
    m, d = x.shape
    c2 = lambda i: (0, 0)
    row = lambda n: pl.BlockSpec((tm, n), lambda i: (i, 0))
    return pl.pallas_call(
        _even_in_kernel, name="even_in",
        grid=(m // tm,),
        in_specs=[row(d), pl.BlockSpec((1, d), c2),
                  pl.BlockSpec((NSA_PROJ_PAD, d), c2, pipeline_mode=pl.Buffered(1)),
                  pl.BlockSpec((RWKV_PROJ, d), c2, pipeline_mode=pl.Buffered(1)),
                  pl.BlockSpec((1, 512), c2), pl.BlockSpec((2, 128), c2),
                  pl.BlockSpec((512, 512), c2), pl.BlockSpec((128, 128), c2)],
        out_specs=[row(512), row(768), row(LANE), row(RWKV_PROJ)],
        out_shape=[jax.ShapeDtypeStruct((m, 512), BF16), jax.ShapeDtypeStruct((m, 768), F32),
                   jax.ShapeDtypeStruct((m, LANE), F32), jax.ShapeDtypeStruct((m, RWKV_PROJ), F32)],
        compiler_params=_cp("arbitrary"),
    )(x, g.reshape(1, d), wn_t, wr_t, qg, kg, g512, g128)


def _compress_rows(rows_refs, m, pe_ref, w1_ref, w2_ref, kn_ref, g128_ref):
    outs = []
    for kv, rows_ref in enumerate(rows_refs):
        acc0 = jnp.zeros((m, NSA_KV * CMP_HIDDEN), F32)
        acc1 = jnp.zeros((m, NSA_KV * CMP_HIDDEN), F32)
        for j in range(CMP_STRIDE):
            xj = rows_ref[pl.ds(j, m, stride=CMP_STRIDE), :]
            acc0 = acc0 + jnp.dot((xj + pe_ref[kv, j:j + 1, :]).astype(BF16), w1_ref[kv, j],
                                  preferred_element_type=F32)
            acc1 = acc1 + jnp.dot((xj + pe_ref[kv, CMP_STRIDE + j:CMP_STRIDE + j + 1, :]).astype(BF16),
                                  w1_ref[kv, CMP_STRIDE + j], preferred_element_type=F32)
        hid = acc0 + pltpu.roll(acc1, m - 1, 0)
        outs.append(jnp.dot(jax.nn.gelu(hid).astype(BF16), w2_ref[kv], preferred_element_type=F32))
    kc, vc = outs
    ms = jnp.dot(kc * kc, g128_ref[...], precision=HI, preferred_element_type=F32)
    kc = kc * lax.rsqrt(ms + RMS_EPS) * kn_ref[...]
    return jnp.concatenate([kc, vc], axis=1)


def _compress_kernel(k_ref, v_ref, pe_ref, w1_ref, w2_ref, kn_ref, g128_ref, o_ref, *, m):
    o_ref[0] = _compress_rows((k_ref.at[0], v_ref.at[0]), m, pe_ref, w1_ref, w2_ref, kn_ref, g128_ref)


def _compress(kv, pe, w1, w2, kn, g128):
    b, t, _ = kv.shape
    m = t // CMP_STRIDE
    c2 = lambda i: (0, 0)
    hw = NSA_KV * HEAD_DIM
    return pl.pallas_call(
        functools.partial(_compress_kernel, m=m), name="compress",
        grid=(b,),
        in_specs=[pl.BlockSpec((1, t, hw), lambda i: (i, 0, 0)),
                  pl.BlockSpec((1, t, hw), lambda i: (i, 0, 1)),
                  pl.BlockSpec((2, CMP_LEN, hw), lambda i: (0, 0, 0)),
                  pl.BlockSpec((2, CMP_LEN, hw, NSA_KV * CMP_HIDDEN), lambda i: (0, 0, 0, 0)),
                  pl.BlockSpec((2, NSA_KV * CMP_HIDDEN, hw), lambda i: (0, 0, 0)),
                  pl.BlockSpec((1, 128), c2), pl.BlockSpec((128, 128), c2)],
        out_specs=pl.BlockSpec((1, m, KV_LANES), lambda i: (i, 0, 0)),
        out_shape=jax.ShapeDtypeStruct((b, m, KV_LANES), F32),
        compiler_params=_cp("arbitrary"),
    )(kv, kv, pe, w1, w2, kn, g128)


def _msoftmax(s, mask):
    s = jnp.where(mask, s, NEG)
    mx = jnp.max(s, axis=-1, keepdims=True)
    e = jnp.where(mask, jnp.exp(s - mx), 0.0)
    den = jnp.sum(e, axis=-1, keepdims=True)
    return e * (1.0 / jnp.where(den > 0, den, 1.0))


def _select_blocks(imp, qpos, ns):
    s_io = lax.broadcasted_iota(jnp.int32, imp.shape, 1)
    cur = qpos // SEL_LEN
    forced = (s_io == 0) | (s_io == cur) | (s_io == cur - 1)
    future = s_io * SEL_LEN > qpos
    imp = jnp.where(forced, FORCE_SCORE, imp)
    imp = jnp.where(future, -FORCE_SCORE, imp)
    imp = jnp.where(s_io >= ns, -3e38, imp)
    rank = jnp.zeros(imp.shape, jnp.int32)
    for sp in range(ns):
        col = imp[:, sp:sp + 1]
        beats = (col > imp) | ((col == imp) & (s_io > sp))
        rank = rank + beats.astype(jnp.int32)
    sel = (rank < min(N_SELECT, ns)) & (s_io < ns)
    return sel.astype(F32)


def _heads_to_rows(q, g):
    return jnp.concatenate(
        [q[:, HEAD_DIM * (NSA_QPG * g + h):HEAD_DIM * (NSA_QPG * g + h + 1)] for h in range(NSA_QPG)],
        axis=0)


def _msoftmax0(s, mask):
    s = jnp.where(mask, s, NEG)
    mx = jnp.max(s, axis=0, keepdims=True)
    e = jnp.where(mask, jnp.exp(s - mx), 0.0)
    den = jnp.sum(e, axis=0, keepdims=True)
    return e * (1.0 / jnp.where(den > 0, den, 1.0))


def _select_blocks_t(imp, qpos, ns):
    s_io = lax.broadcasted_iota(jnp.int32, imp.shape, 0)
    cur = qpos // SEL_LEN
    forced = (s_io == 0) | (s_io == cur) | (s_io == cur - 1)
    future = s_io * SEL_LEN > qpos
    imp = jnp.where(forced, FORCE_SCORE, imp)
    imp = jnp.where(future, -FORCE_SCORE, imp)
    imp = jnp.where(s_io >= ns, -3e38, imp)
    rank = jnp.zeros(imp.shape, jnp.int32)
    for sp in range(ns):
        row = imp[sp:sp + 1, :]
        beats = (row > imp) | ((row == imp) & (s_io > sp))
        rank = rank + beats.astype(jnp.int32)
    sel = (rank < min(N_SELECT, ns)) & (s_io < ns)
    return sel.astype(F32)


def _flash_steps_t(chains):
    sts = [_nt(k, qg) for (qg, k, _, _, _, _) in chains]
    mids = []
    for st, (_, _, _, bias, madd, (m_, l_, _)) in zip(sts, chains):
        ps, ms, ls, als = [], [], [], []
        for h in range(NSA_QPG):
            hs = slice(KT * h, KT * (h + 1))
            s = st[:, hs]
            if bias is not None:
                s = s + bias[:, hs]
            if madd is not None:
                s = s + madd
            m_new = jnp.maximum(m_[:, hs], jnp.max(s, axis=0, keepdims=True))
            alpha = jnp.exp(m_[:, hs] - m_new)
            e = jnp.exp(s - m_new)
            ls.append(alpha * l_[:, hs] + jnp.sum(e, axis=0, keepdims=True))
            ms.append(m_new)
            als.append(alpha)
            ps.append(e.astype(BF16))
        mids.append((jnp.concatenate(ps, axis=1), jnp.concatenate(ms, axis=1),
                     jnp.concatenate(ls, axis=1), jnp.concatenate(als, axis=1)))
    out = []
    for (p, m_new, l_new, alpha), (_, _, v, _, _, (_, _, acc)) in zip(mids, chains):
        out.append((m_new, l_new, alpha * acc + _tn(v, p)))
    return out


def _flash_init_t():
    return (jnp.full((1, NSA_QPG * KT), NEG, F32), jnp.zeros((1, NSA_QPG * KT), F32),
            jnp.zeros((HEAD_DIM, NSA_QPG * KT), F32))


def _flash_out_t(carry):
    _, l_, acc = carry
    return acc * (1.0 / jnp.where(l_ > 0, l_, 1.0))


def _nsa_prompt_kernel(q_ref, gt_ref, cc_ref, kv_ref, bc_ref, tb_ref, ovt_ref, ext_ref, o_ref, selk_s, sc_s,
                       *, m, ns):
    i = pl.program_id(1)
    q = q_ref[0]
    gst = jax.nn.sigmoid(gt_ref[...]).T
    qpos = i * KT + lax.broadcasted_iota(jnp.int32, (1, KT), 1)
    n_io = lax.broadcasted_iota(jnp.int32, (m, KT), 0)
    mask_c = ((n_io * CMP_STRIDE + (CMP_LEN - 1)) <= qpos) & (n_io < m - 1)
    cc = cc_ref[0]
    groups = range(NSA_KV)
    qgs = [_heads_to_rows(q, g) for g in groups]

    kcs = [cc[:, HEAD_DIM * g:HEAD_DIM * (g + 1)].astype(BF16) for g in groups]
    vcs = [cc[:, 128 + HEAD_DIM * g:128 + HEAD_DIM * (g + 1)].astype(BF16) for g in groups]
    c_off = pl.multiple_of(jnp.maximum(i - 1, 0) * (KT // CMP_STRIDE), KT // CMP_STRIDE)
    sts = []
    for g in groups:
        sc_s[g] = _nt(kcs[g], qgs[g])
        band = jnp.where(i == 0, bc_ref[g, 1], bc_ref[g, 0])
        sc_s[g, pl.ds(c_off, 2 * KT // CMP_STRIDE), :] += band
        sts.append(sc_s[g])
    pcs = [[_msoftmax0(sts[g][:, KT * h:KT * (h + 1)], mask_c) for h in range(NSA_QPG)]
           for g in groups]
    o_cs = [_tn(vcs[g], jnp.concatenate(pcs[g], axis=1).astype(BF16)) for g in groups]
    imps = [jnp.dot(ovt_ref[...], pcs[g][0] + pcs[g][1] + pcs[g][2] + pcs[g][3], precision=HI,
                    preferred_element_type=F32) for g in groups]
    sels = [_select_blocks_t(imps[g], qpos, ns).astype(BF16) for g in groups]
    for g in groups:
        selk_s[g] = (jnp.dot(ext_ref[...], sels[g], preferred_element_type=F32) - 1.0) * (-NEG)

    def load_kv(g, kt, off, ntile=1):
        r0 = pl.multiple_of(kt * KT, KT)
        k_lo = 256 + off + HEAD_DIM * g
        v_lo = 384 + off + HEAD_DIM * g
        k = kv_ref[0, pl.ds(r0, ntile * KT), k_lo:k_lo + HEAD_DIM].astype(BF16)
        v = kv_ref[0, pl.ds(r0, ntile * KT), v_lo:v_lo + HEAD_DIM].astype(BF16)
        return k, v

    def sel_mask(g, kt, ntile=1):
        r0 = pl.multiple_of(kt * KT, KT)
        return selk_s[g, pl.ds(r0, ntile * KT), :]

    n_main = jnp.maximum(i - 1, 0)

    def body(pr, carry):
        chains = []
        for g in groups:
            k, v = load_kv(g, 2 * pr, 0, 2)
            chains.append((qgs[g], k, v, None, sel_mask(g, 2 * pr, 2), carry[g]))
        return tuple(_flash_steps_t(chains))

    c_sel = list(lax.fori_loop(0, n_main // 2, body, tuple(_flash_init_t() for _ in groups)))
    kt_odd = (n_main // 2) * 2
    has_odd = n_main % 2 == 1

    nwt = WINDOW // KT
    c_win = [_flash_init_t() for _ in groups]
    for d in range(nwt, -1, -1):
        kt = i - d
        ktc = jnp.maximum(kt, 0)
        valid = None if d == 0 else jnp.where(kt >= 0, 0.0, NEG)
        chains = []
        for g in groups:
            k, v = load_kv(g, ktc, 256)
            bias = tb_ref[g, {nwt: 2, 1: 0, 0: 1}[d]] if d in (nwt, 1, 0) else None
            chains.append((qgs[g], k, v, bias, valid, c_win[g]))
        if d == nwt:
            for g in groups:
                k, v = load_kv(g, kt_odd, 0)
                madd = sel_mask(g, kt_odd) + jnp.where(has_odd, 0.0, NEG)
                chains.append((qgs[g], k, v, None, madd, c_sel[g]))
        elif d <= 1:
            for g in groups:
                k, v = load_kv(g, ktc, 0)
                madd = sel_mask(g, ktc) if d == 0 else sel_mask(g, ktc) + valid
                chains.append((qgs[g], k, v, tb_ref[g, 1 - d], madd, c_sel[g]))
        res = _flash_steps_t(chains)
        c_win = res[:NSA_KV]
        if len(res) > NSA_KV:
            c_sel = res[NSA_KV:]
    o_ss = [_flash_out_t(c_sel[g]) for g in groups]
    o_ws = [_flash_out_t(c_win[g]) for g in groups]

    for g in groups:
        for h in range(NSA_QPG):
            hh = NSA_QPG * g + h
            hs = slice(KT * h, KT * (h + 1))
            o = (gst[3 * hh:3 * hh + 1, :] * o_cs[g][:, hs] + gst[3 * hh + 1:3 * hh + 2, :] * o_ss[g][:, hs]
                 + gst[3 * hh + 2:3 * hh + 3, :] * o_ws[g][:, hs])
            o_ref[0, HEAD_DIM * hh:HEAD_DIM * (hh + 1), :] = o.astype(BF16)


def _nsa_prompt(qn, z_nsa, cc, kv, bct, tbt, ovt, ext, b, t_len):
    m = cc.shape[1]
    ns = -(-t_len // SEL_LEN)
    nq = t_len // KT
    nsp = ovt.shape[0]
    c2 = lambda bi, i: (0, 0)
    return pl.pallas_call(
        functools.partial(_nsa_prompt_kernel, m=m, ns=ns), name="nsa_prompt",
        grid=(b, nq),
        in_specs=[pl.BlockSpec((1, KT, 512), lambda bi, i: (bi, i, 0)),
                  pl.BlockSpec((KT, LANE), lambda bi, i: (bi * nq + i, 0)),
                  pl.BlockSpec((1, m, KV_LANES), lambda bi, i: (bi, 0, 0)),
                  pl.BlockSpec((1, t_len, 768), lambda bi, i: (bi, 0, 0)),
                  pl.BlockSpec((NSA_KV, 2, 2 * KT // CMP_STRIDE, NSA_QPG * KT), lambda bi, i: (0, 0, 0, 0)),
                  pl.BlockSpec((NSA_KV, 3, KT, NSA_QPG * KT), lambda bi, i: (0, 0, 0, 0)),
                  pl.BlockSpec((nsp, m), c2),
                  pl.BlockSpec((t_len, nsp), c2)],
        out_specs=pl.BlockSpec((1, 512, KT), lambda bi, i: (bi, 0, i)),
        out_shape=jax.ShapeDtypeStruct((b, 512, t_len), BF16),
        scratch_shapes=[pltpu.VMEM((NSA_KV, t_len, KT), F32),
                        pltpu.VMEM((NSA_KV, m, NSA_QPG * KT), F32)],
        compiler_params=_cp("arbitrary", "arbitrary"),
    )(qn.reshape(b, t_len, 512), z_nsa, cc, kv.reshape(b, t_len, 768), bct, tbt, ovt, ext)


def _nsa_paged_kernel(pt_ref, cmp_hbm, slc_hbm, new_ref, win_ref, q_ref, gt_ref, pe_ref, w1_ref, w2_ref,
                      kn_ref, g128_ref, bc_ref, bs_ref, bw_ref, ov_ref, ex_ref, o_ref, wo_ref,
                      cmpt_s, slct_s, cmpk_s, cmpv_s, sems, *, li, nsteps, nbr, npages, past, wb, tq, m, ns):
    step = pl.program_id(0)
    slot = step % 2

    def page_copy(st_, bb, p, sl, which):
        hbm, buf = ((cmp_hbm, cmpt_s), (slc_hbm, slct_s))[which]
        return pltpu.make_async_copy(hbm.at[pt_ref[st_ * nbr + bb, p], li],
                                     buf.at[sl, bb, :, pl.ds(p * PAGE, PAGE)], sems.at[sl, which])

    def start_step(st_, sl):
        for bb in range(nbr):
            for p in range(npages):
                page_copy(st_, bb, p, sl, 0).start()
                page_copy(st_, bb, p, sl, 1).start()

    @pl.when(step == 0)
    def _():
        start_step(0, 0)

    @pl.when(step + 1 < nsteps)
    def _():
        start_step(step + 1, 1 - slot)

    for bb in range(nbr):
        for p in range(npages):
            page_copy(step, bb, p, slot, 0).wait()
            page_copy(step, bb, p, slot, 1).wait()

    hw = NSA_KV * HEAD_DIM
    for bb in range(nbr):
        for p in range(npages):
            blk = cmpt_s[slot, bb, :, p * PAGE:(p + 1) * PAGE]
            r0 = bb * past + p * PAGE
            cmpk_s[r0:r0 + PAGE, :] = blk[0:hw].T
            cmpv_s[r0:r0 + PAGE, :] = blk[hw:2 * hw].T
    cc_all = _compress_rows((cmpk_s, cmpv_s), nbr * m, pe_ref, w1_ref, w2_ref, kn_ref, g128_ref)

    pairs = [(bb, g) for bb in range(nbr) for g in range(NSA_KV)]
    groups = range(len(pairs))
    ccs = [cc_all[m * bb:m * (bb + 1)] for bb in range(nbr)]
    newt32 = [jnp.concatenate([new_ref[bb], jnp.zeros((LANE - tq, 768), F32)], axis=0).T
              for bb in range(nbr)]
    newts = [n_.astype(BF16) for n_ in newt32]
    for bb in range(nbr):
        wo_ref[bb] = jnp.concatenate([win_ref[bb, 0, :, wb + tq - WINDOW:wb],
                                      newt32[bb][2 * KV_LANES:3 * KV_LANES, 0:tq]], axis=1)
    gss = [jax.nn.sigmoid(gt_ref[bb]) for bb in range(nbr)]
    kp = past + LANE
    wp = wb + LANE
    qpos = past + lax.broadcasted_iota(jnp.int32, (tq, 1), 0)
    n_io = lax.broadcasted_iota(jnp.int32, (tq, m), 1)
    mask_c = ((n_io * CMP_STRIDE + (CMP_LEN - 1)) <= qpos) & (n_io < m - 1)
    k_io = lax.broadcasted_iota(jnp.int32, (tq, kp), 1)
    mask_s0 = (k_io <= qpos) & (k_io < past + tq)
    j_io = lax.broadcasted_iota(jnp.int32, (tq, wp), 1)
    dist_w = qpos - (past - wb + j_io)
    mask_w = (dist_w >= 0) & (dist_w < WINDOW) & (j_io < wb + tq)
    rep = lambda a_: jnp.concatenate([a_] * NSA_QPG, axis=0)
    mask_c4, mask_w4 = rep(mask_c), rep(mask_w)
    gsl = lambda g, off: slice(off + HEAD_DIM * g, off + HEAD_DIM * (g + 1))
    qgs = [_heads_to_rows(q_ref[bb], g) for bb, g in pairs]
    bof = [bb for bb, _ in pairs]
    gof = [g for _, g in pairs]

    s_c = [_nt(qgs[i], ccs[bof[i]][:, gsl(gof[i], 0)].astype(BF16)) + bc_ref[gof[i]] for i in groups]
    p_c = [_msoftmax(s_c[i], mask_c4) for i in groups]
    o_c = [jnp.dot(p_c[i].astype(BF16), ccs[bof[i]][:, gsl(gof[i], hw)].astype(BF16),
                   preferred_element_type=F32) for i in groups]
    imps = [jnp.dot(p_c[i][0:tq] + p_c[i][tq:2 * tq] + p_c[i][2 * tq:3 * tq] + p_c[i][3 * tq:4 * tq],
                    ov_ref[...], precision=HI, preferred_element_type=F32) for i in groups]
    sels = [_select_blocks(imps[i], qpos, ns).astype(BF16) for i in groups]
    selk = [jnp.dot(sels[i], ex_ref[...], preferred_element_type=F32) for i in groups]

    s_s = [jnp.concatenate(
        [jnp.dot(qgs[i], slct_s[slot, bof[i], gsl(gof[i], 0), :].astype(BF16), preferred_element_type=F32),
         jnp.dot(qgs[i], newts[bof[i]][gsl(gof[i], 256)], preferred_element_type=F32)], axis=1)
        + bs_ref[gof[i]] for i in groups]
    p_s = [_msoftmax(s_s[i], rep((selk[i] > 0.5) & mask_s0)) for i in groups]
    o_s = [_nt(p_s[i][:, 0:past].astype(BF16), slct_s[slot, bof[i], gsl(gof[i], hw), :].astype(BF16))
           + _nt(p_s[i][:, past:kp].astype(BF16), newts[bof[i]][gsl(gof[i], 256 + hw)]) for i in groups]

    s_w = [jnp.concatenate(
        [jnp.dot(qgs[i], win_ref[bof[i], 0, gsl(gof[i], 0), :].astype(BF16), preferred_element_type=F32),
         jnp.dot(qgs[i], newts[bof[i]][gsl(gof[i], 512)], preferred_element_type=F32)], axis=1)
        + bw_ref[gof[i]] for i in groups]
    p_w = [_msoftmax(s_w[i], mask_w4) for i in groups]
    o_w = [_nt(p_w[i][:, 0:wb].astype(BF16), win_ref[bof[i], 0, gsl(gof[i], hw), :].astype(BF16))
           + _nt(p_w[i][:, wb:wp].astype(BF16), newts[bof[i]][gsl(gof[i], 512 + hw)]) for i in groups]

    for bb in range(nbr):
        outs = []
        gs = gss[bb]
        for g in range(NSA_KV):
            i = bb * NSA_KV + g
            for h in range(NSA_QPG):
                hh = NSA_QPG * g + h
                rs = slice(tq * h, tq * (h + 1))
                outs.append(gs[:, 3 * hh:3 * hh + 1] * o_c[i][rs] + gs[:, 3 * hh + 1:3 * hh + 2] * o_s[i][rs]
                            + gs[:, 3 * hh + 2:3 * hh + 3] * o_w[i][rs])
        o_ref[bb] = jnp.concatenate(outs, axis=1).astype(BF16)


def _nsa_paged(pt, pool_cmp_t, pool_slc_t, li, kv_new, win_t, qn, z_nsa, cw, bc, bs, bw, ov, ex, b, tq):
    npages = pt.shape[1]
    past = npages * PAGE
    wb = win_t.shape[3]
    assert past % CMP_STRIDE == 0 and tq < CMP_STRIDE
    assert wb + tq >= WINDOW
    kp = past + LANE
    wp = wb + LANE
    m = (past + tq) // CMP_STRIDE
    ns = -(-(past + tq) // SEL_LEN)
    rows = NSA_QPG * tq
    pe, w1, w2, kn, g128 = cw
    hw = NSA_KV * HEAD_DIM
    c2 = lambda bi, pt_: (0, 0)
    c3 = lambda bi, pt_: (0, 0, 0)
    per_b3 = lambda bi, pt_: (bi, 0, 0)
    nbr = 2 if b % 2 == 0 else 1
    grid_spec = pltpu.PrefetchScalarGridSpec(
        num_scalar_prefetch=1,
        grid=(b // nbr,),
        in_specs=[pl.BlockSpec(memory_space=pl.ANY),
                  pl.BlockSpec(memory_space=pl.ANY),
                  pl.BlockSpec((nbr, tq, 768), per_b3),
                  pl.BlockSpec((nbr, 1, KV_LANES, wb), lambda bi, pt_: (bi, li, 0, 0)),
                  pl.BlockSpec((nbr, tq, 512), per_b3),
                  pl.BlockSpec((nbr, tq, LANE), lambda bi, pt_: (bi, 0, 0)),
                  pl.BlockSpec((2, CMP_LEN, hw), c3),
                  pl.BlockSpec((2, CMP_LEN, hw, NSA_KV * CMP_HIDDEN), lambda bi, pt_: (0, 0, 0, 0)),
                  pl.BlockSpec((2, NSA_KV * CMP_HIDDEN, hw), c3),
                  pl.BlockSpec((1, 128), c2), pl.BlockSpec((128, 128), c2),
                  pl.BlockSpec((NSA_KV, rows, m), c3),
                  pl.BlockSpec((NSA_KV, rows, kp), c3),
                  pl.BlockSpec((NSA_KV, rows, wp), c3),
                  pl.BlockSpec((m, LANE), c2),
                  pl.BlockSpec((LANE, kp), c2)],
        out_specs=[pl.BlockSpec((nbr, tq, 512), per_b3),
                   pl.BlockSpec((nbr, KV_LANES, WINDOW), per_b3)],
        scratch_shapes=[pltpu.VMEM((2, nbr, KV_LANES, past), F32), pltpu.VMEM((2, nbr, KV_LANES, past), F32),
                        pltpu.VMEM((nbr * past, hw), F32), pltpu.VMEM((nbr * past, hw), F32),
                        pltpu.SemaphoreType.DMA((2, 2))])
    kern = functools.partial(_nsa_paged_kernel, li=li, nsteps=b // nbr, nbr=nbr, npages=npages, past=past,
                             wb=wb, tq=tq, m=m, ns=ns)
    return pl.pallas_call(
        kern, grid_spec=grid_spec, name="nsa_paged",
        out_shape=[jax.ShapeDtypeStruct((b, tq, 512), BF16),
                   jax.ShapeDtypeStruct((b, KV_LANES, WINDOW), F32)],
        compiler_params=_cp("arbitrary"),
    )(pt, pool_cmp_t, pool_slc_t, kv_new.reshape(b, tq, 768), win_t, qn.reshape(b, tq, 512),
      z_nsa.reshape(b, tq, LANE), pe, w1, w2, kn, g128, bc, bs, bw, ov, ex)


def _rwkv_kernel(z_ref, zp_ref, s0_ref, mu_ref, w0_ref, w2_ref, a0_ref, a2_ref, g2_ref, kkp_ref, ka_ref,
                 rk_ref, lnw_ref, lnb_ref, y_ref, st_ref, s_scr, carry_scr, *, c_len, nb):
    c = pl.program_id(1)
    cl = c_len
    nr = nb * cl

    @pl.when(c == 0)
    def _():
        s_scr[...] = s0_ref[...]
        carry_scr[...] = zp_ref[...]

    z3 = z_ref[...]
    row = lax.broadcasted_iota(jnp.int32, z3.shape, 1)
    shifted = jnp.where(row == 0, carry_scr[...], pltpu.roll(z3, 1, 1))
    carry_scr[...] = z3[:, cl - 1:cl, :]
    zz = (z3 + (shifted - z3) * mu_ref[...]).reshape(nr, RWKV_PROJ)
    w = RWKV_WIDTH
    r = zz[:, 0:w]
    k = zz[:, w:2 * w]
    v = zz[:, 2 * w:3 * w]
    zw = zz[:, 3 * w:3 * w + DECAY_LORA]
    za = zz[:, 3 * w + DECAY_LORA:3 * w + DECAY_LORA + AAA_LORA]
    zg = zz[:, 3 * w + DECAY_LORA + AAA_LORA:]
    xw = -(w0_ref[...] + _mmp(jnp.tanh(zw), w2_ref[...], _NN, RWKV_LORA_MM))
    softplus = jnp.maximum(xw, 0.0) + jnp.log(1.0 + jnp.exp(-jnp.abs(xw)))
    logdec = -jnp.exp(-softplus - 0.5)
    a = jax.nn.sigmoid(a0_ref[...] + _mmp(za, a2_ref[...], _NN, RWKV_LORA_MM))
    gate = _mmp(jax.nn.sigmoid(zg), g2_ref[...], _NN, RWKV_LORA_MM)
    k_mod = k * (1.0 + (a - 1.0) * ka_ref[...])
    kku = k * kkp_ref[...]

    ti = lax.broadcasted_iota(jnp.int32, (nr, nr), 0)
    si = lax.broadcasted_iota(jnp.int32, (nr, nr), 1)
    tri = ((si <= ti) & (si >= (ti // cl) * cl)).astype(BF16)
    ld_hi = logdec.astype(BF16)
    ld_r = logdec - ld_hi.astype(F32)
    ld_mid = ld_r.astype(BF16)
    ld_lo = (ld_r - ld_mid.astype(F32)).astype(BF16)
    gcum = (jnp.dot(tri, ld_hi, preferred_element_type=F32)
            + (jnp.dot(tri, ld_mid, preferred_element_type=F32)
               + jnp.dot(tri, ld_lo, preferred_element_type=F32)))
    e_g_all = jnp.exp(gcum)
    e_gm_all = jnp.exp(gcum - logdec)
    e_ng_all = jnp.exp(-gcum)
    nsteps = max(1, int(math.ceil(math.log2(cl))))
    row2 = lax.broadcasted_iota(jnp.int32, (cl, 2 * cl), 0)
    col2 = lax.broadcasted_iota(jnp.int32, (cl, 2 * cl), 1)
    col2 = jnp.where(col2 >= cl, col2 - cl, col2)
    strict2 = col2 < row2
    lower2 = col2 <= row2
    keep_z = lax.broadcasted_iota(jnp.int32, (cl, cl + HEAD_DIM), 1) >= cl

    pairs = [(bb, hd) for bb in range(nb) for hd in range(RWKV_HEADS)]
    heads = range(len(pairs))
    rws = [slice(cl * bb, cl * (bb + 1)) for bb, _ in pairs]
    sls = [slice(HEAD_DIM * hd, HEAD_DIM * (hd + 1)) for _, hd in pairs]
    kks = [kku[rws[h], sls[h]] for h in heads]
    kks = [kk * lax.rsqrt(jnp.maximum(jnp.sum(kk * kk, axis=-1, keepdims=True), 1e-24)) for kk in kks]
    kkas = [kks[h] * a[rws[h], sls[h]] for h in heads]
    kms = [k_mod[rws[h], sls[h]] for h in heads]
    vhs = [v[rws[h], sls[h]] for h in heads]
    rhs_ = [r[rws[h], sls[h]] for h in heads]
    s_hs = [s_scr[bb, hd] for bb, hd in pairs]
    e_g = [e_g_all[rws[h], sls[h]] for h in heads]
    e_gm = [e_gm_all[rws[h], sls[h]] for h in heads]
    e_ng = [e_ng_all[rws[h], sls[h]] for h in heads]
    glast = [gcum[cl * (bb + 1) - 1:cl * (bb + 1), sls[h]] for h, (bb, _) in enumerate(pairs)]
    e_lg = [jnp.exp(glast[h] - gcum[rws[h], sls[h]]) for h in heads]
    e_l = [jnp.exp(glast[h]) for h in heads]
    lrs = [jnp.concatenate([-kks[h] * e_gm[h], rhs_[h] * e_g[h]], axis=0) for h in heads]
    rrs = [jnp.concatenate([kkas[h] * e_ng[h], kms[h] * e_ng[h]], axis=0) for h in heads]
    m1s = [_mmp(lrs[h], rrs[h], _NT) for h in heads]
    m2s = [_mmp(lrs[h], s_hs[h], _NT) for h in heads]
    tops = [jnp.where(strict2, m1[0:cl], 0.0) for m1 in m1s]
    bots = [jnp.where(lower2, m1[cl:2 * cl], 0.0) for m1 in m1s]
    zeros_v = jnp.zeros((cl, HEAD_DIM), F32)
    rhss = [m2s[h][0:cl] + _mmp(tops[h], jnp.concatenate([zeros_v, vhs[h]], axis=0), _NN) for h in heads]
    wms = [jnp.concatenate([tops[h][:, 0:cl], rhss[h]], axis=1) for h in heads]
    for _ in range(nsteps):
        wms = [_mmp(w_[:, 0:cl], w_, _NN) + jnp.where(keep_z, w_, 0.0) for w_ in wms]
    zvs = [jnp.concatenate([wms[h][:, cl:cl + HEAD_DIM], vhs[h]], axis=0) for h in heads]
    ys = [m2s[h][cl:2 * cl] + _mmp(bots[h], zvs[h], _NN) for h in heads]
    bkhs = [jnp.concatenate([kkas[h] * e_lg[h], kms[h] * e_lg[h]], axis=0) for h in heads]
    s_new = [s_hs[h] * e_l[h] + _mmp(zvs[h], bkhs[h], _TN) for h in heads]
    for h, (bb, hd) in enumerate(pairs):
        s_scr[bb, hd] = s_new[h]
    outs = []
    for h in heads:
        y = ys[h]
        sl = sls[h]
        mean = jnp.mean(y, axis=-1, keepdims=True)
        yc = y - mean
        var = jnp.mean(yc * yc, axis=-1, keepdims=True)
        yn = yc * lax.rsqrt(var + GN_EPS) * lnw_ref[:, sl] + lnb_ref[:, sl]
        bonus = jnp.sum(rhs_[h] * kms[h] * rk_ref[:, sl], axis=-1, keepdims=True) * vhs[h]
        outs.append(yn + bonus)
    for bb in range(nb):
        yb = jnp.concatenate(outs[RWKV_HEADS * bb:RWKV_HEADS * (bb + 1)], axis=1)
        y_ref[bb] = (yb * gate[cl * bb:cl * (bb + 1)]).astype(BF16)
    st_ref[...] = s_scr[...]


def _rwkv(z_rw, z_prev, s0, pr, b, t_len, c_len, nb):
    nchunk = t_len // c_len
    c2 = lambda bi, c: (0, 0)
    vec = lambda n: pl.BlockSpec((1, n), c2)
    return pl.pallas_call(
        functools.partial(_rwkv_kernel, c_len=c_len, nb=nb), name="rwkv",
        grid=(b // nb, nchunk),
        in_specs=[pl.BlockSpec((nb, c_len, RWKV_PROJ), lambda bi, c: (bi, c, 0)),
                  pl.BlockSpec((nb, 1, RWKV_PROJ), lambda bi, c: (bi, 0, 0)),
                  pl.BlockSpec((nb, RWKV_HEADS, HEAD_DIM, HEAD_DIM), lambda bi, c: (bi, 0, 0, 0)),
                  vec(RWKV_PROJ), vec(512), pl.BlockSpec((DECAY_LORA, 512), c2),
                  vec(512), pl.BlockSpec((AAA_LORA, 512), c2), pl.BlockSpec((GATE_LORA, 512), c2),
                  vec(512), vec(512), vec(512), vec(512), vec(512)],
        out_specs=[pl.BlockSpec((nb, c_len, 512), lambda bi, c: (bi, c, 0)),
                   pl.BlockSpec((nb, RWKV_HEADS, HEAD_DIM, HEAD_DIM), lambda bi, c: (bi, 0, 0, 0))],
        out_shape=[jax.ShapeDtypeStruct((b, t_len, 512), BF16),
                   jax.ShapeDtypeStruct((b, RWKV_HEADS, HEAD_DIM, HEAD_DIM), F32)],
        scratch_shapes=[pltpu.VMEM((nb, RWKV_HEADS, HEAD_DIM, HEAD_DIM), F32),
                        pltpu.VMEM((nb, 1, RWKV_PROJ), F32)],
        compiler_params=_cp("arbitrary", "arbitrary"),
    )(z_rw.reshape(b, t_len, RWKV_PROJ), z_prev.reshape(b, 1, RWKV_PROJ), s0, *pr)


def _rel_bucket(dist):
    n = jnp.maximum(dist, 0)
    max_exact = NUM_BUCKETS // 2
    nf = jnp.maximum(n, 1).astype(F32)
    large = max_exact + (jnp.log(nf / max_exact) / math.log(MAX_DISTANCE / max_exact)
                         * (NUM_BUCKETS - max_exact)).astype(jnp.int32)
    large = jnp.minimum(large, NUM_BUCKETS - 1)
    return jnp.where(n < max_exact, n, large)


def _bias_table(rel_bias, dist, delta):
    tb = rel_bias.astype(F32)
    out = jnp.moveaxis(tb[_rel_bucket(dist)], -1, 0)
    if delta:
        out = out - tb[NUM_BUCKETS - 1].reshape((NSA_HEADS,) + (1,) * dist.ndim)
    return out


def _toeplitz(u, nrows, ncols, base, step):
    lo = base - step * (nrows - 1)
    ln = ncols + step * (nrows - 1)
    lead = u.shape[:-1]
    w = jnp.broadcast_to(u[..., None, lo:lo + ln], lead + (nrows, ln)).reshape(lead + (nrows * ln,))
    w = jnp.pad(w, [(0, 0)] * len(lead) + [(0, nrows * step)]).reshape(lead + (nrows, ln + step))
    return w[..., ::-1, :ncols]


def _block_diag_ones(n, grp):
    idx = jnp.arange(n) // grp
    return (idx[:, None] == idx[None, :]).astype(F32) / grp


def _compress_weights(phi_pe, phi_w1, phi_w2, kn_cmp):
    pe = jnp.concatenate([phi_pe, phi_pe], axis=-1)
    w1 = jnp.zeros((2, CMP_LEN, NSA_KV * HEAD_DIM, NSA_KV * CMP_HIDDEN), F32)
    w2 = jnp.zeros((2, NSA_KV * CMP_HIDDEN, NSA_KV * HEAD_DIM), F32)
    for g in range(NSA_KV):
        w1 = w1.at[:, :, HEAD_DIM * g:HEAD_DIM * (g + 1), CMP_HIDDEN * g:CMP_HIDDEN * (g + 1)].set(phi_w1)
        w2 = w2.at[:, CMP_HIDDEN * g:CMP_HIDDEN * (g + 1), HEAD_DIM * g:HEAD_DIM * (g + 1)].set(phi_w2)
    kn = jnp.tile(kn_cmp, NSA_KV).reshape(1, 128)
    return pe, w1.astype(BF16), w2.astype(BF16), kn


def _overlap_expand(t_all):
    nc = t_all // CMP_STRIDE - 1
    m = nc + 1
    ns = -(-t_all // SEL_LEN)
    c0 = jnp.arange(m)[:, None] * CMP_STRIDE
    s0 = jnp.arange(LANE)[None, :] * SEL_LEN
    ov = ((c0 < s0 + SEL_LEN) & (c0 + CMP_LEN > s0) & (jnp.arange(m)[:, None] < nc)
          & (jnp.arange(LANE)[None, :] < ns)).astype(F32)
    kp = _round_up(t_all, LANE)
    ex = (jnp.arange(kp)[None, :] // SEL_LEN == jnp.arange(LANE)[:, None]).astype(BF16)
    return ov, ex


def _pick_tm(t_len):
    return 512 if t_len % 512 == 0 else 128


def _trunk(x3, prm, st):
    b, t_len, d = x3.shape
    m_rows = b * t_len
    x = x3.reshape(m_rows, d)
    prompt = st is None
    if prompt:
        tm = _pick_tm(t_len)
        c_len = 64
    else:
        tm = 256 if m_rows % 256 == 0 else m_rows
        c_len = t_len
    tm_f = tm
    g512 = _block_diag_ones(512, HEAD_DIM)
    g128 = _block_diag_ones(128, HEAD_DIM)
    rel_bias = prm["rel_bias"]

    if prompt:
        ov, ex = _overlap_expand(t_len)
        nsp = _round_up(-(-t_len // SEL_LEN), 8)
        ovt = ov[:, :nsp].T
        ext = ex[:nsp, :t_len].T
        off = 3 * KT
        u = _bias_table(rel_bias, jnp.arange(-off, 2 * KT), True)

        def band(rows, base, step):
            t_ = _toeplitz(u, rows, KT, off + base, step).reshape(NSA_KV, NSA_QPG, rows, KT)
            return jnp.transpose(t_, (0, 2, 1, 3)).reshape(NSA_KV, rows, NSA_QPG * KT)

        after = jnp.tile(jnp.arange(KT)[:, None] > jnp.arange(KT)[None, :], (1, NSA_QPG))
        neg_after = jnp.broadcast_to(jnp.where(after, NEG, 0.0), (NSA_KV, KT, NSA_QPG * KT))
        neg_upto = jnp.broadcast_to(jnp.where(after, 0.0, NEG), (NSA_KV, KT, NSA_QPG * KT))
        tbt = jnp.stack([band(KT, KT, 1), band(KT, 0, 1) + neg_after, neg_upto], axis=1)
        ncb = 2 * KT // CMP_STRIDE
        bct = jnp.stack([band(ncb, KT - (CMP_LEN - 1), CMP_STRIDE),
                         band(ncb, -(CMP_LEN - 1), CMP_STRIDE)], axis=1)
    else:
        pt = st["page_table"]
        past = pt.shape[1] * PAGE
        wb = st["win"].shape[2]
        t_all = past + t_len
        ov, ex = _overlap_expand(t_all)
        mcmp = t_all // CMP_STRIDE
        kp = past + LANE
        wp = wb + LANE
        qp = past + jnp.arange(t_len)[:, None]
        cend = jnp.arange(mcmp)[None, :] * CMP_STRIDE + (CMP_LEN - 1)
        rows_ = NSA_QPG * t_len
        bc = _bias_table(rel_bias, qp - cend, False).reshape(NSA_KV, rows_, mcmp)
        near = _bias_table(rel_bias, jnp.arange(-LANE, 2 * LANE), True)[:, ::-1]

        def by_distance(d_top, ncols):
            u_r = jnp.concatenate([jnp.zeros((NSA_HEADS, d_top - (2 * LANE - 1)), F32), near], axis=1)
            return _toeplitz(u_r, t_len, ncols, t_len - 1, 1).reshape(NSA_KV, rows_, ncols)

        bs = by_distance(past + t_len - 1, kp)
        bw = by_distance(wb + t_len - 1, wp)
        pool_cmp = jnp.swapaxes(st["cmp"].reshape(st["cmp"].shape[0], st["cmp"].shape[1], PAGE, KV_LANES), 2, 3)
        pool_slc = jnp.swapaxes(st["slc"].reshape(st["slc"].shape[0], st["slc"].shape[1], PAGE, KV_LANES), 2, 3)
        win_all = st["win"].reshape(b, st["win"].shape[1], wb, KV_LANES)
        win_all_t = jnp.swapaxes(win_all, 2, 3)

    cmp_r, slc_r, win_r, wkv_r, sh_r, conv_r, ffn_r = [], [], [], [], [], [], []
    mix_pending = None
    for l in range(4):
        li = l // 2
        if l % 2 == 0:
            w_in = prm["w_in_even"][li].T
            w_nsa = jnp.pad(w_in[:NSA_PROJ], ((0, NSA_PROJ_PAD - NSA_PROJ), (0, 0))).astype(BF16)
            w_rw = w_in[NSA_PROJ:].astype(BF16)
            qg = jnp.tile(prm["q_norm"][li], NSA_HEADS).reshape(1, 512)
            kg = jnp.stack([jnp.tile(prm["k_norm"][li, 1], NSA_KV), jnp.tile(prm["k_norm"][li, 2], NSA_KV)])
            qn, kv3, z_nsa, z_rw = _even_in(x, prm["norm_mix"][l], w_nsa, w_rw, qg, kg,
                                            g512.astype(BF16), g128.astype(BF16), tm)
            cw = _compress_weights(prm["phi_pe"][li], prm["phi_w1"][li], prm["phi_w2"][li],
                                   prm["k_norm"][li, 0])
            kv3b = kv3.reshape(b, t_len, 768)
            if prompt:
                cc = _compress(kv3b, *cw, g128)
                nsa = _nsa_prompt(qn, z_nsa, cc, kv3, bct, tbt, ovt, ext, b, t_len)
                z_prev = jnp.zeros((b, RWKV_PROJ), F32)
                s0 = jnp.zeros((b, RWKV_HEADS, HEAD_DIM, HEAD_DIM), F32)
                n_keep = min(WINDOW, t_len)
                win_rows = kv3b[:, t_len - n_keep:, 512:768]
            else:
                nsa, win_new_t = _nsa_paged(pt, pool_cmp, pool_slc, li, kv3, win_all_t, qn, z_nsa,
                                            cw + (g128,), bc, bs, bw, ov, ex, b, t_len)
                z_prev = st["shift"][:, li]
                s0 = st["wkv"][:, li]
                n_keep = WINDOW
                win_rows = jnp.swapaxes(win_new_t, 1, 2)
            pr = (prm["rwkv_mu"][li].reshape(1, -1), prm["rwkv_w0"][li].reshape(1, -1), prm["rwkv_w2"][li],
                  prm["rwkv_a0"][li].reshape(1, -1), prm["rwkv_a2"][li], prm["rwkv_g2"][li],
                  prm["rwkv_kk"][li].reshape(1, -1), prm["rwkv_ka"][li].reshape(1, -1),
                  prm["rwkv_rk"][li].reshape(1, -1), prm["rwkv_ln_w"][li].reshape(1, -1),
                  prm["rwkv_ln_b"][li].reshape(1, -1))
            nb_rw = max(n_ for n_ in ((2, 1) if prompt else (4, 2, 1)) if b % n_ == 0)
            rw, s_t = _rwkv(z_rw, z_prev, s0, pr, b, t_len, c_len, nb_rw)
            w_out = prm["w_out_even"][li].astype(BF16)
            if prompt:
                mix_pending = (nsa, rw.reshape(m_rows, 512), w_out)
            else:
                mix = jnp.concatenate([nsa.reshape(m_rows, 512), rw.reshape(m_rows, 512)], axis=1)
                x = _matmul_res(mix, w_out, x, tm, 512)
            cmp_r.append(kv3b[:, :, 0:256].reshape(b, t_len, 2, NSA_KV, HEAD_DIM))
            slc_r.append(kv3b[:, :, 256:512].reshape(b, t_len, 2, NSA_KV, HEAD_DIM))
            win_r.append(win_rows.reshape(b, n_keep, 2, NSA_KV, HEAD_DIM))
            wkv_r.append(s_t)
            sh_r.append(z_rw.reshape(b, t_len, RWKV_PROJ)[:, -1])
        else:
            prev = jnp.zeros((b, 2, D_MODEL), F32) if prompt else st["conv"][:, li]
            x, cs = _fused_block(x, prm["norm_mix"][l], prm["w_in_odd"][li].astype(BF16),
                                 prm["w_out_odd"][li].astype(BF16), prm["conv_w"][li], prev, "odd", t_len, tm_f)
            conv_r.append(cs)
        prev = jnp.zeros((b, 2, D_FF), F32) if prompt else st["ffn"][:, l]
        x, fs = _fused_block(x, prm["norm_ffn"][l], prm["ffn_up"][l].astype(BF16),
                             prm["ffn_down"][l].astype(BF16), prm["ffn_conv"][l], prev, "ffn", t_len, tm_f,
                             mix=mix_pending)
        mix_pending = None
        ffn_r.append(fs)
    stk = lambda a_: jnp.stack(a_, axis=1)
    return x.reshape(b, t_len, d), (stk(cmp_r), stk(slc_r), stk(win_r), stk(wkv_r), stk(sh_r),
                                    stk(conv_r), stk(ffn_r))


def kernel(x_prompt, x_sample, cache_cmp_kv, cache_slc_kv, cache_win_kv, state_rwkv_wkv, state_rwkv_shift, state_conv, state_ffn_conv, page_table, norm_mix, norm_ffn, rel_bias, w_in_even, w_out_even, q_norm, k_norm, phi_pe, phi_w1, phi_w2, rwkv_mu, rwkv_w0, rwkv_w2, rwkv_a0, rwkv_a2, rwkv_g2, rwkv_kk, rwkv_ka, rwkv_rk, rwkv_ln_w, rwkv_ln_b, w_in_odd, conv_w, w_out_odd, ffn_up, ffn_conv, ffn_down):
    prm = dict(norm_mix=norm_mix, norm_ffn=norm_ffn, rel_bias=rel_bias, w_in_even=w_in_even,
               w_out_even=w_out_even, q_norm=q_norm, k_norm=k_norm, phi_pe=phi_pe, phi_w1=phi_w1,
               phi_w2=phi_w2, rwkv_mu=rwkv_mu, rwkv_w0=rwkv_w0, rwkv_w2=rwkv_w2, rwkv_a0=rwkv_a0,
               rwkv_a2=rwkv_a2, rwkv_g2=rwkv_g2, rwkv_kk=rwkv_kk, rwkv_ka=rwkv_ka, rwkv_rk=rwkv_rk,
               rwkv_ln_w=rwkv_ln_w, rwkv_ln_b=rwkv_ln_b, w_in_odd=w_in_odd, conv_w=conv_w,
               w_out_odd=w_out_odd, ffn_up=ffn_up, ffn_conv=ffn_conv, ffn_down=ffn_down)
    st = dict(cmp=cache_cmp_kv, slc=cache_slc_kv, win=cache_win_kv, wkv=state_rwkv_wkv,
              shift=state_rwkv_shift, conv=state_conv, ffn=state_ffn_conv, page_table=page_table)
    y_p, (cmp_p, slc_p, win_p, wkv_p, sh_p, conv_p, ffn_p) = _trunk(x_prompt, prm, None)
    y_s, (cmp_s, slc_s, win_s, wkv_s, sh_s, conv_s, ffn_s) = _trunk(x_sample, prm, st)
    return (y_p, y_s, cmp_p, cmp_s, slc_p, slc_s, win_p, win_s, wkv_p, wkv_s,
            sh_p, sh_s, conv_p, conv_s, ffn_p, ffn_s)
```
